```python
import math
import jax, jax.numpy as jnp
from jax import lax
import numpy as np

D_MODEL = 2048
BATCH = 2
SEQ = 8192
DEPTH = 1

D_MIX = D_MODEL
D_SSM = D_MIX // 2
SSM_GROUP = 16
N_SSM_GROUPS = D_SSM // SSM_GROUP
SSM_STATE = 64
D_ATTN = D_MIX - D_SSM
HEAD_DIM = 128
N_HEADS = D_ATTN // HEAD_DIM
N_KV_HEADS = 2
KV_REP = N_HEADS // N_KV_HEADS
N_IDX_HEADS = 16
IDX_DIM = 64
TOPK_MAX = 256
Q_BLOCK = 128
ROPE_FRAC = 4
ROPE_THETA = 500000.0
D_FF = 4 * D_MODEL
EPS = 1e-6
N_MOD = 6
IN_SPLITS = (D_SSM, N_HEADS * HEAD_DIM, N_KV_HEADS * HEAD_DIM, N_KV_HEADS * HEAD_DIM,
             N_IDX_HEADS * IDX_DIM, IDX_DIM, N_IDX_HEADS)
D_IN = D_SSM + N_HEADS * HEAD_DIM + 2 * N_KV_HEADS * HEAD_DIM + N_IDX_HEADS * IDX_DIM + IDX_DIM + N_IDX_HEADS

kernel_name = "hymba_s5_dsa_hybrid_layer"


def rms_norm(x, g):
    xf = x.astype(jnp.float32)
    y = xf * lax.rsqrt(jnp.mean(xf * xf, axis=-1, keepdims=True) + EPS)
    return (y * g.astype(jnp.float32)).astype(x.dtype)


def partial_rope(x, pos):
    d = x.shape[-1]
    r = d // ROPE_FRAC
    half = r // 2
    inv = ROPE_THETA ** (-jnp.arange(half, dtype=jnp.float32) / half)
    ang = pos.astype(jnp.float32)[..., None] * inv
    cos = jnp.cos(ang)[:, :, None, :]
    sin = jnp.sin(ang)[:, :, None, :]
    x1 = x[..., :half].astype(jnp.float32)
    x2 = x[..., half:r].astype(jnp.float32)
    rot = jnp.concatenate([x1 * cos - x2 * sin, x1 * sin + x2 * cos], axis=-1).astype(x.dtype)
    return jnp.concatenate([rot, x[..., r:]], axis=-1)


def s5_mixer(u, lam_re, lam_im, log_dt, b_re, b_im, c_re, c_im, d_skip, w_glu, b_glu):
    bsz, seq, _ = u.shape
    f32 = jnp.float32
    uf = u.astype(f32).reshape(bsz, seq, N_SSM_GROUPS, SSM_GROUP)
    lr = jnp.minimum(lam_re.astype(f32), -1e-4)
    li = lam_im.astype(f32)
    dt = jnp.exp(log_dt.astype(f32))[:, None]
    mag = jnp.exp(lr * dt)
    abar_r = mag * jnp.cos(li * dt)
    abar_i = mag * jnp.sin(li * dt)
    den = lr * lr + li * li
    fr = ((abar_r - 1.0) * lr + abar_i * li) / den
    fi = (abar_i * lr - (abar_r - 1.0) * li) / den
    bu_r = jnp.einsum('blgh,gph->blgp', uf, b_re.astype(f32))
    bu_i = jnp.einsum('blgh,gph->blgp', uf, b_im.astype(f32))
    xr0 = fr * bu_r - fi * bu_i
    xi0 = fr * bu_i + fi * bu_r
    a_r = jnp.broadcast_to(abar_r, xr0.shape)
    a_i = jnp.broadcast_to(abar_i, xr0.shape)

    def combine(e1, e2):
        a1r, a1i, b1r, b1i = e1
        a2r, a2i, b2r, b2i = e2
        return (a2r * a1r - a2i * a1i,
                a2r * a1i + a2i * a1r,
                a2r * b1r - a2i * b1i + b2r,
                a2r * b1i + a2i * b1r + b2i)

    _, _, s_r, s_i = lax.associative_scan(combine, (a_r, a_i, xr0, xi0), axis=1)
    y = (jnp.einsum('blgp,ghp->blgh', s_r, c_re.astype(f32))
         - jnp.einsum('blgp,ghp->blgh', s_i, c_im.astype(f32)))
    y = y.reshape(bsz, seq, D_SSM) + d_skip.astype(f32) * u.astype(f32)
    ya = jax.nn.gelu(y)
    out = ya * jax.nn.sigmoid(ya @ w_glu.astype(f32) + b_glu.astype(f32))
    return out.astype(u.dtype)


def dsa_attention(q, k, v, q_idx, k_idx, w_idx, topk):
    f32 = jnp.float32
    bsz, seq = q.shape[:2]
    n_blocks = seq // Q_BLOCK
    key_pos = jnp.arange(seq)
    bidx = jnp.arange(bsz)[:, None, None]
    k_idx_f = k_idx.astype(f32)
    idx_scale = IDX_DIM ** -0.5
    head_w_scale = N_IDX_HEADS ** -0.5
    attn_scale = HEAD_DIM ** -0.5

    def block(i):
        start = i * Q_BLOCK
        qb = lax.dynamic_slice_in_dim(q, start, Q_BLOCK, axis=1)
        qib = lax.dynamic_slice_in_dim(q_idx, start, Q_BLOCK, axis=1).astype(f32)
        wb = lax.dynamic_slice_in_dim(w_idx, start, Q_BLOCK, axis=1).astype(f32) * head_w_scale
        logits = jnp.einsum('bqhd,bkd->bqhk', qib, k_idx_f) * idx_scale
        isc = jnp.einsum('bqhk,bqh->bqk', jax.nn.relu(logits), wb)
        qpos = start + jnp.arange(Q_BLOCK)
        causal = key_pos[None, :] <= qpos[:, None]
        isc = jnp.where(causal[None], isc, -jnp.inf)
        top_val, top_idx = lax.top_k(isc, topk)
        valid = jnp.isfinite(top_val)
        ks = k[bidx, top_idx].astype(f32)
        vs = v[bidx, top_idx].astype(f32)
        qg = qb.reshape(bsz, Q_BLOCK, N_KV_HEADS, KV_REP, HEAD_DIM).astype(f32)
        sc = jnp.einsum('bqgrd,bqkgd->bqgrk', qg, ks) * attn_scale
        sc = jnp.where(valid[:, :, None, None, :], sc, -jnp.inf)
        p = jax.nn.softmax(sc, axis=-1)
        o = jnp.einsum('bqgrk,bqkgd->bqgrd', p, vs)
        return o.reshape(bsz, Q_BLOCK, N_HEADS * HEAD_DIM).astype(q.dtype)

    out = lax.map(block, jnp.arange(n_blocks))
    return out.transpose(1, 0, 2, 3).reshape(bsz, seq, N_HEADS * HEAD_DIM)


def setup_inputs(seed: int = 0) -> dict:
    key = jax.random.key(seed)
    ks = jax.random.split(key, 32)
    f32 = jnp.float32
    nrm = lambda k, shape, s: jax.random.normal(k, shape, f32) * s
    G, P, H = N_SSM_GROUPS, SSM_STATE, SSM_GROUP
    x = jax.random.normal(ks[0], (BATCH, SEQ, D_MODEL), f32)
    c = jax.random.normal(ks[1], (BATCH, D_MODEL), f32)
    offset = jax.random.randint(ks[2], (BATCH, 1), 0, 1024)
    positions = (offset + jnp.arange(SEQ, dtype=jnp.int32)[None, :]).astype(jnp.int32)
    lam_im = jnp.broadcast_to(math.pi * jnp.arange(P, dtype=f32), (DEPTH, G, P))
    return {
        "x": x,
        "c": c,
        "positions": positions,
        "w_ada": nrm(ks[3], (DEPTH, D_MODEL, N_MOD * D_MODEL), 0.5 * D_MODEL ** -0.5),
        "b_ada": nrm(ks[4], (DEPTH, N_MOD * D_MODEL), 0.02),
        "g_norm_mix": 1.0 + nrm(ks[5], (DEPTH, D_MODEL), 0.02),
        "w_in": nrm(ks[6], (DEPTH, D_MODEL, D_IN), D_MODEL ** -0.5),
        "lam_re": -0.5 + nrm(ks[7], (DEPTH, G, P), 0.01),
        "lam_im": lam_im,
        "log_dt": jax.random.uniform(ks[8], (DEPTH, G), f32, math.log(1e-3), math.log(1e-1)),
        "b_re": nrm(ks[9], (DEPTH, G, P, H), (2 * H) ** -0.5),
        "b_im": nrm(ks[10], (DEPTH, G, P, H), (2 * H) ** -0.5),
        "c_re": nrm(ks[11], (DEPTH, G, H, P), (2 * P) ** -0.5),
        "c_im": nrm(ks[12], (DEPTH, G, H, P), (2 * P) ** -0.5),
        "d_skip": nrm(ks[13], (DEPTH, D_SSM), 1.0),
        "w_glu": nrm(ks[14], (DEPTH, D_SSM, D_SSM), D_SSM ** -0.5),
        "b_glu": nrm(ks[15], (DEPTH, D_SSM), 0.02),
        "g_q": 1.0 + nrm(ks[16], (DEPTH, HEAD_DIM), 0.02),
        "g_k": 1.0 + nrm(ks[17], (DEPTH, HEAD_DIM), 0.02),
        "g_out_ssm": 1.0 + nrm(ks[18], (DEPTH, D_SSM), 0.02),
        "g_out_attn": 1.0 + nrm(ks[19], (DEPTH, D_ATTN), 0.02),
        "w_out": nrm(ks[20], (DEPTH, D_MIX, D_MODEL), D_MIX ** -0.5),
        "g_norm_mlp": 1.0 + nrm(ks[21], (DEPTH, D_MODEL), 0.02),
        "w_mlp_in": nrm(ks[22], (DEPTH, D_MODEL, D_FF), D_MODEL ** -0.5),
        "w_mlp_out": nrm(ks[23], (DEPTH, D_FF, D_MODEL), D_FF ** -0.5),
    }


def reference(x, c, positions, w_ada, b_ada, g_norm_mix, w_in, lam_re, lam_im, log_dt,
              b_re, b_im, c_re, c_im, d_skip, w_glu, b_glu, g_q, g_k, g_out_ssm, g_out_attn,
              w_out, g_norm_mlp, w_mlp_in, w_mlp_out):
    bsz, seq, _ = x.shape
    topk = min(TOPK_MAX, seq // 4)
    split_pts = [int(s) for s in np.cumsum(IN_SPLITS)[:-1]]
    c_act = jax.nn.silu(c.astype(jnp.float32))
    for l in range(DEPTH):
        mod = c_act @ w_ada[l].astype(jnp.float32) + b_ada[l].astype(jnp.float32)
        sh_a, sc_a, gt_a, sh_m, sc_m, gt_m = [m[:, None, :] for m in jnp.split(mod, N_MOD, axis=-1)]

        h = (rms_norm(x, g_norm_mix[l]).astype(jnp.float32) * (1.0 + sc_a) + sh_a).astype(x.dtype)
        proj = h @ w_in[l]
        u_ssm, q, k, v, q_i, k_i, w_i = jnp.split(proj, split_pts, axis=-1)

        y_ssm = s5_mixer(u_ssm, lam_re[l], lam_im[l], log_dt[l], b_re[l], b_im[l],
                         c_re[l], c_im[l], d_skip[l], w_glu[l], b_glu[l])

        q = partial_rope(rms_norm(q.reshape(bsz, seq, N_HEADS, HEAD_DIM), g_q[l]), positions)
        k = partial_rope(rms_norm(k.reshape(bsz, seq, N_KV_HEADS, HEAD_DIM), g_k[l]), positions)
        v = v.reshape(bsz, seq, N_KV_HEADS, HEAD_DIM)
        q_i = partial_rope(q_i.reshape(bsz, seq, N_IDX_HEADS, IDX_DIM), positions)
        k_i = partial_rope(k_i.reshape(bsz, seq, 1, IDX_DIM), positions)[:, :, 0, :]
        y_attn = dsa_attention(q, k, v, q_i, k_i, w_i, topk)

        mix = jnp.concatenate([rms_norm(y_ssm, g_out_ssm[l]), rms_norm(y_attn, g_out_attn[l])], axis=-1)
        x = (x + gt_a * (mix @ w_out[l])).astype(x.dtype)

        h2 = (rms_norm(x, g_norm_mlp[l]).astype(jnp.float32) * (1.0 + sc_m) + sh_m).astype(x.dtype)
        ff = jnp.square(jax.nn.relu(h2 @ w_mlp_in[l])) @ w_mlp_out[l]
        x = (x + gt_m * ff).astype(x.dtype)
    return x
```

```python
import functools
import math

import numpy as np
import jax
import jax.numpy as jnp
from jax import lax
from jax.experimental import pallas as pl
from jax.experimental.pallas import tpu as pltpu

F32 = jnp.float32
BF16 = jnp.bfloat16
I32 = jnp.int32

D_MODEL = 2048
D_SSM = 1024
SSM_GROUP = 16
N_SSM_GROUPS = 64
SSM_STATE = 64
HEAD_DIM = 128
N_HEADS = 8
N_KV_HEADS = 2
KV_REP = N_HEADS // N_KV_HEADS
N_IDX_HEADS = 16
IDX_DIM = 64
TOPK_MAX = 256
ROPE_FRAC = 4
ROPE_THETA = 500000.0
D_FF = 4 * D_MODEL
EPS = 1e-6
N_MOD = 6

LANES = 128
VMEM_LIMIT = 56 * 1024 * 1024

ADA_TN = 1024
INPROJ_TM = 512
S5_CHUNK = 64
S5_GB = 16
DSA_TQ = 128
DSA_TK = 512
OUT_TM = 256
MLP_TM = 512
MLP_TF = 512

NEG_BIG = -1e30
INT_MIN = -2147483648
KEY_NEG_INF = -2139095041


def _cparams(sem):
    return pltpu.CompilerParams(dimension_semantics=sem, vmem_limit_bytes=VMEM_LIMIT)


def _const_spec(shape):
    nd = len(shape)
    return pl.BlockSpec(shape, lambda *_: (0,) * nd, pipeline_mode=pl.Buffered(1))


def _ada_kernel(c_ref, w_ref, b_ref, o_ref):
    c = c_ref[...]
    act = c / (1.0 + jnp.exp(-c))
    o_ref[...] = jnp.dot(act.astype(BF16), w_ref[...].astype(BF16),
                         preferred_element_type=F32) + b_ref[...]


def _ada_call(c_pad, w_ada, b_ada):
    rows, d = c_pad.shape
    n = w_ada.shape[1]
    return pl.pallas_call(
        _ada_kernel,
        grid=(n // ADA_TN,),
        in_specs=[pl.BlockSpec((rows, d), lambda j: (0, 0)),
                  pl.BlockSpec((d, ADA_TN), lambda j: (0, j)),
                  pl.BlockSpec((1, ADA_TN), lambda j: (0, j))],
        out_specs=pl.BlockSpec((rows, ADA_TN), lambda j: (0, j)),
        out_shape=jax.ShapeDtypeStruct((rows, n), F32),
        compiler_params=_cparams(("arbitrary",)),
        name="ada",
    )(c_pad, w_ada, b_ada)


def _rope(x, cos, sin_lo, sin_hi, half):
    n = x.shape[-1]
    return (x * cos + pltpu.roll(x, n - half, 1) * sin_lo + pltpu.roll(x, half, 1) * sin_hi)


def _inproj_kernel(x_ref, sc_ref, sh_ref, g_ref, pos_ref, inv_qk_ref, inv_ix_ref,
                   mlo_qk_ref, mhi_qk_ref, mlo_ix_ref, mhi_ix_ref, gq_ref, gk_ref,
                   wu_ref, wq_ref, wkv_ref, wqi_ref, wkw_ref,
                   u_out, q_out, k_out, v_out, qi_out, ki_out, wv_out):
    x = x_ref[...]
    ms = jnp.mean(x * x, axis=-1, keepdims=True)
    h = x * lax.rsqrt(ms + EPS) * g_ref[...]
    h = h * (1.0 + sc_ref[...]) + sh_ref[...]
    hb = h.astype(BF16)

    u_out[...] = jnp.dot(hb, wu_ref[...], preferred_element_type=F32).astype(BF16)

    pos = pos_ref[...].astype(F32)
    ang = pos * inv_qk_ref[...]
    cos_qk = jnp.cos(ang)
    sin_qk = jnp.sin(ang)
    slo_qk = -sin_qk * mlo_qk_ref[...]
    shi_qk = sin_qk * mhi_qk_ref[...]
    ang = pos * inv_ix_ref[...]
    cos_ix = jnp.cos(ang)
    sin_ix = jnp.sin(ang)
    slo_ix = -sin_ix * mlo_ix_ref[...]
    shi_ix = sin_ix * mhi_ix_ref[...]
    half_qk = HEAD_DIM // ROPE_FRAC // 2
    half_ix = IDX_DIM // ROPE_FRAC // 2

    def head_norm(t, g):
        return t * lax.rsqrt(jnp.mean(t * t, axis=-1, keepdims=True) + EPS) * g

    q = jnp.dot(hb, wq_ref[...], preferred_element_type=F32)
    attn_scale = HEAD_DIM ** -0.5
    for hd in range(N_HEADS):
        sl = slice(hd * HEAD_DIM, (hd + 1) * HEAD_DIM)
        t = _rope(head_norm(q[:, sl], gq_ref[...]), cos_qk, slo_qk, shi_qk, half_qk)
        q_out[:, sl] = (t * attn_scale).astype(BF16)

    kv = jnp.dot(hb, wkv_ref[...], preferred_element_type=F32)
    nk = N_KV_HEADS * HEAD_DIM
    for hd in range(N_KV_HEADS):
        sl = slice(hd * HEAD_DIM, (hd + 1) * HEAD_DIM)
        t = _rope(head_norm(kv[:, sl], gk_ref[...]), cos_qk, slo_qk, shi_qk, half_qk)
        k_out[:, sl] = t.astype(BF16)
    v_out[...] = kv[:, nk:].astype(BF16)

    qi = jnp.dot(hb, wqi_ref[...], preferred_element_type=F32)
    idx_scale = IDX_DIM ** -0.5
    for pr in range(N_IDX_HEADS * IDX_DIM // LANES):
        sl = slice(pr * LANES, (pr + 1) * LANES)
        t = _rope(qi[:, sl], cos_ix, slo_ix, shi_ix, half_ix)
        qi_out[:, sl] = (t * idx_scale).astype(BF16)

    kw = jnp.dot(hb, wkw_ref[...], preferred_element_type=F32)
    ki_out[...] = _rope(kw[:, :LANES], cos_ix, slo_ix, shi_ix, half_ix).astype(BF16)
    wv_out[...] = kw[:, LANES:] * (N_IDX_HEADS ** -0.5)


def _inproj_call(x2, sc, sh, g, pos2, tabs, gq, gk, wu, wq, wkv, wqi, wkw, seq):
    n, d = x2.shape
    tm = INPROJ_TM
    per_b = seq // tm
    row = lambda i: (i, 0)
    modmap = lambda i: (i // per_b, 0, 0)
    small = [_const_spec(t.shape) for t in tabs]
    in_specs = ([pl.BlockSpec((tm, d), row),
                 pl.BlockSpec((None, 1, d), modmap),
                 pl.BlockSpec((None, 1, d), modmap),
                 _const_spec(g.shape),
                 pl.BlockSpec((tm, 1), row)]
                + small
                + [_const_spec(gq.shape), _const_spec(gk.shape),
                   _const_spec(wu.shape), _const_spec(wq.shape), _const_spec(wkv.shape),
                   _const_spec(wqi.shape), _const_spec(wkw.shape)])
    widths = (D_SSM, N_HEADS * HEAD_DIM, N_KV_HEADS * HEAD_DIM, N_KV_HEADS * HEAD_DIM,
              N_IDX_HEADS * IDX_DIM, LANES, LANES)
    dtypes = (BF16, BF16, BF16, BF16, BF16, BF16, F32)
    return pl.pallas_call(
        _inproj_kernel,
        grid=(n // tm,),
        in_specs=in_specs,
        out_specs=[pl.BlockSpec((tm, w), row) for w in widths],
        out_shape=[jax.ShapeDtypeStruct((n, w), dt) for w, dt in zip(widths, dtypes)],
        compiler_params=_cparams(("parallel",)),
        name="inproj",
    )(x2, sc, sh, g, pos2, *tabs, gq, gk, wu, wq, wkv, wqi, wkw)


def _s5prep_kernel(lr_ref, li_ref, dt_ref, bre_ref, bim_ref, pr_ref, pi_ref, bbr_ref, bbi_ref):
    lr = jnp.minimum(lr_ref[...], -1e-4)
    li = li_ref[...]
    dt = dt_ref[...]
    mag = jnp.exp(lr * dt)
    ab_r = mag * jnp.cos(li * dt)
    ab_i = mag * jnp.sin(li * dt)
    den = lr * lr + li * li
    fr = ((ab_r - 1.0) * lr + ab_i * li) / den
    fi = (ab_i * lr - (ab_r - 1.0) * li) / den
    bbr_ref[...] = fr * bre_ref[...] - fi * bim_ref[...]
    bbi_ref[...] = fr * bim_ref[...] + fi * bre_ref[...]
    steps = (lax.broadcasted_iota(I32, pr_ref.shape, 0) + 1).astype(F32)
    magn = jnp.exp(steps * (lr * dt))
    pr_ref[...] = magn * jnp.cos(steps * (li * dt))
    pi_ref[...] = magn * jnp.sin(steps * (li * dt))


def _s5prep_call(lr, li, dt, bre_t, bim_t, chunk):
    gp = lr.shape[1]
    hh = bre_t.shape[0]
    outs = [jax.ShapeDtypeStruct((chunk, gp), F32)] * 2 + [jax.ShapeDtypeStruct((hh, gp), F32)] * 2
    return pl.pallas_call(_s5prep_kernel, out_shape=outs, name="s5prep")(lr, li, dt, bre_t, bim_t)


def _s5_kernel(u_ref, bw_ref, cw_ref, pw_ref, dsk_ref, y_ref, st_ref, cin_ref):
    tt, rr, _ = u_ref.shape
    ns = st_ref.shape[1] // 2
    ar = pw_ref[0:1, :ns]
    ai = pw_ref[0:1, ns:]
    st_ref[...] = jnp.zeros(st_ref.shape, F32)

    def cproj(zr, zi):
        return (jnp.dot(zr.astype(BF16), cw_ref[:ns, :], preferred_element_type=F32)
                + jnp.dot(zi.astype(BF16), cw_ref[ns:, :], preferred_element_type=F32))

    def local_step(i, carry):
        ui = u_ref[i]
        x = jnp.dot(ui, bw_ref[...], preferred_element_type=F32)
        sr = st_ref[:, :ns]
        si = st_ref[:, ns:]
        nr = ar * sr - ai * si + x[:, :ns]
        ni = ar * si + ai * sr + x[:, ns:]
        st_ref[:, :ns] = nr
        st_ref[:, ns:] = ni
        y_ref[i] = cproj(nr, ni) + dsk_ref[...] * ui.astype(F32)
        return carry

    lax.fori_loop(0, tt, local_step, 0)

    atr = pw_ref[tt - 1:tt, :ns]
    ati = pw_ref[tt - 1:tt, ns:]

    def chain(r, carry):
        cr, ci = carry
        cin_ref[pl.ds(r, 1), :ns] = cr
        cin_ref[pl.ds(r, 1), ns:] = ci
        er = st_ref[pl.ds(r, 1), :ns]
        ei = st_ref[pl.ds(r, 1), ns:]
        return (atr * cr - ati * ci + er, atr * ci + ati * cr + ei)

    zero = jnp.zeros((1, ns), F32)
    lax.fori_loop(0, rr, chain, (zero, zero))

    def carry_step(i, carry):
        pr = pw_ref[pl.ds(i, 1), :ns]
        pi = pw_ref[pl.ds(i, 1), ns:]
        cr = cin_ref[:, :ns]
        ci = cin_ref[:, ns:]
        y_ref[i] += cproj(pr * cr - pi * ci, pr * ci + pi * cr)
        return carry

    lax.fori_loop(0, tt, carry_step, 0)


def _s5_call(u_t, bw, cw, pw, dsk):
    bsz, tt, rr, dch = u_t.shape
    nb = bw.shape[0]
    cb = bw.shape[1]
    ns2 = bw.shape[2]
    return pl.pallas_call(
        _s5_kernel,
        grid=(bsz, nb),
        in_specs=[pl.BlockSpec((None, tt, rr, cb), lambda b, j: (b, 0, 0, j)),
                  pl.BlockSpec((None, cb, ns2), lambda b, j: (j, 0, 0)),
                  pl.BlockSpec((None, ns2, cb), lambda b, j: (j, 0, 0)),
                  pl.BlockSpec((None, tt, ns2), lambda b, j: (j, 0, 0)),
                  pl.BlockSpec((1, cb), lambda b, j: (0, j))],
        out_specs=pl.BlockSpec((None, tt, rr, cb), lambda b, j: (b, 0, 0, j)),
        out_shape=jax.ShapeDtypeStruct((bsz, tt, rr, dch), F32),
        scratch_shapes=[pltpu.VMEM((rr, ns2), F32), pltpu.VMEM((rr, ns2), F32)],
        compiler_params=_cparams(("parallel", "parallel")),
        name="s5scan",
    )(u_t, bw, cw, pw, dsk)


def _order_key(x):
    bits = pltpu.bitcast(x, I32)
    return jnp.where(bits < 0, bits ^ jnp.int32(0x7FFFFFFF), bits)


def _dsa_kernel(q_ref, qi_ref, wv_ref, k_ref, v_ref, ki_ref, o_ref,
                key_sc, lhs_sc, wb_sc, qs_sc, m_sc, l_sc, acc_sc, *, topk):
    tq = q_ref.shape[0]
    tk = key_sc.shape[2]
    qb = pl.program_id(1)
    nkt = (qb * tq + tq + tk - 1) // tk
    nsub = tk // LANES

    lane = lax.broadcasted_iota(I32, (tq, LANES), 1)
    for hd in range(N_IDX_HEADS):
        pair = qi_ref[:, (hd // 2) * LANES:(hd // 2 + 1) * LANES]
        keep = (lane < IDX_DIM) if hd % 2 == 0 else (lane >= IDX_DIM)
        lhs_sc[hd * tq:(hd + 1) * tq, :] = jnp.where(keep, pair, jnp.zeros_like(pair))
        wb_sc[hd] = jnp.broadcast_to(wv_ref[:, hd:hd + 1], (tq, LANES))
    for g in range(N_KV_HEADS):
        for r in range(KV_REP):
            hd = g * KV_REP + r
            qs_sc[g, r * tq:(r + 1) * tq, :] = q_ref[:, hd * HEAD_DIM:(hd + 1) * HEAD_DIM]

    qpos = qb * tq + lax.broadcasted_iota(I32, (tq, tk), 0)
    kiota = lax.broadcasted_iota(I32, (tq, tk), 1)

    def score_tile(kt, carry):
        k0 = pl.multiple_of(kt * tk, tk)
        kit = ki_ref[pl.ds(k0, tk), :]
        acc = jnp.zeros((tq, tk), F32)
        for hd in range(N_IDX_HEADS):
            lg = lax.dot_general(lhs_sc[hd * tq:(hd + 1) * tq, :], kit,
                                 (((1,), (1,)), ((), ())), preferred_element_type=F32)
            wb = wb_sc[hd]
            wbt = jnp.concatenate([wb] * nsub, axis=1)
            acc = acc + jnp.maximum(lg, 0.0) * wbt
        acc = jnp.where(kiota + k0 <= qpos, acc, -jnp.inf)
        key_sc[kt] = _order_key(acc)
        return carry

    lax.fori_loop(0, nkt, score_tile, 0)

    def count_ge(cand):
        cb = jnp.broadcast_to(cand, (tq, LANES))

        def body(kt, acc):
            kk = key_sc[kt]
            for c in range(nsub):
                acc = acc + jnp.where(kk[:, c * LANES:(c + 1) * LANES] >= cb, 1, 0).astype(I32)
            return acc

        acc = lax.fori_loop(0, nkt, body, jnp.zeros((tq, LANES), I32))
        return jnp.sum(acc, axis=1, keepdims=True)

    c0 = count_ge(jnp.zeros((tq, 1), I32))
    pos_ok = c0 >= topk
    t0 = jnp.where(pos_ok, 0, INT_MIN).astype(I32)
    n0 = jnp.where(pos_ok, c0, nkt * tk).astype(I32)

    def search_cond(st):
        bit, _, cnt = st
        return jnp.logical_and(bit >= 0, jnp.max(jnp.abs(cnt - topk)) > 0)

    def search_body(st):
        bit, t, cnt = st
        cand = t + jnp.left_shift(jnp.int32(1), bit)
        c = count_ge(cand)
        ok = c >= topk
        return bit - 1, jnp.where(ok, cand, t), jnp.where(ok, c, cnt)

    _, thr, _ = lax.while_loop(search_cond, search_body, (jnp.int32(30), t0, n0))
    thr = jnp.maximum(thr, KEY_NEG_INF + 1)
    thr_b = jnp.broadcast_to(thr, (tq, tk))

    m_sc[...] = jnp.full(m_sc.shape, NEG_BIG, F32)
    l_sc[...] = jnp.zeros(l_sc.shape, F32)
    acc_sc[...] = jnp.zeros(acc_sc.shape, F32)

    def attn_tile(kt, carry):
        k0 = pl.multiple_of(kt * tk, tk)
        sel = key_sc[kt] >= thr_b
        for g in range(N_KV_HEADS):
            kg = k_ref[pl.ds(k0, tk), g * HEAD_DIM:(g + 1) * HEAD_DIM]
            vg = v_ref[pl.ds(k0, tk), g * HEAD_DIM:(g + 1) * HEAD_DIM]
            s = lax.dot_general(qs_sc[g], kg, (((1,), (1,)), ((), ())),
                                preferred_element_type=F32)
            s = jnp.where(sel[None], s.reshape(KV_REP, tq, tk), NEG_BIG).reshape(KV_REP * tq, tk)
            m_old = m_sc[g]
            m_new = jnp.maximum(m_old, jnp.max(s, axis=-1, keepdims=True))
            alpha = jnp.exp(m_old - m_new)
            p = jnp.exp(s - m_new)
            l_sc[g] = alpha * l_sc[g] + jnp.sum(p, axis=-1, keepdims=True)
            acc_sc[g] = alpha * acc_sc[g] + jnp.dot(p.astype(BF16), vg,
                                                    preferred_element_type=F32)
            m_sc[g] = m_new
        return carry

    lax.fori_loop(0, nkt, attn_tile, 0)

    for g in range(N_KV_HEADS):
        out = acc_sc[g] / l_sc[g]
        for r in range(KV_REP):
            hd = g * KV_REP + r
            o_ref[:, hd * HEAD_DIM:(hd + 1) * HEAD_DIM] = out[r * tq:(r + 1) * tq].astype(o_ref.dtype)


def _dsa_call(q, qi, wv, k, v, ki, topk):
    bsz, seq, dq = q.shape
    tq, tk = DSA_TQ, min(DSA_TK, seq)
    nq = seq // tq
    qmap = lambda b, i: (b, i, 0)
    bmap = lambda b, i: (b, 0, 0)
    return pl.pallas_call(
        functools.partial(_dsa_kernel, topk=topk),
        grid=(bsz, nq),
        in_specs=[pl.BlockSpec((None, tq, dq), qmap),
                  pl.BlockSpec((None, tq, qi.shape[2]), qmap),
                  pl.BlockSpec((None, tq, wv.shape[2]), qmap),
                  pl.BlockSpec((None, seq, k.shape[2]), bmap),
                  pl.BlockSpec((None, seq, v.shape[2]), bmap),
                  pl.BlockSpec((None, seq, ki.shape[2]), bmap)],
        out_specs=pl.BlockSpec((None, tq, dq), qmap),
        out_shape=jax.ShapeDtypeStruct((bsz, seq, dq), BF16),
        scratch_shapes=[pltpu.VMEM((seq // tk, tq, tk), I32),
                        pltpu.VMEM((N_IDX_HEADS * tq, LANES), BF16),
                        pltpu.VMEM((N_IDX_HEADS, tq, LANES), F32),
                        pltpu.VMEM((N_KV_HEADS, KV_REP * tq, HEAD_DIM), BF16),
                        pltpu.VMEM((N_KV_HEADS, KV_REP * tq, 1), F32),
                        pltpu.VMEM((N_KV_HEADS, KV_REP * tq, 1), F32),
                        pltpu.VMEM((N_KV_HEADS, KV_REP * tq, HEAD_DIM), F32)],
        compiler_params=_cparams(("parallel", "arbitrary")),
        name="dsa",
    )(q, qi, wv, k, v, ki)


def _outproj_kernel(x_ref, ys_ref, ya_ref, gt_ref, sc_ref, sh_ref, wglu_ref, bglu_ref,
                    gs_ref, ga_ref, wout_ref, gm_ref, x1_out, h2_out):
    def rms(t, g):
        return t * lax.rsqrt(jnp.mean(t * t, axis=-1, keepdims=True) + EPS) * g

    y = ys_ref[...]
    ya = 0.5 * y * (1.0 + jnp.tanh(math.sqrt(2.0 / math.pi) * (y + 0.044715 * (y * y * y))))
    z = jnp.dot(ya.astype(BF16), wglu_ref[...], preferred_element_type=F32) + bglu_ref[...]
    o = ya / (1.0 + jnp.exp(-z))
    n1 = rms(o, gs_ref[...]).astype(BF16)
    n2 = rms(ya_ref[...].astype(F32), ga_ref[...]).astype(BF16)
    d1 = n1.shape[1]
    mixw = (jnp.dot(n1, wout_ref[:d1, :], preferred_element_type=F32)
            + jnp.dot(n2, wout_ref[d1:, :], preferred_element_type=F32))
    x1 = x_ref[...] + gt_ref[...] * mixw
    x1_out[...] = x1
    h2 = rms(x1, gm_ref[...]) * (1.0 + sc_ref[...]) + sh_ref[...]
    h2_out[...] = h2.astype(BF16)


def _outproj_call(x2, ys, ya, gt, sc, sh, wglu, bglu, gs, ga, wout, gm, seq):
    n, d = x2.shape
    tm = OUT_TM
    per_b = seq // tm
    row = lambda i: (i, 0)
    modmap = lambda i: (i // per_b, 0, 0)
    return pl.pallas_call(
        _outproj_kernel,
        grid=(n // tm,),
        in_specs=[pl.BlockSpec((tm, d), row),
                  pl.BlockSpec((tm, ys.shape[1]), row),
                  pl.BlockSpec((tm, ya.shape[1]), row),
                  pl.BlockSpec((None, 1, d), modmap),
                  pl.BlockSpec((None, 1, d), modmap),
                  pl.BlockSpec((None, 1, d), modmap),
                  _const_spec(wglu.shape), _const_spec(bglu.shape),
                  _const_spec(gs.shape), _const_spec(ga.shape),
                  _const_spec(wout.shape), _const_spec(gm.shape)],
        out_specs=[pl.BlockSpec((tm, d), row), pl.BlockSpec((tm, d), row)],
        out_shape=[jax.ShapeDtypeStruct((n, d), F32), jax.ShapeDtypeStruct((n, d), BF16)],
        compiler_params=_cparams(("parallel",)),
        name="outproj",
    )(x2, ys, ya, gt, sc, sh, wglu, bglu, gs, ga, wout, gm)


def _mlp_kernel(h_ref, x1_ref, gt_ref, w1_ref, w2_ref, o_ref):
    j = pl.program_id(1)
    a = jnp.maximum(jnp.dot(h_ref[...], w1_ref[...], preferred_element_type=F32), 0.0)
    part = jnp.dot((a * a).astype(BF16), w2_ref[...], preferred_element_type=F32)

    @pl.when(j == 0)
    def _():
        o_ref[...] = part

    @pl.when(j > 0)
    def _():
        o_ref[...] += part

    @pl.when(j == pl.num_programs(1) - 1)
    def _():
        o_ref[...] = x1_ref[...] + gt_ref[...] * o_ref[...]


def _mlp_call(h2, x1, gt, w1, w2, seq):
    n, d = x1.shape
    dff = w1.shape[1]
    tm, tf = MLP_TM, MLP_TF
    per_b = seq // tm
    return pl.pallas_call(
        _mlp_kernel,
        grid=(n // tm, dff // tf),
        in_specs=[pl.BlockSpec((tm, d), lambda i, j: (i, 0)),
                  pl.BlockSpec((tm, d), lambda i, j: (i, 0)),
                  pl.BlockSpec((None, 1, d), lambda i, j: (i // per_b, 0, 0)),
                  pl.BlockSpec((d, tf), lambda i, j: (0, j)),
                  pl.BlockSpec((tf, d), lambda i, j: (j, 0))],
        out_specs=pl.BlockSpec((tm, d), lambda i, j: (i, 0)),
        out_shape=jax.ShapeDtypeStruct((n, d), F32),
        compiler_params=_cparams(("parallel", "arbitrary")),
        name="mlp",
    )(h2, x1, gt, w1, w2)


def _rope_tables():
    def tables(head_dim, per_vreg):
        r = head_dim // ROPE_FRAC
        half = r // 2
        inv = ROPE_THETA ** (-jnp.arange(half, dtype=F32) / half)
        lane = np.arange(LANES) % head_dim
        inv_l = jnp.where(lane < r, inv[lane % half], 0.0).astype(F32)[None, :]
        lo = jnp.asarray((lane < half).astype(np.float32))[None, :]
        hi = jnp.asarray(((lane >= half) & (lane < r)).astype(np.float32))[None, :]
        return inv_l, lo, hi

    inv_qk, lo_qk, hi_qk = tables(HEAD_DIM, 1)
    inv_ix, lo_ix, hi_ix = tables(IDX_DIM, 2)
    return [inv_qk, inv_ix, lo_qk, hi_qk, lo_ix, hi_ix]


def _block_diag(m):
    nb, g, a, b = m.shape
    eye = jnp.eye(g, dtype=m.dtype)
    return jnp.einsum('ngab,gk->ngakb', m, eye).reshape(nb, g * a, g * b)


def kernel(x, c, positions, w_ada, b_ada, g_norm_mix, w_in, lam_re, lam_im, log_dt,
           b_re, b_im, c_re, c_im, d_skip, w_glu, b_glu, g_q, g_k, g_out_ssm, g_out_attn,
           w_out, g_norm_mlp, w_mlp_in, w_mlp_out):
    bsz, seq, d = x.shape
    depth = w_ada.shape[0]
    n = bsz * seq
    topk = min(TOPK_MAX, seq // 4)
    gg, pp, hh = N_SSM_GROUPS, SSM_STATE, SSM_GROUP
    nblk = gg // S5_GB
    chunk = S5_CHUNK
    nchunks = seq // chunk

    c_pad = jnp.zeros((8, d), F32).at[:bsz].set(c.astype(F32))
    pos2 = positions.reshape(n, 1).astype(I32)
    tabs = _rope_tables()
    xcur = x.reshape(n, d).astype(F32)

    for l in range(depth):
        mod = _ada_call(c_pad, w_ada[l], b_ada[l][None, :])[:bsz]
        mod = mod.reshape(bsz, N_MOD, 1, d)
        sh_a, sc_a, gt_a, sh_m, sc_m, gt_m = [mod[:, i] for i in range(N_MOD)]

        wi = w_in[l]
        o0 = D_SSM
        o1 = o0 + N_HEADS * HEAD_DIM
        o2 = o1 + 2 * N_KV_HEADS * HEAD_DIM
        o3 = o2 + N_IDX_HEADS * IDX_DIM
        o4 = o3 + IDX_DIM
        w_ki = wi[:, o3:o4]
        w_wi = wi[:, o4:]
        w_kw = jnp.concatenate(
            [w_ki, w_ki, w_wi, jnp.zeros((d, 2 * LANES - 2 * IDX_DIM - N_IDX_HEADS), wi.dtype)], axis=1)
        u, q, k, v, qi, ki, wv = _inproj_call(
            xcur, sc_a, sh_a, g_norm_mix[l][None, :], pos2, tabs,
            g_q[l][None, :], g_k[l][None, :],
            wi[:, :o0].astype(BF16), wi[:, o0:o1].astype(BF16), wi[:, o1:o2].astype(BF16),
            wi[:, o2:o3].astype(BF16), w_kw.astype(BF16), seq)

        lr = lam_re[l].astype(F32).reshape(1, gg * pp)
        li = lam_im[l].astype(F32).reshape(1, gg * pp)
        dt = jnp.broadcast_to(jnp.exp(log_dt[l].astype(F32))[:, None], (gg, pp)).reshape(1, gg * pp)
        bre_t = b_re[l].astype(F32).transpose(2, 0, 1).reshape(hh, gg * pp)
        bim_t = b_im[l].astype(F32).transpose(2, 0, 1).reshape(hh, gg * pp)
        pw_r, pw_i, bb_r, bb_i = _s5prep_call(lr, li, dt, bre_t, bim_t, chunk)
        bbr = bb_r.reshape(hh, nblk, S5_GB, pp).transpose(1, 2, 0, 3)
        bbi = bb_i.reshape(hh, nblk, S5_GB, pp).transpose(1, 2, 0, 3)
        bw = jnp.concatenate([_block_diag(bbr), _block_diag(bbi)], axis=2).astype(BF16)
        cr = c_re[l].astype(F32).reshape(nblk, S5_GB, hh, pp).transpose(0, 1, 3, 2)
        ci = c_im[l].astype(F32).reshape(nblk, S5_GB, hh, pp).transpose(0, 1, 3, 2)
        cw = jnp.concatenate([_block_diag(cr), -_block_diag(ci)], axis=1).astype(BF16)
        ns = S5_GB * pp
        pw = jnp.concatenate([pw_r.reshape(chunk, nblk, ns), pw_i.reshape(chunk, nblk, ns)],
                             axis=2).transpose(1, 0, 2)
        u_t = u.reshape(bsz, nchunks, chunk, D_SSM).transpose(0, 2, 1, 3)
        y_t = _s5_call(u_t, bw, cw, pw, d_skip[l].astype(F32)[None, :])
        y_ssm = y_t.transpose(0, 2, 1, 3).reshape(n, D_SSM)

        y_attn = _dsa_call(q.reshape(bsz, seq, -1), qi.reshape(bsz, seq, -1),
                           wv.reshape(bsz, seq, -1), k.reshape(bsz, seq, -1),
                           v.reshape(bsz, seq, -1), ki.reshape(bsz, seq, -1), topk)
        y_attn = y_attn.reshape(n, -1)

        x1, h2 = _outproj_call(
            xcur, y_ssm, y_attn, gt_a, sc_m, sh_m,
            w_glu[l].astype(BF16), b_glu[l].astype(F32)[None, :],
            g_out_ssm[l].astype(F32)[None, :], g_out_attn[l].astype(F32)[None, :],
            w_out[l].astype(BF16), g_norm_mlp[l].astype(F32)[None, :], seq)

        xcur = _mlp_call(h2, x1, gt_m, w_mlp_in[l].astype(BF16), w_mlp_out[l].astype(BF16), seq)

    return xcur.reshape(bsz, seq, d).astype(x.dtype)
```

```python
import functools
import math

import numpy as np
import jax
import jax.numpy as jnp
from jax import lax
from jax.experimental import pallas as pl
from jax.experimental.pallas import tpu as pltpu

F32 = jnp.float32
BF16 = jnp.bfloat16
I32 = jnp.int32

D_MODEL = 2048
D_SSM = 1024
SSM_GROUP = 16
N_SSM_GROUPS = 64
SSM_STATE = 64
HEAD_DIM = 128
N_HEADS = 8
N_KV_HEADS = 2
KV_REP = N_HEADS // N_KV_HEADS
N_IDX_HEADS = 16
IDX_DIM = 64
TOPK_MAX = 256
ROPE_FRAC = 4
ROPE_THETA = 500000.0
D_FF = 4 * D_MODEL
EPS = 1e-6
N_MOD = 6

LANES = 128
VMEM_LIMIT = 56 * 1024 * 1024

ADA_TN = 1024
INPROJ_TM = 512
S5_CHUNK = 64
S5_GB = 16
DSA_TQ = 256
DSA_TK = 256
DSA_ONES = 16
OUT_TM = 256
MLP_TM = 1024
MLP_TF = 512

NEG_BIG = -1e30
INT_MIN = -2147483648
KEY_NEG_INF = -2139095041


def _cparams(sem):
    return pltpu.CompilerParams(dimension_semantics=sem, vmem_limit_bytes=VMEM_LIMIT)


def _const_spec(shape):
    nd = len(shape)
    return pl.BlockSpec(shape, lambda *_: (0,) * nd, pipeline_mode=pl.Buffered(1))


def _ada_kernel(c_ref, w_ref, b_ref, o_ref):
    c = c_ref[...]
    act = c / (1.0 + jnp.exp(-c))
    o_ref[...] = jnp.dot(act.astype(BF16), w_ref[...].astype(BF16),
                         preferred_element_type=F32) + b_ref[...]


def _ada_call(c_pad, w_ada, b_ada):
    rows, d = c_pad.shape
    n = w_ada.shape[1]
    return pl.pallas_call(
        _ada_kernel,
        grid=(n // ADA_TN,),
        in_specs=[pl.BlockSpec((rows, d), lambda j: (0, 0)),
                  pl.BlockSpec((d, ADA_TN), lambda j: (0, j)),
                  pl.BlockSpec((1, ADA_TN), lambda j: (0, j))],
        out_specs=pl.BlockSpec((rows, ADA_TN), lambda j: (0, j)),
        out_shape=jax.ShapeDtypeStruct((rows, n), F32),
        compiler_params=_cparams(("arbitrary",)),
        name="ada",
    )(c_pad, w_ada, b_ada)


def _rope(x, cos, sin_lo, sin_hi, half):
    n = x.shape[-1]
    return (x * cos + pltpu.roll(x, n - half, 1) * sin_lo + pltpu.roll(x, half, 1) * sin_hi)


def _inproj_kernel(x_ref, sc_ref, sh_ref, g_ref, pos_ref, inv_qk_ref, inv_ix_ref,
                   mlo_qk_ref, mhi_qk_ref, mlo_ix_ref, mhi_ix_ref, gq_ref, gk_ref,
                   wu_ref, wq_ref, wkv_ref, wqi_ref, wkw_ref,
                   u_out, q_out, k_out, v_out, qi_out, ki_out, wv_out):
    x = x_ref[...]
    ms = jnp.mean(x * x, axis=-1, keepdims=True)
    h = x * lax.rsqrt(ms + EPS) * g_ref[...]
    h = h * (1.0 + sc_ref[...]) + sh_ref[...]
    hb = h.astype(BF16)

    u_out[...] = jnp.dot(hb, wu_ref[...], preferred_element_type=F32).astype(BF16)

    pos = pos_ref[...].astype(F32)
    ang = pos * inv_qk_ref[...]
    cos_qk = jnp.cos(ang)
    sin_qk = jnp.sin(ang)
    slo_qk = -sin_qk * mlo_qk_ref[...]
    shi_qk = sin_qk * mhi_qk_ref[...]
    ang = pos * inv_ix_ref[...]
    cos_ix = jnp.cos(ang)
    sin_ix = jnp.sin(ang)
    slo_ix = -sin_ix * mlo_ix_ref[...]
    shi_ix = sin_ix * mhi_ix_ref[...]
    half_qk = HEAD_DIM // ROPE_FRAC // 2
    half_ix = IDX_DIM // ROPE_FRAC // 2

    def head_norm(t, g):
        return t * lax.rsqrt(jnp.mean(t * t, axis=-1, keepdims=True) + EPS) * g

    q = jnp.dot(hb, wq_ref[...], preferred_element_type=F32)
    attn_scale = HEAD_DIM ** -0.5
    for hd in range(N_HEADS):
        sl = slice(hd * HEAD_DIM, (hd + 1) * HEAD_DIM)
        t = _rope(head_norm(q[:, sl], gq_ref[...]), cos_qk, slo_qk, shi_qk, half_qk)
        q_out[:, sl] = (t * attn_scale).astype(BF16)

    kv = jnp.dot(hb, wkv_ref[...], preferred_element_type=F32)
    nk = N_KV_HEADS * HEAD_DIM
    for hd in range(N_KV_HEADS):
        sl = slice(hd * HEAD_DIM, (hd + 1) * HEAD_DIM)
        t = _rope(head_norm(kv[:, sl], gk_ref[...]), cos_qk, slo_qk, shi_qk, half_qk)
        k_out[:, sl] = t.astype(BF16)
    v_out[...] = kv[:, nk:].astype(BF16)

    qi = jnp.dot(hb, wqi_ref[...], preferred_element_type=F32)
    idx_scale = IDX_DIM ** -0.5
    for pr in range(N_IDX_HEADS * IDX_DIM // LANES):
        sl = slice(pr * LANES, (pr + 1) * LANES)
        t = _rope(qi[:, sl], cos_ix, slo_ix, shi_ix, half_ix)
        qi_out[:, sl] = (t * idx_scale).astype(BF16)

    kw = jnp.dot(hb, wkw_ref[...], preferred_element_type=F32)
    ki_out[...] = _rope(kw[:, :LANES], cos_ix, slo_ix, shi_ix, half_ix).astype(BF16)
    wv_out[...] = kw[:, LANES:] * (N_IDX_HEADS ** -0.5)


def _inproj_call(x2, sc, sh, g, pos2, tabs, gq, gk, wu, wq, wkv, wqi, wkw, seq):
    n, d = x2.shape
    tm = INPROJ_TM
    per_b = seq // tm
    row = lambda i: (i, 0)
    modmap = lambda i: (i // per_b, 0, 0)
    small = [_const_spec(t.shape) for t in tabs]
    in_specs = ([pl.BlockSpec((tm, d), row),
                 pl.BlockSpec((None, 1, d), modmap),
                 pl.BlockSpec((None, 1, d), modmap),
                 _const_spec(g.shape),
                 pl.BlockSpec((tm, 1), row)]
                + small
                + [_const_spec(gq.shape), _const_spec(gk.shape),
                   _const_spec(wu.shape), _const_spec(wq.shape), _const_spec(wkv.shape),
                   _const_spec(wqi.shape), _const_spec(wkw.shape)])
    widths = (D_SSM, N_HEADS * HEAD_DIM, N_KV_HEADS * HEAD_DIM, N_KV_HEADS * HEAD_DIM,
              N_IDX_HEADS * IDX_DIM, LANES, LANES)
    dtypes = (BF16, BF16, BF16, BF16, BF16, BF16, F32)
    return pl.pallas_call(
        _inproj_kernel,
        grid=(n // tm,),
        in_specs=in_specs,
        out_specs=[pl.BlockSpec((tm, w), row) for w in widths],
        out_shape=[jax.ShapeDtypeStruct((n, w), dt) for w, dt in zip(widths, dtypes)],
        compiler_params=_cparams(("parallel",)),
        name="inproj",
    )(x2, sc, sh, g, pos2, *tabs, gq, gk, wu, wq, wkv, wqi, wkw)


def _s5prep_kernel(lr_ref, li_ref, dt_ref, bre_ref, bim_ref, pr_ref, pi_ref, bbr_ref, bbi_ref):
    lr = jnp.minimum(lr_ref[...], -1e-4)
    li = li_ref[...]
    dt = dt_ref[...]
    mag = jnp.exp(lr * dt)
    ab_r = mag * jnp.cos(li * dt)
    ab_i = mag * jnp.sin(li * dt)
    den = lr * lr + li * li
    fr = ((ab_r - 1.0) * lr + ab_i * li) / den
    fi = (ab_i * lr - (ab_r - 1.0) * li) / den
    bbr_ref[...] = fr * bre_ref[...] - fi * bim_ref[...]
    bbi_ref[...] = fr * bim_ref[...] + fi * bre_ref[...]
    steps = (lax.broadcasted_iota(I32, pr_ref.shape, 0) + 1).astype(F32)
    magn = jnp.exp(steps * (lr * dt))
    pr_ref[...] = magn * jnp.cos(steps * (li * dt))
    pi_ref[...] = magn * jnp.sin(steps * (li * dt))


def _s5prep_call(lr, li, dt, bre_t, bim_t, chunk):
    gp = lr.shape[1]
    hh = bre_t.shape[0]
    outs = [jax.ShapeDtypeStruct((chunk, gp), F32)] * 2 + [jax.ShapeDtypeStruct((hh, gp), F32)] * 2
    return pl.pallas_call(_s5prep_kernel, out_shape=outs, name="s5prep")(lr, li, dt, bre_t, bim_t)


def _s5_kernel(u_ref, bw_ref, cw_ref, pw_ref, dsk_ref, y_ref, st_ref, cin_ref):
    tt, rr, _ = u_ref.shape
    ns = st_ref.shape[1] // 2
    ar = pw_ref[0:1, :ns]
    ai = pw_ref[0:1, ns:]
    st_ref[...] = jnp.zeros(st_ref.shape, F32)

    def cproj(zr, zi):
        return (jnp.dot(zr.astype(BF16), cw_ref[:ns, :], preferred_element_type=F32)
                + jnp.dot(zi.astype(BF16), cw_ref[ns:, :], preferred_element_type=F32))

    def local_step(i, carry):
        ui = u_ref[i]
        x = jnp.dot(ui, bw_ref[...], preferred_element_type=F32)
        sr = st_ref[:, :ns]
        si = st_ref[:, ns:]
        nr = ar * sr - ai * si + x[:, :ns]
        ni = ar * si + ai * sr + x[:, ns:]
        st_ref[:, :ns] = nr
        st_ref[:, ns:] = ni
        y_ref[i] = cproj(nr, ni) + dsk_ref[...] * ui.astype(F32)
        return carry

    lax.fori_loop(0, tt, local_step, 0)

    atr = pw_ref[tt - 1:tt, :ns]
    ati = pw_ref[tt - 1:tt, ns:]

    def chain(r, carry):
        cr, ci = carry
        cin_ref[pl.ds(r, 1), :ns] = cr
        cin_ref[pl.ds(r, 1), ns:] = ci
        er = st_ref[pl.ds(r, 1), :ns]
        ei = st_ref[pl.ds(r, 1), ns:]
        return (atr * cr - ati * ci + er, atr * ci + ati * cr + ei)

    zero = jnp.zeros((1, ns), F32)
    lax.fori_loop(0, rr, chain, (zero, zero))

    def carry_step(i, carry):
        pr = pw_ref[pl.ds(i, 1), :ns]
        pi = pw_ref[pl.ds(i, 1), ns:]
        cr = cin_ref[:, :ns]
        ci = cin_ref[:, ns:]
        y_ref[i] += cproj(pr * cr - pi * ci, pr * ci + pi * cr)
        return carry

    lax.fori_loop(0, tt, carry_step, 0)


def _s5_call(u_t, bw, cw, pw, dsk):
    bsz, tt, rr, dch = u_t.shape
    nb = bw.shape[0]
    cb = bw.shape[1]
    ns2 = bw.shape[2]
    return pl.pallas_call(
        _s5_kernel,
        grid=(bsz, nb),
        in_specs=[pl.BlockSpec((None, tt, rr, cb), lambda b, j: (b, 0, 0, j)),
                  pl.BlockSpec((None, cb, ns2), lambda b, j: (j, 0, 0)),
                  pl.BlockSpec((None, ns2, cb), lambda b, j: (j, 0, 0)),
                  pl.BlockSpec((None, tt, ns2), lambda b, j: (j, 0, 0)),
                  pl.BlockSpec((1, cb), lambda b, j: (0, j))],
        out_specs=pl.BlockSpec((None, tt, rr, cb), lambda b, j: (b, 0, 0, j)),
        out_shape=jax.ShapeDtypeStruct((bsz, tt, rr, dch), F32),
        scratch_shapes=[pltpu.VMEM((rr, ns2), F32), pltpu.VMEM((rr, ns2), F32)],
        compiler_params=_cparams(("parallel", "parallel")),
        name="s5scan",
    )(u_t, bw, cw, pw, dsk)


def _order_key(x):
    bits = pltpu.bitcast(x, I32)
    return jnp.where(bits < 0, bits ^ jnp.int32(0x7FFFFFFF), bits)


def _dsa_kernel(q_ref, qi_ref, wv_ref, k_ref, vt_ref, ki_ref, o_ref,
                key_sc, lhs_sc, wt_sc, qs_sc, m_sc, acc_sc, *, topk):
    tq = q_ref.shape[0]
    tk = key_sc.shape[1]
    qb = pl.program_id(1)
    nkt = (qb * tq + tq + tk - 1) // tk

    lane = lax.broadcasted_iota(I32, (tq, LANES), 1)
    for hd in range(N_IDX_HEADS):
        pair = qi_ref[:, (hd // 2) * LANES:(hd // 2 + 1) * LANES].astype(F32)
        keep = (lane < IDX_DIM) if hd % 2 == 0 else (lane >= IDX_DIM)
        lhs_sc[hd] = jnp.where(keep, pair, 0.0).T.astype(BF16)
    wt_sc[...] = wv_ref[...].T
    for g in range(N_KV_HEADS):
        for r in range(KV_REP):
            hd = g * KV_REP + r
            qh = q_ref[:, hd * HEAD_DIM:(hd + 1) * HEAD_DIM].astype(F32)
            qs_sc[g, :, r * tq:(r + 1) * tq] = qh.T.astype(BF16)

    qpos = qb * tq + lax.broadcasted_iota(I32, (tk, tq), 1)
    kiota = lax.broadcasted_iota(I32, (tk, tq), 0)

    def score_tile(kt, carry):
        k0 = pl.multiple_of(kt * tk, tk)
        kit = ki_ref[pl.ds(k0, tk), :]
        acc = jnp.zeros((tk, tq), F32)
        for hd in range(N_IDX_HEADS):
            lg = jnp.dot(kit, lhs_sc[hd], preferred_element_type=F32)
            acc = acc + jnp.maximum(lg, 0.0) * wt_sc[hd:hd + 1, :]
        acc = jnp.where(kiota + k0 <= qpos, acc, -jnp.inf)
        key_sc[kt] = _order_key(acc)
        return carry

    lax.fori_loop(0, nkt, score_tile, 0)

    def count_ge(cand):
        def body(kt, acc):
            hit = jnp.where(key_sc[kt] >= cand, 1, 0).astype(I32)
            return acc + jnp.sum(hit.reshape(tk // 8, 8, tq), axis=0)

        acc = lax.fori_loop(0, nkt, body, jnp.zeros((8, tq), I32))
        return jnp.sum(acc, axis=0, keepdims=True)

    c0 = count_ge(jnp.zeros((1, tq), I32))
    pos_ok = c0 >= topk
    t0 = jnp.where(pos_ok, 0, INT_MIN).astype(I32)
    n0 = jnp.where(pos_ok, c0, nkt * tk).astype(I32)

    def search_cond(st):
        bit, _, cnt = st
        return jnp.logical_and(bit >= 0, jnp.max(jnp.abs(cnt - topk)) > 0)

    def search_body(st):
        bit, t, cnt = st
        for _ in range(2):
            cand = t + jnp.left_shift(jnp.int32(1), jnp.maximum(bit, 0))
            c = count_ge(cand)
            ok = jnp.logical_and(c >= topk, bit >= 0)
            t = jnp.where(ok, cand, t)
            cnt = jnp.where(ok, c, cnt)
            bit = bit - 1
        return bit, t, cnt

    _, thr, _ = lax.while_loop(search_cond, search_body, (jnp.int32(30), t0, n0))
    thr = jnp.maximum(thr, KEY_NEG_INF + 1)

    m_sc[...] = jnp.full(m_sc.shape, NEG_BIG, F32)
    acc_sc[...] = jnp.zeros(acc_sc.shape, F32)
    vrows = acc_sc.shape[1]

    def attn_tile(kt, carry):
        k0 = pl.multiple_of(kt * tk, tk)
        bias = jnp.where(key_sc[kt] >= thr, 0.0, NEG_BIG)
        bias = jnp.concatenate([bias] * KV_REP, axis=1)
        for g in range(N_KV_HEADS):
            kg = k_ref[pl.ds(k0, tk), g * HEAD_DIM:(g + 1) * HEAD_DIM]
            vtg = vt_ref[kt, g * vrows:(g + 1) * vrows, :]
            s = jnp.dot(kg, qs_sc[g], preferred_element_type=F32) + bias
            m_old = m_sc[g]
            m_new = jnp.maximum(m_old, jnp.max(s, axis=0, keepdims=True))
            alpha = jnp.exp(m_old - m_new)
            p = jnp.exp(s - m_new)
            acc_sc[g] = alpha * acc_sc[g] + jnp.dot(vtg, p.astype(BF16),
                                                    preferred_element_type=F32)
            m_sc[g] = m_new
        return carry

    lax.fori_loop(0, nkt, attn_tile, 0)

    for g in range(N_KV_HEADS):
        out = acc_sc[g, :HEAD_DIM, :] / acc_sc[g, HEAD_DIM:HEAD_DIM + 1, :]
        for r in range(KV_REP):
            hd = g * KV_REP + r
            o_ref[:, hd * HEAD_DIM:(hd + 1) * HEAD_DIM] = out[:, r * tq:(r + 1) * tq].T.astype(o_ref.dtype)


def _dsa_call(q, qi, wv, k, vt, ki, topk):
    bsz, seq, dq = q.shape
    tq = DSA_TQ
    nkt, dv, tk = vt.shape[1:]
    nq = seq // tq
    qmap = lambda b, i: (b, i, 0)
    bmap = lambda b, i: (b, 0, 0)
    return pl.pallas_call(
        functools.partial(_dsa_kernel, topk=topk),
        grid=(bsz, nq),
        in_specs=[pl.BlockSpec((None, tq, dq), qmap),
                  pl.BlockSpec((None, tq, qi.shape[2]), qmap),
                  pl.BlockSpec((None, tq, wv.shape[2]), qmap),
                  pl.BlockSpec((None, seq, k.shape[2]), bmap),
                  pl.BlockSpec((None, nkt, dv, tk), lambda b, i: (b, 0, 0, 0)),
                  pl.BlockSpec((None, seq, ki.shape[2]), bmap)],
        out_specs=pl.BlockSpec((None, tq, dq), qmap),
        out_shape=jax.ShapeDtypeStruct((bsz, seq, dq), BF16),
        scratch_shapes=[pltpu.VMEM((nkt, tk, tq), I32),
                        pltpu.VMEM((N_IDX_HEADS, LANES, tq), BF16),
                        pltpu.VMEM((LANES, tq), F32),
                        pltpu.VMEM((N_KV_HEADS, HEAD_DIM, KV_REP * tq), BF16),
                        pltpu.VMEM((N_KV_HEADS, 1, KV_REP * tq), F32),
                        pltpu.VMEM((N_KV_HEADS, dv // N_KV_HEADS, KV_REP * tq), F32)],
        compiler_params=_cparams(("parallel", "arbitrary")),
        name="dsa",
    )(q, qi, wv, k, vt, ki)


def _outproj_kernel(x_ref, ys_ref, ya_ref, gt_ref, sc_ref, sh_ref, wglu_ref, bglu_ref,
                    gs_ref, ga_ref, wout_ref, gm_ref, x1_out, h2_out):
    def rms(t, g):
        return t * lax.rsqrt(jnp.mean(t * t, axis=-1, keepdims=True) + EPS) * g

    y = ys_ref[...]
    ya = 0.5 * y * (1.0 + jnp.tanh(math.sqrt(2.0 / math.pi) * (y + 0.044715 * (y * y * y))))
    z = jnp.dot(ya.astype(BF16), wglu_ref[...], preferred_element_type=F32) + bglu_ref[...]
    o = ya / (1.0 + jnp.exp(-z))
    n1 = rms(o, gs_ref[...]).astype(BF16)
    n2 = rms(ya_ref[...].astype(F32), ga_ref[...]).astype(BF16)
    d1 = n1.shape[1]
    mixw = (jnp.dot(n1, wout_ref[:d1, :], preferred_element_type=F32)
            + jnp.dot(n2, wout_ref[d1:, :], preferred_element_type=F32))
    x1 = x_ref[...] + gt_ref[...] * mixw
    x1_out[...] = x1
    h2 = rms(x1, gm_ref[...]) * (1.0 + sc_ref[...]) + sh_ref[...]
    h2_out[...] = h2.astype(BF16)


def _outproj_call(x2, ys, ya, gt, sc, sh, wglu, bglu, gs, ga, wout, gm, seq):
    n, d = x2.shape
    tm = OUT_TM
    per_b = seq // tm
    row = lambda i: (i, 0)
    modmap = lambda i: (i // per_b, 0, 0)
    return pl.pallas_call(
        _outproj_kernel,
        grid=(n // tm,),
        in_specs=[pl.BlockSpec((tm, d), row),
                  pl.BlockSpec((tm, ys.shape[1]), row),
                  pl.BlockSpec((tm, ya.shape[1]), row),
                  pl.BlockSpec((None, 1, d), modmap),
                  pl.BlockSpec((None, 1, d), modmap),
                  pl.BlockSpec((None, 1, d), modmap),
                  _const_spec(wglu.shape), _const_spec(bglu.shape),
                  _const_spec(gs.shape), _const_spec(ga.shape),
                  _const_spec(wout.shape), _const_spec(gm.shape)],
        out_specs=[pl.BlockSpec((tm, d), row), pl.BlockSpec((tm, d), row)],
        out_shape=[jax.ShapeDtypeStruct((n, d), F32), jax.ShapeDtypeStruct((n, d), BF16)],
        compiler_params=_cparams(("parallel",)),
        name="outproj",
    )(x2, ys, ya, gt, sc, sh, wglu, bglu, gs, ga, wout, gm)


def _mlp_kernel(h_ref, x1_ref, gt_ref, w1_ref, w2_ref, o_ref):
    j = pl.program_id(1)
    a = jnp.maximum(jnp.dot(h_ref[...], w1_ref[...], preferred_element_type=F32), 0.0)
    part = jnp.dot((a * a).astype(BF16), w2_ref[...], preferred_element_type=F32)

    @pl.when(j == 0)
    def _():
        o_ref[...] = part

    @pl.when(j > 0)
    def _():
        o_ref[...] += part

    @pl.when(j == pl.num_programs(1) - 1)
    def _():
        o_ref[...] = x1_ref[...] + gt_ref[...] * o_ref[...]


def _mlp_call(h2, x1, gt, w1, w2, seq):
    n, d = x1.shape
    dff = w1.shape[1]
    tm, tf = MLP_TM, MLP_TF
    per_b = seq // tm
    return pl.pallas_call(
        _mlp_kernel,
        grid=(n // tm, dff // tf),
        in_specs=[pl.BlockSpec((tm, d), lambda i, j: (i, 0), pipeline_mode=pl.Buffered(1)),
                  pl.BlockSpec((tm, d), lambda i, j: (i, 0), pipeline_mode=pl.Buffered(1)),
                  pl.BlockSpec((None, 1, d), lambda i, j: (i // per_b, 0, 0)),
                  pl.BlockSpec((d, tf), lambda i, j: (0, j)),
                  pl.BlockSpec((tf, d), lambda i, j: (j, 0))],
        out_specs=pl.BlockSpec((tm, d), lambda i, j: (i, 0)),
        out_shape=jax.ShapeDtypeStruct((n, d), F32),
        compiler_params=_cparams(("parallel", "arbitrary")),
        name="mlp",
    )(h2, x1, gt, w1, w2)


def _rope_tables():
    def tables(head_dim, per_vreg):
        r = head_dim // ROPE_FRAC
        half = r // 2
        inv = ROPE_THETA ** (-jnp.arange(half, dtype=F32) / half)
        lane = np.arange(LANES) % head_dim
        inv_l = jnp.where(lane < r, inv[lane % half], 0.0).astype(F32)[None, :]
        lo = jnp.asarray((lane < half).astype(np.float32))[None, :]
        hi = jnp.asarray(((lane >= half) & (lane < r)).astype(np.float32))[None, :]
        return inv_l, lo, hi

    inv_qk, lo_qk, hi_qk = tables(HEAD_DIM, 1)
    inv_ix, lo_ix, hi_ix = tables(IDX_DIM, 2)
    return [inv_qk, inv_ix, lo_qk, hi_qk, lo_ix, hi_ix]


def _block_diag(m):
    nb, g, a, b = m.shape
    eye = jnp.eye(g, dtype=m.dtype)
    return jnp.einsum('ngab,gk->ngakb', m, eye).reshape(nb, g * a, g * b)


def kernel(x, c, positions, w_ada, b_ada, g_norm_mix, w_in, lam_re, lam_im, log_dt,
           b_re, b_im, c_re, c_im, d_skip, w_glu, b_glu, g_q, g_k, g_out_ssm, g_out_attn,
           w_out, g_norm_mlp, w_mlp_in, w_mlp_out):
    bsz, seq, d = x.shape
    depth = w_ada.shape[0]
    n = bsz * seq
    topk = min(TOPK_MAX, seq // 4)
    gg, pp, hh = N_SSM_GROUPS, SSM_STATE, SSM_GROUP
    nblk = gg // S5_GB
    chunk = S5_CHUNK
    nchunks = seq // chunk

    c_pad = jnp.zeros((8, d), F32).at[:bsz].set(c.astype(F32))
    pos2 = positions.reshape(n, 1).astype(I32)
    tabs = _rope_tables()
    xcur = x.reshape(n, d).astype(F32)

    for l in range(depth):
        mod = _ada_call(c_pad, w_ada[l], b_ada[l][None, :])[:bsz]
        mod = mod.reshape(bsz, N_MOD, 1, d)
        sh_a, sc_a, gt_a, sh_m, sc_m, gt_m = [mod[:, i] for i in range(N_MOD)]

        wi = w_in[l]
        o0 = D_SSM
        o1 = o0 + N_HEADS * HEAD_DIM
        o2 = o1 + 2 * N_KV_HEADS * HEAD_DIM
        o3 = o2 + N_IDX_HEADS * IDX_DIM
        o4 = o3 + IDX_DIM
        w_ki = wi[:, o3:o4]
        w_wi = wi[:, o4:]
        w_kw = jnp.concatenate(
            [w_ki, w_ki, w_wi, jnp.zeros((d, 2 * LANES - 2 * IDX_DIM - N_IDX_HEADS), wi.dtype)], axis=1)
        u, q, k, v, qi, ki, wv = _inproj_call(
            xcur, sc_a, sh_a, g_norm_mix[l][None, :], pos2, tabs,
            g_q[l][None, :], g_k[l][None, :],
            wi[:, :o0].astype(BF16), wi[:, o0:o1].astype(BF16), wi[:, o1:o2].astype(BF16),
            wi[:, o2:o3].astype(BF16), w_kw.astype(BF16), seq)

        lr = lam_re[l].astype(F32).reshape(1, gg * pp)
        li = lam_im[l].astype(F32).reshape(1, gg * pp)
        dt = jnp.broadcast_to(jnp.exp(log_dt[l].astype(F32))[:, None], (gg, pp)).reshape(1, gg * pp)
        bre_t = b_re[l].astype(F32).transpose(2, 0, 1).reshape(hh, gg * pp)
        bim_t = b_im[l].astype(F32).transpose(2, 0, 1).reshape(hh, gg * pp)
        pw_r, pw_i, bb_r, bb_i = _s5prep_call(lr, li, dt, bre_t, bim_t, chunk)
        bbr = bb_r.reshape(hh, nblk, S5_GB, pp).transpose(1, 2, 0, 3)
        bbi = bb_i.reshape(hh, nblk, S5_GB, pp).transpose(1, 2, 0, 3)
        bw = jnp.concatenate([_block_diag(bbr), _block_diag(bbi)], axis=2).astype(BF16)
        cr = c_re[l].astype(F32).reshape(nblk, S5_GB, hh, pp).transpose(0, 1, 3, 2)
        ci = c_im[l].astype(F32).reshape(nblk, S5_GB, hh, pp).transpose(0, 1, 3, 2)
        cw = jnp.concatenate([_block_diag(cr), -_block_diag(ci)], axis=1).astype(BF16)
        ns = S5_GB * pp
        pw = jnp.concatenate([pw_r.reshape(chunk, nblk, ns), pw_i.reshape(chunk, nblk, ns)],
                             axis=2).transpose(1, 0, 2)
        u_t = u.reshape(bsz, nchunks, chunk, D_SSM).transpose(0, 2, 1, 3)
        y_t = _s5_call(u_t, bw, cw, pw, d_skip[l].astype(F32)[None, :])
        y_ssm = y_t.transpose(0, 2, 1, 3).reshape(n, D_SSM)

        v_t = v.reshape(bsz, seq // DSA_TK, DSA_TK, N_KV_HEADS, HEAD_DIM).transpose(0, 1, 3, 4, 2)
        v_t = jnp.concatenate([v_t, jnp.ones(v_t.shape[:3] + (DSA_ONES, DSA_TK), v_t.dtype)], axis=3)
        v_t = v_t.reshape(bsz, seq // DSA_TK, N_KV_HEADS * (HEAD_DIM + DSA_ONES), DSA_TK)
        y_attn = _dsa_call(q.reshape(bsz, seq, -1), qi.reshape(bsz, seq, -1),
                           wv.reshape(bsz, seq, -1), k.reshape(bsz, seq, -1),
                           v_t, ki.reshape(bsz, seq, -1), topk)
        y_attn = y_attn.reshape(n, -1)

        x1, h2 = _outproj_call(
            xcur, y_ssm, y_attn, gt_a, sc_m, sh_m,
            w_glu[l].astype(BF16), b_glu[l].astype(F32)[None, :],
            g_out_ssm[l].astype(F32)[None, :], g_out_attn[l].astype(F32)[None, :],
            w_out[l].astype(BF16), g_norm_mlp[l].astype(F32)[None, :], seq)

        xcur = _mlp_call(h2, x1, gt_m, w_mlp_in[l].astype(BF16), w_mlp_out[l].astype(BF16), seq)

    return xcur.reshape(bsz, seq, d).astype(x.dtype)
```

```python
import functools
import math

import numpy as np
import jax
import jax.numpy as jnp
from jax import lax
from jax.experimental import pallas as pl
from jax.experimental.pallas import tpu as pltpu

F32 = jnp.float32
BF16 = jnp.bfloat16
I32 = jnp.int32

D_MODEL = 2048
D_SSM = 1024
SSM_GROUP = 16
N_SSM_GROUPS = 64
SSM_STATE = 64
HEAD_DIM = 128
N_HEADS = 8
N_KV_HEADS = 2
KV_REP = N_HEADS // N_KV_HEADS
N_IDX_HEADS = 16
IDX_DIM = 64
TOPK_MAX = 256
ROPE_FRAC = 4
ROPE_THETA = 500000.0
D_FF = 4 * D_MODEL
EPS = 1e-6
N_MOD = 6

LANES = 128
VMEM_LIMIT = 56 * 1024 * 1024

ADA_TN = 1024
INPROJ_TM = 512
S5_CHUNK = 64
S5_GB = 16
DSA_TQ = 256
DSA_TK = 256
DSA_ONES = 16
OUT_TM = 256
MLP_TM = 1024
MLP_TF = 512

NEG_BIG = -1e30
INT_MIN = -2147483648
KEY_NEG_INF = -2139095041


def _cparams(sem):
    return pltpu.CompilerParams(dimension_semantics=sem, vmem_limit_bytes=VMEM_LIMIT)


def _const_spec(shape):
    nd = len(shape)
    return pl.BlockSpec(shape, lambda *_: (0,) * nd, pipeline_mode=pl.Buffered(1))


def _ada_kernel(c_ref, w_ref, b_ref, o_ref):
    c = c_ref[...]
    act = c / (1.0 + jnp.exp(-c))
    o_ref[...] = jnp.dot(act.astype(BF16), w_ref[...].astype(BF16),
                         preferred_element_type=F32) + b_ref[...]


def _ada_call(c_pad, w_ada, b_ada):
    rows, d = c_pad.shape
    n = w_ada.shape[1]
    return pl.pallas_call(
        _ada_kernel,
        grid=(n // ADA_TN,),
        in_specs=[pl.BlockSpec((rows, d), lambda j: (0, 0)),
                  pl.BlockSpec((d, ADA_TN), lambda j: (0, j)),
                  pl.BlockSpec((1, ADA_TN), lambda j: (0, j))],
        out_specs=pl.BlockSpec((rows, ADA_TN), lambda j: (0, j)),
        out_shape=jax.ShapeDtypeStruct((rows, n), F32),
        compiler_params=_cparams(("arbitrary",)),
        name="ada",
    )(c_pad, w_ada, b_ada)


def _rope(x, cos, sin_lo, sin_hi, half):
    n = x.shape[-1]
    return (x * cos + pltpu.roll(x, n - half, 1) * sin_lo + pltpu.roll(x, half, 1) * sin_hi)


def _inproj_kernel(x_ref, sc_ref, sh_ref, g_ref, pos_ref, inv_qk_ref, inv_ix_ref,
                   mlo_qk_ref, mhi_qk_ref, mlo_ix_ref, mhi_ix_ref, gq_ref, gk_ref,
                   wu_ref, wq_ref, wkv_ref, wqi_ref, wkw_ref,
                   u_out, q_out, k_out, v_out, qi_out, ki_out, wv_out):
    x = x_ref[...]
    ms = jnp.mean(x * x, axis=-1, keepdims=True)
    h = x * lax.rsqrt(ms + EPS) * g_ref[...]
    h = h * (1.0 + sc_ref[...]) + sh_ref[...]
    hb = h.astype(BF16)

    u_out[...] = jnp.dot(hb, wu_ref[...], preferred_element_type=F32).astype(BF16)

    pos = pos_ref[...].astype(F32)
    ang = pos * inv_qk_ref[...]
    cos_qk = jnp.cos(ang)
    sin_qk = jnp.sin(ang)
    slo_qk = -sin_qk * mlo_qk_ref[...]
    shi_qk = sin_qk * mhi_qk_ref[...]
    ang = pos * inv_ix_ref[...]
    cos_ix = jnp.cos(ang)
    sin_ix = jnp.sin(ang)
    slo_ix = -sin_ix * mlo_ix_ref[...]
    shi_ix = sin_ix * mhi_ix_ref[...]
    half_qk = HEAD_DIM // ROPE_FRAC // 2
    half_ix = IDX_DIM // ROPE_FRAC // 2

    def head_norm(t, g):
        return t * lax.rsqrt(jnp.mean(t * t, axis=-1, keepdims=True) + EPS) * g

    q = jnp.dot(hb, wq_ref[...], preferred_element_type=F32)
    attn_scale = HEAD_DIM ** -0.5
    for hd in range(N_HEADS):
        sl = slice(hd * HEAD_DIM, (hd + 1) * HEAD_DIM)
        t = _rope(head_norm(q[:, sl], gq_ref[...]), cos_qk, slo_qk, shi_qk, half_qk)
        q_out[:, sl] = (t * attn_scale).astype(BF16)

    kv = jnp.dot(hb, wkv_ref[...], preferred_element_type=F32)
    nk = N_KV_HEADS * HEAD_DIM
    for hd in range(N_KV_HEADS):
        sl = slice(hd * HEAD_DIM, (hd + 1) * HEAD_DIM)
        t = _rope(head_norm(kv[:, sl], gk_ref[...]), cos_qk, slo_qk, shi_qk, half_qk)
        k_out[:, sl] = t.astype(BF16)
    v_out[...] = kv[:, nk:].astype(BF16)

    qi = jnp.dot(hb, wqi_ref[...], preferred_element_type=F32)
    idx_scale = IDX_DIM ** -0.5
    for pr in range(N_IDX_HEADS * IDX_DIM // LANES):
        sl = slice(pr * LANES, (pr + 1) * LANES)
        t = _rope(qi[:, sl], cos_ix, slo_ix, shi_ix, half_ix)
        qi_out[:, sl] = (t * idx_scale).astype(BF16)

    kw = jnp.dot(hb, wkw_ref[...], preferred_element_type=F32)
    ki_out[...] = _rope(kw[:, :LANES], cos_ix, slo_ix, shi_ix, half_ix).astype(BF16)
    wv_out[...] = kw[:, LANES:] * (N_IDX_HEADS ** -0.5)


def _inproj_call(x2, sc, sh, g, pos2, tabs, gq, gk, wu, wq, wkv, wqi, wkw, seq):
    n, d = x2.shape
    tm = INPROJ_TM
    per_b = seq // tm
    row = lambda i: (i, 0)
    modmap = lambda i: (i // per_b, 0, 0)
    small = [_const_spec(t.shape) for t in tabs]
    in_specs = ([pl.BlockSpec((tm, d), row),
                 pl.BlockSpec((None, 1, d), modmap),
                 pl.BlockSpec((None, 1, d), modmap),
                 _const_spec(g.shape),
                 pl.BlockSpec((tm, 1), row)]
                + small
                + [_const_spec(gq.shape), _const_spec(gk.shape),
                   _const_spec(wu.shape), _const_spec(wq.shape), _const_spec(wkv.shape),
                   _const_spec(wqi.shape), _const_spec(wkw.shape)])
    widths = (D_SSM, N_HEADS * HEAD_DIM, N_KV_HEADS * HEAD_DIM, N_KV_HEADS * HEAD_DIM,
              N_IDX_HEADS * IDX_DIM, LANES, LANES)
    dtypes = (BF16, BF16, BF16, BF16, BF16, BF16, F32)
    return pl.pallas_call(
        _inproj_kernel,
        grid=(n // tm,),
        in_specs=in_specs,
        out_specs=[pl.BlockSpec((tm, w), row) for w in widths],
        out_shape=[jax.ShapeDtypeStruct((n, w), dt) for w, dt in zip(widths, dtypes)],
        compiler_params=_cparams(("parallel",)),
        name="inproj",
    )(x2, sc, sh, g, pos2, *tabs, gq, gk, wu, wq, wkv, wqi, wkw)


def _s5prep_kernel(lr_ref, li_ref, dt_ref, bre_ref, bim_ref, pr_ref, pi_ref, bbr_ref, bbi_ref):
    lr = jnp.minimum(lr_ref[...], -1e-4)
    li = li_ref[...]
    dt = dt_ref[...]
    mag = jnp.exp(lr * dt)
    ab_r = mag * jnp.cos(li * dt)
    ab_i = mag * jnp.sin(li * dt)
    den = lr * lr + li * li
    fr = ((ab_r - 1.0) * lr + ab_i * li) / den
    fi = (ab_i * lr - (ab_r - 1.0) * li) / den
    bbr_ref[...] = fr * bre_ref[...] - fi * bim_ref[...]
    bbi_ref[...] = fr * bim_ref[...] + fi * bre_ref[...]
    steps = (lax.broadcasted_iota(I32, pr_ref.shape, 0) + 1).astype(F32)
    magn = jnp.exp(steps * (lr * dt))
    pr_ref[...] = magn * jnp.cos(steps * (li * dt))
    pi_ref[...] = magn * jnp.sin(steps * (li * dt))


def _s5prep_call(lr, li, dt, bre_t, bim_t, chunk):
    gp = lr.shape[1]
    hh = bre_t.shape[0]
    outs = [jax.ShapeDtypeStruct((chunk, gp), F32)] * 2 + [jax.ShapeDtypeStruct((hh, gp), F32)] * 2
    return pl.pallas_call(_s5prep_kernel, out_shape=outs, name="s5prep")(lr, li, dt, bre_t, bim_t)


def _s5_kernel(u_ref, bw_ref, cw_ref, pw_ref, dsk_ref, y_ref, st_ref, cin_ref):
    tt, rr, _ = u_ref.shape
    ns = st_ref.shape[1] // 2
    ar = pw_ref[0:1, :ns]
    ai = pw_ref[0:1, ns:]
    st_ref[...] = jnp.zeros(st_ref.shape, F32)

    def cproj(zr, zi):
        return (jnp.dot(zr.astype(BF16), cw_ref[:ns, :], preferred_element_type=F32)
                + jnp.dot(zi.astype(BF16), cw_ref[ns:, :], preferred_element_type=F32))

    def local_step(i, carry):
        ui = u_ref[i]
        x = jnp.dot(ui, bw_ref[...], preferred_element_type=F32)
        sr = st_ref[:, :ns]
        si = st_ref[:, ns:]
        nr = ar * sr - ai * si + x[:, :ns]
        ni = ar * si + ai * sr + x[:, ns:]
        st_ref[:, :ns] = nr
        st_ref[:, ns:] = ni
        y_ref[i] = cproj(nr, ni) + dsk_ref[...] * ui.astype(F32)
        return carry

    lax.fori_loop(0, tt, local_step, 0)

    atr = pw_ref[tt - 1:tt, :ns]
    ati = pw_ref[tt - 1:tt, ns:]

    def chain(r, carry):
        cr, ci = carry
        cin_ref[pl.ds(r, 1), :ns] = cr
        cin_ref[pl.ds(r, 1), ns:] = ci
        er = st_ref[pl.ds(r, 1), :ns]
        ei = st_ref[pl.ds(r, 1), ns:]
        return (atr * cr - ati * ci + er, atr * ci + ati * cr + ei)

    zero = jnp.zeros((1, ns), F32)
    lax.fori_loop(0, rr, chain, (zero, zero))

    def carry_step(i, carry):
        pr = pw_ref[pl.ds(i, 1), :ns]
        pi = pw_ref[pl.ds(i, 1), ns:]
        cr = cin_ref[:, :ns]
        ci = cin_ref[:, ns:]
        y_ref[i] += cproj(pr * cr - pi * ci, pr * ci + pi * cr)
        return carry

    lax.fori_loop(0, tt, carry_step, 0)


def _s5_call(u_t, bw, cw, pw, dsk):
    bsz, tt, rr, dch = u_t.shape
    nb = bw.shape[0]
    cb = bw.shape[1]
    ns2 = bw.shape[2]
    return pl.pallas_call(
        _s5_kernel,
        grid=(bsz, nb),
        in_specs=[pl.BlockSpec((None, tt, rr, cb), lambda b, j: (b, 0, 0, j)),
                  pl.BlockSpec((None, cb, ns2), lambda b, j: (j, 0, 0)),
                  pl.BlockSpec((None, ns2, cb), lambda b, j: (j, 0, 0)),
                  pl.BlockSpec((None, tt, ns2), lambda b, j: (j, 0, 0)),
                  pl.BlockSpec((1, cb), lambda b, j: (0, j))],
        out_specs=pl.BlockSpec((None, tt, rr, cb), lambda b, j: (b, 0, 0, j)),
        out_shape=jax.ShapeDtypeStruct((bsz, tt, rr, dch), F32),
        scratch_shapes=[pltpu.VMEM((rr, ns2), F32), pltpu.VMEM((rr, ns2), F32)],
        compiler_params=_cparams(("parallel", "parallel")),
        name="s5scan",
    )(u_t, bw, cw, pw, dsk)


def _order_key(x):
    bits = pltpu.bitcast(x, I32)
    return jnp.where(bits < 0, bits ^ jnp.int32(0x7FFFFFFF), bits)


def _dsa_kernel(q_ref, qi_ref, wv_ref, k_ref, vt_ref, ki_ref, o_ref,
                key_sc, lhs_sc, wt_sc, qs_sc, m_sc, acc_sc, s_sc, p_sc, al_sc, *, topk):
    tq = q_ref.shape[0]
    tk = key_sc.shape[1]
    qb = pl.program_id(1)
    nkt = (qb * tq + tq + tk - 1) // tk

    lane = lax.broadcasted_iota(I32, (tq, LANES), 1)
    for hd in range(N_IDX_HEADS):
        pair = qi_ref[:, (hd // 2) * LANES:(hd // 2 + 1) * LANES].astype(F32)
        keep = (lane < IDX_DIM) if hd % 2 == 0 else (lane >= IDX_DIM)
        lhs_sc[hd] = jnp.where(keep, pair, 0.0).T.astype(BF16)
    wt_sc[...] = wv_ref[...].T
    for g in range(N_KV_HEADS):
        for r in range(KV_REP):
            hd = g * KV_REP + r
            qh = q_ref[:, hd * HEAD_DIM:(hd + 1) * HEAD_DIM].astype(F32)
            qs_sc[g, :, r * tq:(r + 1) * tq] = qh.T.astype(BF16)

    qpos = qb * tq + lax.broadcasted_iota(I32, (tk, tq), 1)
    kiota = lax.broadcasted_iota(I32, (tk, tq), 0)

    def score_tile(kt, carry):
        k0 = pl.multiple_of(kt * tk, tk)
        kit = ki_ref[pl.ds(k0, tk), :]
        acc = jnp.zeros((tk, tq), F32)
        for hd in range(N_IDX_HEADS):
            lg = jnp.dot(kit, lhs_sc[hd], preferred_element_type=F32)
            acc = acc + jnp.maximum(lg, 0.0) * wt_sc[hd:hd + 1, :]
        acc = jnp.where(kiota + k0 <= qpos, acc, -jnp.inf)
        key_sc[kt] = _order_key(acc)
        return carry

    lax.fori_loop(0, nkt, score_tile, 0)

    def count_ge(cand):
        def body(kt, acc):
            hit = jnp.where(key_sc[kt] >= cand, 1, 0).astype(I32)
            return acc + jnp.sum(hit.reshape(tk // 8, 8, tq), axis=0)

        acc = lax.fori_loop(0, nkt, body, jnp.zeros((8, tq), I32))
        return jnp.sum(acc, axis=0, keepdims=True)

    c0 = count_ge(jnp.zeros((1, tq), I32))
    pos_ok = c0 >= topk
    t0 = jnp.where(pos_ok, 0, INT_MIN).astype(I32)
    n0 = jnp.where(pos_ok, c0, nkt * tk).astype(I32)

    def search_cond(st):
        bit, _, cnt = st
        return jnp.logical_and(bit >= 0, jnp.max(jnp.abs(cnt - topk)) > 0)

    def search_body(st):
        bit, t, cnt = st
        for _ in range(2):
            cand = t + jnp.left_shift(jnp.int32(1), jnp.maximum(bit, 0))
            c = count_ge(cand)
            ok = jnp.logical_and(c >= topk, bit >= 0)
            t = jnp.where(ok, cand, t)
            cnt = jnp.where(ok, c, cnt)
            bit = bit - 1
        return bit, t, cnt

    _, thr, _ = lax.while_loop(search_cond, search_body, (jnp.int32(30), t0, n0))
    thr = jnp.maximum(thr, KEY_NEG_INF + 1)

    m_sc[...] = jnp.full(m_sc.shape, NEG_BIG, F32)
    acc_sc[...] = jnp.zeros(acc_sc.shape, F32)
    p_sc[1] = jnp.zeros(p_sc.shape[1:], BF16)
    al_sc[1] = jnp.ones(al_sc.shape[1:], F32)
    vrows = acc_sc.shape[1]

    def qk(kt, g):
        k0 = pl.multiple_of(kt * tk, tk)
        kg = k_ref[pl.ds(k0, tk), g * HEAD_DIM:(g + 1) * HEAD_DIM]
        s_sc[g] = jnp.dot(kg, qs_sc[g], preferred_element_type=F32)

    def softmax(g, bias):
        s = s_sc[g] + bias
        m_old = m_sc[g]
        m_new = jnp.maximum(m_old, jnp.max(s, axis=0, keepdims=True))
        al_sc[g] = jnp.exp(m_old - m_new)
        p_sc[g] = jnp.exp(s - m_new).astype(BF16)
        m_sc[g] = m_new

    def pv(kt, g):
        vtg = vt_ref[kt, g * vrows:(g + 1) * vrows, :]
        acc_sc[g] = al_sc[g] * acc_sc[g] + jnp.dot(vtg, p_sc[g], preferred_element_type=F32)

    qk(0, 0)

    def attn_tile(kt, carry):
        bias = jnp.where(key_sc[kt] >= thr, 0.0, NEG_BIG)
        bias = jnp.concatenate([bias] * KV_REP, axis=1)
        qk(kt, 1)
        softmax(0, bias)
        pv(jnp.maximum(kt - 1, 0), 1)
        qk(jnp.minimum(kt + 1, nkt - 1), 0)
        softmax(1, bias)
        pv(kt, 0)
        return carry

    lax.fori_loop(0, nkt, attn_tile, 0)
    pv(nkt - 1, 1)

    for g in range(N_KV_HEADS):
        out = acc_sc[g, :HEAD_DIM, :] / acc_sc[g, HEAD_DIM:HEAD_DIM + 1, :]
        for r in range(KV_REP):
            hd = g * KV_REP + r
            o_ref[:, hd * HEAD_DIM:(hd + 1) * HEAD_DIM] = out[:, r * tq:(r + 1) * tq].T.astype(o_ref.dtype)


def _dsa_call(q, qi, wv, k, vt, ki, topk):
    bsz, seq, dq = q.shape
    tq = DSA_TQ
    nkt, dv, tk = vt.shape[1:]
    nq = seq // tq
    qmap = lambda b, i: (b, i, 0)
    bmap = lambda b, i: (b, 0, 0)
    return pl.pallas_call(
        functools.partial(_dsa_kernel, topk=topk),
        grid=(bsz, nq),
        in_specs=[pl.BlockSpec((None, tq, dq), qmap),
                  pl.BlockSpec((None, tq, qi.shape[2]), qmap),
                  pl.BlockSpec((None, tq, wv.shape[2]), qmap),
                  pl.BlockSpec((None, seq, k.shape[2]), bmap),
                  pl.BlockSpec((None, nkt, dv, tk), lambda b, i: (b, 0, 0, 0)),
                  pl.BlockSpec((None, seq, ki.shape[2]), bmap)],
        out_specs=pl.BlockSpec((None, tq, dq), qmap),
        out_shape=jax.ShapeDtypeStruct((bsz, seq, dq), BF16),
        scratch_shapes=[pltpu.VMEM((nkt, tk, tq), I32),
                        pltpu.VMEM((N_IDX_HEADS, LANES, tq), BF16),
                        pltpu.VMEM((LANES, tq), F32),
                        pltpu.VMEM((N_KV_HEADS, HEAD_DIM, KV_REP * tq), BF16),
                        pltpu.VMEM((N_KV_HEADS, 1, KV_REP * tq), F32),
                        pltpu.VMEM((N_KV_HEADS, dv // N_KV_HEADS, KV_REP * tq), F32),
                        pltpu.VMEM((N_KV_HEADS, tk, KV_REP * tq), F32),
                        pltpu.VMEM((N_KV_HEADS, tk, KV_REP * tq), BF16),
                        pltpu.VMEM((N_KV_HEADS, 1, KV_REP * tq), F32)],
        compiler_params=_cparams(("parallel", "arbitrary")),
        name="dsa",
    )(q, qi, wv, k, vt, ki)


def _outproj_kernel(x_ref, ys_ref, ya_ref, gt_ref, sc_ref, sh_ref, wglu_ref, bglu_ref,
                    gs_ref, ga_ref, wout_ref, gm_ref, x1_out, h2_out):
    def rms(t, g):
        return t * lax.rsqrt(jnp.mean(t * t, axis=-1, keepdims=True) + EPS) * g

    y = ys_ref[...]
    ya = 0.5 * y * (1.0 + jnp.tanh(math.sqrt(2.0 / math.pi) * (y + 0.044715 * (y * y * y))))
    z = jnp.dot(ya.astype(BF16), wglu_ref[...], preferred_element_type=F32) + bglu_ref[...]
    o = ya / (1.0 + jnp.exp(-z))
    n1 = rms(o, gs_ref[...]).astype(BF16)
    n2 = rms(ya_ref[...].astype(F32), ga_ref[...]).astype(BF16)
    d1 = n1.shape[1]
    mixw = (jnp.dot(n1, wout_ref[:d1, :], preferred_element_type=F32)
            + jnp.dot(n2, wout_ref[d1:, :], preferred_element_type=F32))
    x1 = x_ref[...] + gt_ref[...] * mixw
    x1_out[...] = x1
    h2 = rms(x1, gm_ref[...]) * (1.0 + sc_ref[...]) + sh_ref[...]
    h2_out[...] = h2.astype(BF16)


def _outproj_call(x2, ys, ya, gt, sc, sh, wglu, bglu, gs, ga, wout, gm, seq):
    n, d = x2.shape
    tm = OUT_TM
    per_b = seq // tm
    row = lambda i: (i, 0)
    modmap = lambda i: (i // per_b, 0, 0)
    return pl.pallas_call(
        _outproj_kernel,
        grid=(n // tm,),
        in_specs=[pl.BlockSpec((tm, d), row),
                  pl.BlockSpec((tm, ys.shape[1]), row),
                  pl.BlockSpec((tm, ya.shape[1]), row),
                  pl.BlockSpec((None, 1, d), modmap),
                  pl.BlockSpec((None, 1, d), modmap),
                  pl.BlockSpec((None, 1, d), modmap),
                  _const_spec(wglu.shape), _const_spec(bglu.shape),
                  _const_spec(gs.shape), _const_spec(ga.shape),
                  _const_spec(wout.shape), _const_spec(gm.shape)],
        out_specs=[pl.BlockSpec((tm, d), row), pl.BlockSpec((tm, d), row)],
        out_shape=[jax.ShapeDtypeStruct((n, d), F32), jax.ShapeDtypeStruct((n, d), BF16)],
        compiler_params=_cparams(("parallel",)),
        name="outproj",
    )(x2, ys, ya, gt, sc, sh, wglu, bglu, gs, ga, wout, gm)


def _mlp_kernel(h_ref, x1_ref, gt_ref, w1_ref, w2_ref, o_ref):
    j = pl.program_id(1)

    @pl.when(j == 0)
    def _():
        o_ref[...] = jnp.zeros(o_ref.shape, F32)

    a = jnp.maximum(jnp.dot(h_ref[...], w1_ref[...], preferred_element_type=F32), 0.0)
    o_ref[...] += jnp.dot((a * a).astype(BF16), w2_ref[...], preferred_element_type=F32)

    @pl.when(j == pl.num_programs(1) - 1)
    def _():
        o_ref[...] = x1_ref[...] + gt_ref[...] * o_ref[...]


def _mlp_call(h2, x1, gt, w1, w2, seq):
    n, d = x1.shape
    dff = w1.shape[1]
    tm, tf = MLP_TM, MLP_TF
    per_b = seq // tm
    return pl.pallas_call(
        _mlp_kernel,
        grid=(n // tm, dff // tf),
        in_specs=[pl.BlockSpec((tm, d), lambda i, j: (i, 0), pipeline_mode=pl.Buffered(1)),
                  pl.BlockSpec((tm, d), lambda i, j: (i, 0), pipeline_mode=pl.Buffered(1)),
                  pl.BlockSpec((None, 1, d), lambda i, j: (i // per_b, 0, 0)),
                  pl.BlockSpec((d, tf), lambda i, j: (0, j)),
                  pl.BlockSpec((tf, d), lambda i, j: (j, 0))],
        out_specs=pl.BlockSpec((tm, d), lambda i, j: (i, 0)),
        out_shape=jax.ShapeDtypeStruct((n, d), F32),
        compiler_params=_cparams(("parallel", "arbitrary")),
        name="mlp",
    )(h2, x1, gt, w1, w2)


def _rope_tables():
    def tables(head_dim, per_vreg):
        r = head_dim // ROPE_FRAC
        half = r // 2
        inv = ROPE_THETA ** (-jnp.arange(half, dtype=F32) / half)
        lane = np.arange(LANES) % head_dim
        inv_l = jnp.where(lane < r, inv[lane % half], 0.0).astype(F32)[None, :]
        lo = jnp.asarray((lane < half).astype(np.float32))[None, :]
        hi = jnp.asarray(((lane >= half) & (lane < r)).astype(np.float32))[None, :]
        return inv_l, lo, hi

    inv_qk, lo_qk, hi_qk = tables(HEAD_DIM, 1)
    inv_ix, lo_ix, hi_ix = tables(IDX_DIM, 2)
    return [inv_qk, inv_ix, lo_qk, hi_qk, lo_ix, hi_ix]


def _block_diag(m):
    nb, g, a, b = m.shape
    eye = jnp.eye(g, dtype=m.dtype)
    return jnp.einsum('ngab,gk->ngakb', m, eye).reshape(nb, g * a, g * b)


def kernel(x, c, positions, w_ada, b_ada, g_norm_mix, w_in, lam_re, lam_im, log_dt,
           b_re, b_im, c_re, c_im, d_skip, w_glu, b_glu, g_q, g_k, g_out_ssm, g_out_attn,
           w_out, g_norm_mlp, w_mlp_in, w_mlp_out):
    bsz, seq, d = x.shape
    depth = w_ada.shape[0]
    n = bsz * seq
    topk = min(TOPK_MAX, seq // 4)
    gg, pp, hh = N_SSM_GROUPS, SSM_STATE, SSM_GROUP
    nblk = gg // S5_GB
    chunk = S5_CHUNK
    nchunks = seq // chunk

    c_pad = jnp.zeros((8, d), F32).at[:bsz].set(c.astype(F32))
    pos2 = positions.reshape(n, 1).astype(I32)
    tabs = _rope_tables()
    xcur = x.reshape(n, d).astype(F32)

    for l in range(depth):
        mod = _ada_call(c_pad, w_ada[l], b_ada[l][None, :])[:bsz]
        mod = mod.reshape(bsz, N_MOD, 1, d)
        sh_a, sc_a, gt_a, sh_m, sc_m, gt_m = [mod[:, i] for i in range(N_MOD)]

        wi = w_in[l]
        o0 = D_SSM
        o1 = o0 + N_HEADS * HEAD_DIM
        o2 = o1 + 2 * N_KV_HEADS * HEAD_DIM
        o3 = o2 + N_IDX_HEADS * IDX_DIM
        o4 = o3 + IDX_DIM
        w_ki = wi[:, o3:o4]
        w_wi = wi[:, o4:]
        w_kw = jnp.concatenate(
            [w_ki, w_ki, w_wi, jnp.zeros((d, 2 * LANES - 2 * IDX_DIM - N_IDX_HEADS), wi.dtype)], axis=1)
        u, q, k, v, qi, ki, wv = _inproj_call(
            xcur, sc_a, sh_a, g_norm_mix[l][None, :], pos2, tabs,
            g_q[l][None, :], g_k[l][None, :],
            wi[:, :o0].astype(BF16), wi[:, o0:o1].astype(BF16), wi[:, o1:o2].astype(BF16),
            wi[:, o2:o3].astype(BF16), w_kw.astype(BF16), seq)

        lr = lam_re[l].astype(F32).reshape(1, gg * pp)
        li = lam_im[l].astype(F32).reshape(1, gg * pp)
        dt = jnp.broadcast_to(jnp.exp(log_dt[l].astype(F32))[:, None], (gg, pp)).reshape(1, gg * pp)
        bre_t = b_re[l].astype(F32).transpose(2, 0, 1).reshape(hh, gg * pp)
        bim_t = b_im[l].astype(F32).transpose(2, 0, 1).reshape(hh, gg * pp)
        pw_r, pw_i, bb_r, bb_i = _s5prep_call(lr, li, dt, bre_t, bim_t, chunk)
        bbr = bb_r.reshape(hh, nblk, S5_GB, pp).transpose(1, 2, 0, 3)
        bbi = bb_i.reshape(hh, nblk, S5_GB, pp).transpose(1, 2, 0, 3)
        bw = jnp.concatenate([_block_diag(bbr), _block_diag(bbi)], axis=2).astype(BF16)
        cr = c_re[l].astype(F32).reshape(nblk, S5_GB, hh, pp).transpose(0, 1, 3, 2)
        ci = c_im[l].astype(F32).reshape(nblk, S5_GB, hh, pp).transpose(0, 1, 3, 2)
        cw = jnp.concatenate([_block_diag(cr), -_block_diag(ci)], axis=1).astype(BF16)
        ns = S5_GB * pp
        pw = jnp.concatenate([pw_r.reshape(chunk, nblk, ns), pw_i.reshape(chunk, nblk, ns)],
                             axis=2).transpose(1, 0, 2)
        u_t = u.reshape(bsz, nchunks, chunk, D_SSM).transpose(0, 2, 1, 3)
        y_t = _s5_call(u_t, bw, cw, pw, d_skip[l].astype(F32)[None, :])
        y_ssm = y_t.transpose(0, 2, 1, 3).reshape(n, D_SSM)

        v_t = v.reshape(bsz, seq // DSA_TK, DSA_TK, N_KV_HEADS, HEAD_DIM).transpose(0, 1, 3, 4, 2)
        v_t = jnp.concatenate([v_t, jnp.ones(v_t.shape[:3] + (DSA_ONES, DSA_TK), v_t.dtype)], axis=3)
        v_t = v_t.reshape(bsz, seq // DSA_TK, N_KV_HEADS * (HEAD_DIM + DSA_ONES), DSA_TK)
        y_attn = _dsa_call(q.reshape(bsz, seq, -1), qi.reshape(bsz, seq, -1),
                           wv.reshape(bsz, seq, -1), k.reshape(bsz, seq, -1),
                           v_t, ki.reshape(bsz, seq, -1), topk)
        y_attn = y_attn.reshape(n, -1)

        x1, h2 = _outproj_call(
            xcur, y_ssm, y_attn, gt_a, sc_m, sh_m,
            w_glu[l].astype(BF16), b_glu[l].astype(F32)[None, :],
            g_out_ssm[l].astype(F32)[None, :], g_out_attn[l].astype(F32)[None, :],
            w_out[l].astype(BF16), g_norm_mlp[l].astype(F32)[None, :], seq)

        xcur = _mlp_call(h2, x1, gt_m, w_mlp_in[l].astype(BF16), w_mlp_out[l].astype(BF16), seq)

    return xcur.reshape(bsz, seq, d).astype(x.dtype)
```

```python
import functools
import math

import numpy as np
import jax
import jax.numpy as jnp
from jax import lax
from jax.experimental import pallas as pl
from jax.experimental.pallas import tpu as pltpu

F32 = jnp.float32
BF16 = jnp.bfloat16
I32 = jnp.int32

D_MODEL = 2048
D_SSM = 1024
SSM_GROUP = 16
N_SSM_GROUPS = 64
SSM_STATE = 64
HEAD_DIM = 128
N_HEADS = 8
N_KV_HEADS = 2
KV_REP = N_HEADS // N_KV_HEADS
N_IDX_HEADS = 16
IDX_DIM = 64
TOPK_MAX = 256
ROPE_FRAC = 4
ROPE_THETA = 500000.0
D_FF = 4 * D_MODEL
EPS = 1e-6
N_MOD = 6

LANES = 128
VMEM_LIMIT = 56 * 1024 * 1024

ADA_TN = 1024
INPROJ_TM = 512
S5_CHUNK = 32
S5_GB = 16
DSA_TQ = 256
DSA_TK = 256
DSA_ONES = 16
OUT_TM = 256
MLP_TM = 1024
MLP_TF = 512

NEG_BIG = -1e30
INT_MIN = -2147483648
KEY_NEG_INF = -2139095041
SEARCH_LO_KEY = 0x3C000000
SEARCH_HI_KEY = 0x40000000
SEARCH_LO_BIT = 26


def _cparams(sem):
    return pltpu.CompilerParams(dimension_semantics=sem, vmem_limit_bytes=VMEM_LIMIT)


def _const_spec(shape):
    nd = len(shape)
    return pl.BlockSpec(shape, lambda *_: (0,) * nd, pipeline_mode=pl.Buffered(1))


def _ada_kernel(c_ref, w_ref, b_ref, o_ref):
    c = c_ref[...]
    act = c / (1.0 + jnp.exp(-c))
    o_ref[...] = jnp.dot(act.astype(BF16), w_ref[...].astype(BF16),
                         preferred_element_type=F32) + b_ref[...]


def _ada_call(c_pad, w_ada, b_ada):
    rows, d = c_pad.shape
    n = w_ada.shape[1]
    return pl.pallas_call(
        _ada_kernel,
        grid=(n // ADA_TN,),
        in_specs=[pl.BlockSpec((rows, d), lambda j: (0, 0)),
                  pl.BlockSpec((d, ADA_TN), lambda j: (0, j)),
                  pl.BlockSpec((1, ADA_TN), lambda j: (0, j))],
        out_specs=pl.BlockSpec((rows, ADA_TN), lambda j: (0, j)),
        out_shape=jax.ShapeDtypeStruct((rows, n), F32),
        compiler_params=_cparams(("arbitrary",)),
        name="ada",
    )(c_pad, w_ada, b_ada)


def _rope(x, cos, sin_lo, sin_hi, half):
    n = x.shape[-1]
    return (x * cos + pltpu.roll(x, n - half, 1) * sin_lo + pltpu.roll(x, half, 1) * sin_hi)


def _inproj_kernel(x_ref, sc_ref, sh_ref, g_ref, pos_ref, inv_qk_ref, inv_ix_ref,
                   mlo_qk_ref, mhi_qk_ref, mlo_ix_ref, mhi_ix_ref, gq_ref, gk_ref,
                   wu_ref, wq_ref, wkv_ref, wqi_ref, wkw_ref,
                   u_out, q_out, k_out, v_out, qi_out, ki_out, wv_out):
    x = x_ref[...]
    ms = jnp.mean(x * x, axis=-1, keepdims=True)
    h = x * lax.rsqrt(ms + EPS) * g_ref[...]
    h = h * (1.0 + sc_ref[...]) + sh_ref[...]
    hb = h.astype(BF16)

    u_out[...] = jnp.dot(hb, wu_ref[...], preferred_element_type=F32).astype(BF16)

    pos = pos_ref[...].astype(F32)
    ang = pos * inv_qk_ref[...]
    cos_qk = jnp.cos(ang)
    sin_qk = jnp.sin(ang)
    slo_qk = -sin_qk * mlo_qk_ref[...]
    shi_qk = sin_qk * mhi_qk_ref[...]
    ang = pos * inv_ix_ref[...]
    cos_ix = jnp.cos(ang)
    sin_ix = jnp.sin(ang)
    slo_ix = -sin_ix * mlo_ix_ref[...]
    shi_ix = sin_ix * mhi_ix_ref[...]
    half_qk = HEAD_DIM // ROPE_FRAC // 2
    half_ix = IDX_DIM // ROPE_FRAC // 2

    def head_norm(t, g):
        return t * lax.rsqrt(jnp.mean(t * t, axis=-1, keepdims=True) + EPS) * g

    q = jnp.dot(hb, wq_ref[...], preferred_element_type=F32)
    attn_scale = HEAD_DIM ** -0.5
    for hd in range(N_HEADS):
        sl = slice(hd * HEAD_DIM, (hd + 1) * HEAD_DIM)
        t = _rope(head_norm(q[:, sl], gq_ref[...]), cos_qk, slo_qk, shi_qk, half_qk)
        q_out[:, sl] = (t * attn_scale).astype(BF16)

    kv = jnp.dot(hb, wkv_ref[...], preferred_element_type=F32)
    nk = N_KV_HEADS * HEAD_DIM
    for hd in range(N_KV_HEADS):
        sl = slice(hd * HEAD_DIM, (hd + 1) * HEAD_DIM)
        t = _rope(head_norm(kv[:, sl], gk_ref[...]), cos_qk, slo_qk, shi_qk, half_qk)
        k_out[:, sl] = t.astype(BF16)
    vrows = HEAD_DIM + DSA_ONES
    ones = jnp.ones((DSA_ONES, DSA_TK), BF16)
    for t in range(v_out.shape[0]):
        for hd in range(N_KV_HEADS):
            vh = kv[t * DSA_TK:(t + 1) * DSA_TK, nk + hd * HEAD_DIM:nk + (hd + 1) * HEAD_DIM]
            v_out[t, hd * vrows:hd * vrows + HEAD_DIM, :] = vh.T.astype(BF16)
            v_out[t, hd * vrows + HEAD_DIM:(hd + 1) * vrows, :] = ones

    qi = jnp.dot(hb, wqi_ref[...], preferred_element_type=F32)
    idx_scale = IDX_DIM ** -0.5
    for pr in range(N_IDX_HEADS * IDX_DIM // LANES):
        sl = slice(pr * LANES, (pr + 1) * LANES)
        t = _rope(qi[:, sl], cos_ix, slo_ix, shi_ix, half_ix)
        qi_out[:, sl] = (t * idx_scale).astype(BF16)

    kw = jnp.dot(hb, wkw_ref[...], preferred_element_type=F32)
    ki_out[...] = _rope(kw[:, :LANES], cos_ix, slo_ix, shi_ix, half_ix).astype(BF16)
    wv_out[...] = kw[:, LANES:] * (N_IDX_HEADS ** -0.5)


def _inproj_call(x2, sc, sh, g, pos2, tabs, gq, gk, wu, wq, wkv, wqi, wkw, seq):
    n, d = x2.shape
    tm = INPROJ_TM
    per_b = seq // tm
    row = lambda i: (i, 0)
    modmap = lambda i: (i // per_b, 0, 0)
    small = [_const_spec(t.shape) for t in tabs]
    in_specs = ([pl.BlockSpec((tm, d), row),
                 pl.BlockSpec((None, 1, d), modmap),
                 pl.BlockSpec((None, 1, d), modmap),
                 _const_spec(g.shape),
                 pl.BlockSpec((tm, 1), row)]
                + small
                + [_const_spec(gq.shape), _const_spec(gk.shape),
                   _const_spec(wu.shape), _const_spec(wq.shape), _const_spec(wkv.shape),
                   _const_spec(wqi.shape), _const_spec(wkw.shape)])
    widths = (D_SSM, N_HEADS * HEAD_DIM, N_KV_HEADS * HEAD_DIM, N_KV_HEADS * HEAD_DIM,
              N_IDX_HEADS * IDX_DIM, LANES, LANES)
    dtypes = (BF16, BF16, BF16, BF16, BF16, BF16, F32)
    vt_rows = N_KV_HEADS * (HEAD_DIM + DSA_ONES)
    out_specs = [pl.BlockSpec((tm, w), row) for w in widths]
    out_shape = [jax.ShapeDtypeStruct((n, w), dt) for w, dt in zip(widths, dtypes)]
    out_specs[3] = pl.BlockSpec((tm // DSA_TK, vt_rows, DSA_TK), lambda i: (i, 0, 0))
    out_shape[3] = jax.ShapeDtypeStruct((n // DSA_TK, vt_rows, DSA_TK), BF16)
    return pl.pallas_call(
        _inproj_kernel,
        grid=(n // tm,),
        in_specs=in_specs,
        out_specs=out_specs,
        out_shape=out_shape,
        compiler_params=_cparams(("parallel",)),
        name="inproj",
    )(x2, sc, sh, g, pos2, *tabs, gq, gk, wu, wq, wkv, wqi, wkw)


def _s5prep_kernel(lr_ref, li_ref, dt_ref, bre_ref, bim_ref, pr_ref, pi_ref, bbr_ref, bbi_ref):
    lr = jnp.minimum(lr_ref[...], -1e-4)
    li = li_ref[...]
    dt = dt_ref[...]
    mag = jnp.exp(lr * dt)
    ab_r = mag * jnp.cos(li * dt)
    ab_i = mag * jnp.sin(li * dt)
    den = lr * lr + li * li
    fr = ((ab_r - 1.0) * lr + ab_i * li) / den
    fi = (ab_i * lr - (ab_r - 1.0) * li) / den
    bbr_ref[...] = fr * bre_ref[...] - fi * bim_ref[...]
    bbi_ref[...] = fr * bim_ref[...] + fi * bre_ref[...]
    steps = (lax.broadcasted_iota(I32, pr_ref.shape, 0) + 1).astype(F32)
    magn = jnp.exp(steps * (lr * dt))
    pr_ref[...] = magn * jnp.cos(steps * (li * dt))
    pi_ref[...] = magn * jnp.sin(steps * (li * dt))


def _s5prep_call(lr, li, dt, bre_t, bim_t, chunk):
    gp = lr.shape[1]
    hh = bre_t.shape[0]
    outs = [jax.ShapeDtypeStruct((chunk, gp), F32)] * 2 + [jax.ShapeDtypeStruct((hh, gp), F32)] * 2
    return pl.pallas_call(_s5prep_kernel, out_shape=outs, name="s5prep")(lr, li, dt, bre_t, bim_t)


def _s5_kernel(u_ref, bw_ref, cw_ref, pw_ref, dsk_ref, y_ref, st_ref, cin_ref, xs_ref):
    tt, rr, _ = u_ref.shape
    ns = st_ref.shape[1] // 2
    ar = pw_ref[0:1, :ns]
    ai = pw_ref[0:1, ns:]
    st_ref[...] = jnp.zeros(st_ref.shape, F32)

    def cproj(zr, zi):
        return (jnp.dot(zr.astype(BF16), cw_ref[:ns, :], preferred_element_type=F32)
                + jnp.dot(zi.astype(BF16), cw_ref[ns:, :], preferred_element_type=F32))

    xs_ref[0] = jnp.dot(u_ref[0], bw_ref[...], preferred_element_type=F32)

    def local_step(i, carry):
        x = xs_ref[i % 2]
        nxt = jnp.minimum(i + 1, tt - 1)
        xs_ref[(i + 1) % 2] = jnp.dot(u_ref[nxt], bw_ref[...], preferred_element_type=F32)
        sr = st_ref[:, :ns]
        si = st_ref[:, ns:]
        nr = ar * sr - ai * si + x[:, :ns]
        ni = ar * si + ai * sr + x[:, ns:]
        st_ref[:, :ns] = nr
        st_ref[:, ns:] = ni
        y_ref[i] = cproj(nr, ni) + dsk_ref[...] * u_ref[i].astype(F32)
        return carry

    lax.fori_loop(0, tt, local_step, 0)

    atr = pw_ref[tt - 1:tt, :ns]
    ati = pw_ref[tt - 1:tt, ns:]

    def chain(r, carry):
        cr, ci = carry
        cin_ref[pl.ds(r, 1), :ns] = cr
        cin_ref[pl.ds(r, 1), ns:] = ci
        er = st_ref[pl.ds(r, 1), :ns]
        ei = st_ref[pl.ds(r, 1), ns:]
        return (atr * cr - ati * ci + er, atr * ci + ati * cr + ei)

    zero = jnp.zeros((1, ns), F32)
    lax.fori_loop(0, rr, chain, (zero, zero))

    def carry_step(i, carry):
        pr = pw_ref[pl.ds(i, 1), :ns]
        pi = pw_ref[pl.ds(i, 1), ns:]
        cr = cin_ref[:, :ns]
        ci = cin_ref[:, ns:]
        y_ref[i] += cproj(pr * cr - pi * ci, pr * ci + pi * cr)
        return carry

    lax.fori_loop(0, tt, carry_step, 0)


def _s5_call(u_t, bw, cw, pw, dsk):
    bsz, tt, rr, dch = u_t.shape
    nb = bw.shape[0]
    cb = bw.shape[1]
    ns2 = bw.shape[2]
    return pl.pallas_call(
        _s5_kernel,
        grid=(bsz, nb),
        in_specs=[pl.BlockSpec((None, tt, rr, cb), lambda b, j: (b, 0, 0, j)),
                  pl.BlockSpec((None, cb, ns2), lambda b, j: (j, 0, 0)),
                  pl.BlockSpec((None, ns2, cb), lambda b, j: (j, 0, 0)),
                  pl.BlockSpec((None, tt, ns2), lambda b, j: (j, 0, 0)),
                  pl.BlockSpec((1, cb), lambda b, j: (0, j))],
        out_specs=pl.BlockSpec((None, tt, rr, cb), lambda b, j: (b, 0, 0, j)),
        out_shape=jax.ShapeDtypeStruct((bsz, tt, rr, dch), F32),
        scratch_shapes=[pltpu.VMEM((rr, ns2), F32), pltpu.VMEM((rr, ns2), F32),
                        pltpu.VMEM((2, rr, ns2), F32)],
        compiler_params=_cparams(("parallel", "parallel")),
        name="s5scan",
    )(u_t, bw, cw, pw, dsk)


def _order_key(x):
    bits = pltpu.bitcast(x, I32)
    return jnp.where(bits < 0, bits ^ jnp.int32(0x7FFFFFFF), bits)


def _dsa_kernel(q_ref, qi_ref, wv_ref, k_ref, vt_ref, ki_ref, o_ref,
                key_sc, lhs_sc, wt_sc, qs_sc, m_sc, acc_sc, s_sc, p_sc, al_sc, *, topk):
    tq = q_ref.shape[0]
    tk = key_sc.shape[1]
    qb = pl.program_id(1)
    nkt = (qb * tq + tq + tk - 1) // tk

    lane = lax.broadcasted_iota(I32, (tq, LANES), 1)
    kiota = lax.broadcasted_iota(I32, (tk, tq), 0)
    qpos = qb * tq + lax.broadcasted_iota(I32, (tk, tq), 1)

    for hd in range(N_IDX_HEADS):
        pair = qi_ref[:, (hd // 2) * LANES:(hd // 2 + 1) * LANES].astype(F32)
        keep = (lane < IDX_DIM) if hd % 2 == 0 else (lane >= IDX_DIM)
        lhs_sc[hd] = jnp.where(keep, pair, 0.0).T.astype(BF16)
    wt_sc[...] = wv_ref[...].T

    def score_tile(kt, carry):
        k0 = pl.multiple_of(kt * tk, tk)
        kit = ki_ref[pl.ds(k0, tk), :]
        acc = jnp.zeros((tk, tq), F32)
        for hd in range(N_IDX_HEADS):
            lg = jnp.dot(kit, lhs_sc[hd], preferred_element_type=F32)
            acc = acc + jnp.maximum(lg, 0.0) * wt_sc[hd:hd + 1, :]
        acc = jnp.where(kiota + k0 <= qpos, acc, -jnp.inf)
        key_sc[kt] = _order_key(acc)
        return carry

    lax.fori_loop(0, nkt, score_tile, 0)

    def count_ge(cand):
        def body(kt, acc):
            hit = jnp.where(key_sc[kt] >= cand, 1, 0).astype(I32)
            return acc + jnp.sum(hit.reshape(tk // 8, 8, tq), axis=0)

        acc = lax.fori_loop(0, nkt, body, jnp.zeros((8, tq), I32))
        return jnp.sum(acc, axis=0, keepdims=True)

    c_lo = count_ge(jnp.full((1, tq), SEARCH_LO_KEY, I32))
    c_hi = count_ge(jnp.full((1, tq), SEARCH_HI_KEY, I32))
    fast = jnp.max(jnp.maximum(topk - c_lo, c_hi - topk + 1)) <= 0
    bit0 = jnp.where(fast, SEARCH_LO_BIT - 1, 31).astype(I32)
    t0 = jnp.where(fast, SEARCH_LO_KEY, INT_MIN).astype(I32) + jnp.zeros((1, tq), I32)
    n0 = jnp.where(fast, c_lo, nkt * tk).astype(I32)

    def search_cond(st):
        bit, _, cnt = st
        return jnp.logical_and(bit >= 0, jnp.max(jnp.abs(cnt - topk)) > 0)

    def search_body(st):
        bit, t, cnt = st
        for _ in range(2):
            cand = t + jnp.left_shift(jnp.int32(1), jnp.maximum(bit, 0))
            c = count_ge(cand)
            ok = jnp.logical_and(c >= topk, bit >= 0)
            t = jnp.where(ok, cand, t)
            cnt = jnp.where(ok, c, cnt)
            bit = bit - 1
        return bit, t, cnt

    _, thr, _ = lax.while_loop(search_cond, search_body, (bit0, t0, n0))
    thr = jnp.maximum(thr, KEY_NEG_INF + 1)

    for g in range(N_KV_HEADS):
        for r in range(KV_REP):
            hd = g * KV_REP + r
            qh = q_ref[:, hd * HEAD_DIM:(hd + 1) * HEAD_DIM].astype(F32)
            qs_sc[g, :, r * tq:(r + 1) * tq] = qh.T.astype(BF16)

    m_sc[...] = jnp.full(m_sc.shape, NEG_BIG, F32)
    acc_sc[...] = jnp.zeros(acc_sc.shape, F32)
    p_sc[1] = jnp.zeros(p_sc.shape[1:], BF16)
    al_sc[1] = jnp.ones(al_sc.shape[1:], F32)
    vrows = acc_sc.shape[1]

    def qk(kt, g):
        k0 = pl.multiple_of(kt * tk, tk)
        kg = k_ref[pl.ds(k0, tk), g * HEAD_DIM:(g + 1) * HEAD_DIM]
        s_sc[g] = jnp.dot(kg, qs_sc[g], preferred_element_type=F32)

    def softmax(g, bias):
        s = s_sc[g] + bias
        m_old = m_sc[g]
        m_new = jnp.maximum(m_old, jnp.max(s, axis=0, keepdims=True))
        al_sc[g] = jnp.exp(m_old - m_new)
        p_sc[g] = jnp.exp(s - m_new).astype(BF16)
        m_sc[g] = m_new

    def pv(kt, g):
        vtg = vt_ref[kt, g * vrows:(g + 1) * vrows, :]
        acc_sc[g] = al_sc[g] * acc_sc[g] + jnp.dot(vtg, p_sc[g], preferred_element_type=F32)

    def attn_tile(kt, carry):
        bias = jnp.where(key_sc[kt] >= thr, 0.0, NEG_BIG)
        bias = jnp.concatenate([bias] * KV_REP, axis=1)
        qk(kt, 1)
        softmax(0, bias)
        pv(jnp.maximum(kt - 1, 0), 1)
        qk(jnp.minimum(kt + 1, nkt - 1), 0)
        softmax(1, bias)
        pv(kt, 0)
        return carry

    qk(0, 0)
    lax.fori_loop(0, nkt, attn_tile, 0)
    pv(nkt - 1, 1)

    for g in range(N_KV_HEADS):
        out = acc_sc[g, :HEAD_DIM, :] / acc_sc[g, HEAD_DIM:HEAD_DIM + 1, :]
        for r in range(KV_REP):
            hd = g * KV_REP + r
            o_ref[:, hd * HEAD_DIM:(hd + 1) * HEAD_DIM] = out[:, r * tq:(r + 1) * tq].T.astype(o_ref.dtype)


def _dsa_call(q, qi, wv, k, vt, ki, topk):
    bsz, seq, dq = q.shape
    tq = DSA_TQ
    nkt, dv, tk = vt.shape[1:]
    nq = seq // tq
    qmap = lambda b, i: (b, i, 0)

    def whole(shape):
        return pl.BlockSpec((None,) + tuple(shape), lambda b, i: (b,) + (0,) * len(shape),
                            pipeline_mode=pl.Buffered(1))

    return pl.pallas_call(
        functools.partial(_dsa_kernel, topk=topk),
        grid=(bsz, nq),
        in_specs=[pl.BlockSpec((None, tq, dq), qmap),
                  pl.BlockSpec((None, tq, qi.shape[2]), qmap),
                  pl.BlockSpec((None, tq, wv.shape[2]), qmap),
                  whole(k.shape[1:]), whole(vt.shape[1:]), whole(ki.shape[1:])],
        out_specs=pl.BlockSpec((None, tq, dq), qmap),
        out_shape=jax.ShapeDtypeStruct((bsz, seq, dq), BF16),
        scratch_shapes=[pltpu.VMEM((nkt, tk, tq), I32),
                        pltpu.VMEM((N_IDX_HEADS, LANES, tq), BF16),
                        pltpu.VMEM((LANES, tq), F32),
                        pltpu.VMEM((N_KV_HEADS, HEAD_DIM, KV_REP * tq), BF16),
                        pltpu.VMEM((N_KV_HEADS, 1, KV_REP * tq), F32),
                        pltpu.VMEM((N_KV_HEADS, dv // N_KV_HEADS, KV_REP * tq), F32),
                        pltpu.VMEM((N_KV_HEADS, tk, KV_REP * tq), F32),
                        pltpu.VMEM((N_KV_HEADS, tk, KV_REP * tq), BF16),
                        pltpu.VMEM((N_KV_HEADS, 1, KV_REP * tq), F32)],
        compiler_params=_cparams(("parallel", "arbitrary")),
        name="dsa",
    )(q, qi, wv, k, vt, ki)


def _outproj_kernel(x_ref, ys_ref, ya_ref, gt_ref, sc_ref, sh_ref, wglu_ref, bglu_ref,
                    gs_ref, ga_ref, wout_ref, gm_ref, x1_out, h2_out):
    def rms(t, g):
        return t * lax.rsqrt(jnp.mean(t * t, axis=-1, keepdims=True) + EPS) * g

    y = ys_ref[...]
    ya = 0.5 * y * (1.0 + jnp.tanh(math.sqrt(2.0 / math.pi) * (y + 0.044715 * (y * y * y))))
    z = jnp.dot(ya.astype(BF16), wglu_ref[...], preferred_element_type=F32) + bglu_ref[...]
    o = ya / (1.0 + jnp.exp(-z))
    n1 = rms(o, gs_ref[...]).astype(BF16)
    n2 = rms(ya_ref[...].astype(F32), ga_ref[...]).astype(BF16)
    d1 = n1.shape[1]
    mixw = (jnp.dot(n1, wout_ref[:d1, :], preferred_element_type=F32)
            + jnp.dot(n2, wout_ref[d1:, :], preferred_element_type=F32))
    x1 = x_ref[...] + gt_ref[...] * mixw
    x1_out[...] = x1
    h2 = rms(x1, gm_ref[...]) * (1.0 + sc_ref[...]) + sh_ref[...]
    h2_out[...] = h2.astype(BF16)


def _outproj_call(x2, ys, ya, gt, sc, sh, wglu, bglu, gs, ga, wout, gm, seq):
    n, d = x2.shape
    tm = OUT_TM
    per_b = seq // tm
    row = lambda i: (i, 0)
    modmap = lambda i: (i // per_b, 0, 0)
    return pl.pallas_call(
        _outproj_kernel,
        grid=(n // tm,),
        in_specs=[pl.BlockSpec((tm, d), row),
                  pl.BlockSpec((tm, ys.shape[1]), row),
                  pl.BlockSpec((tm, ya.shape[1]), row),
                  pl.BlockSpec((None, 1, d), modmap),
                  pl.BlockSpec((None, 1, d), modmap),
                  pl.BlockSpec((None, 1, d), modmap),
                  _const_spec(wglu.shape), _const_spec(bglu.shape),
                  _const_spec(gs.shape), _const_spec(ga.shape),
                  _const_spec(wout.shape), _const_spec(gm.shape)],
        out_specs=[pl.BlockSpec((tm, d), row), pl.BlockSpec((tm, d), row)],
        out_shape=[jax.ShapeDtypeStruct((n, d), F32), jax.ShapeDtypeStruct((n, d), BF16)],
        compiler_params=_cparams(("parallel",)),
        name="outproj",
    )(x2, ys, ya, gt, sc, sh, wglu, bglu, gs, ga, wout, gm)


def _mlp_kernel(h_ref, x1_ref, gt_ref, w1_ref, w2_ref, o_ref):
    j = pl.program_id(1)

    @pl.when(j == 0)
    def _():
        o_ref[...] = jnp.zeros(o_ref.shape, F32)

    a = jnp.maximum(jnp.dot(h_ref[...], w1_ref[...], preferred_element_type=F32), 0.0)
    o_ref[...] += jnp.dot((a * a).astype(BF16), w2_ref[...], preferred_element_type=F32)

    @pl.when(j == pl.num_programs(1) - 1)
    def _():
        o_ref[...] = x1_ref[...] + gt_ref[...] * o_ref[...]


def _mlp_call(h2, x1, gt, w1, w2, seq):
    n, d = x1.shape
    dff = w1.shape[1]
    tm, tf = MLP_TM, MLP_TF
    per_b = seq // tm
    return pl.pallas_call(
        _mlp_kernel,
        grid=(n // tm, dff // tf),
        in_specs=[pl.BlockSpec((tm, d), lambda i, j: (i, 0), pipeline_mode=pl.Buffered(1)),
                  pl.BlockSpec((tm, d), lambda i, j: (i, 0), pipeline_mode=pl.Buffered(1)),
                  pl.BlockSpec((None, 1, d), lambda i, j: (i // per_b, 0, 0)),
                  pl.BlockSpec((d, tf), lambda i, j: (0, j)),
                  pl.BlockSpec((tf, d), lambda i, j: (j, 0))],
        out_specs=pl.BlockSpec((tm, d), lambda i, j: (i, 0)),
        out_shape=jax.ShapeDtypeStruct((n, d), F32),
        compiler_params=_cparams(("parallel", "arbitrary")),
        name="mlp",
    )(h2, x1, gt, w1, w2)


def _rope_tables():
    def tables(head_dim, per_vreg):
        r = head_dim // ROPE_FRAC
        half = r // 2
        inv = ROPE_THETA ** (-jnp.arange(half, dtype=F32) / half)
        lane = np.arange(LANES) % head_dim
        inv_l = jnp.where(lane < r, inv[lane % half], 0.0).astype(F32)[None, :]
        lo = jnp.asarray((lane < half).astype(np.float32))[None, :]
        hi = jnp.asarray(((lane >= half) & (lane < r)).astype(np.float32))[None, :]
        return inv_l, lo, hi

    inv_qk, lo_qk, hi_qk = tables(HEAD_DIM, 1)
    inv_ix, lo_ix, hi_ix = tables(IDX_DIM, 2)
    return [inv_qk, inv_ix, lo_qk, hi_qk, lo_ix, hi_ix]


def _block_diag(m):
    nb, g, a, b = m.shape
    eye = jnp.eye(g, dtype=m.dtype)
    return jnp.einsum('ngab,gk->ngakb', m, eye).reshape(nb, g * a, g * b)


def kernel(x, c, positions, w_ada, b_ada, g_norm_mix, w_in, lam_re, lam_im, log_dt,
           b_re, b_im, c_re, c_im, d_skip, w_glu, b_glu, g_q, g_k, g_out_ssm, g_out_attn,
           w_out, g_norm_mlp, w_mlp_in, w_mlp_out):
    bsz, seq, d = x.shape
    depth = w_ada.shape[0]
    n = bsz * seq
    topk = min(TOPK_MAX, seq // 4)
    gg, pp, hh = N_SSM_GROUPS, SSM_STATE, SSM_GROUP
    nblk = gg // S5_GB
    chunk = S5_CHUNK
    nchunks = seq // chunk

    c_pad = jnp.zeros((8, d), F32).at[:bsz].set(c.astype(F32))
    pos2 = positions.reshape(n, 1).astype(I32)
    tabs = _rope_tables()
    xcur = x.reshape(n, d).astype(F32)

    for l in range(depth):
        mod = _ada_call(c_pad, w_ada[l], b_ada[l][None, :])[:bsz]
        mod = mod.reshape(bsz, N_MOD, 1, d)
        sh_a, sc_a, gt_a, sh_m, sc_m, gt_m = [mod[:, i] for i in range(N_MOD)]

        wi = w_in[l]
        o0 = D_SSM
        o1 = o0 + N_HEADS * HEAD_DIM
        o2 = o1 + 2 * N_KV_HEADS * HEAD_DIM
        o3 = o2 + N_IDX_HEADS * IDX_DIM
        o4 = o3 + IDX_DIM
        w_ki = wi[:, o3:o4]
        w_wi = wi[:, o4:]
        w_kw = jnp.concatenate(
            [w_ki, w_ki, w_wi, jnp.zeros((d, 2 * LANES - 2 * IDX_DIM - N_IDX_HEADS), wi.dtype)], axis=1)
        u, q, k, v, qi, ki, wv = _inproj_call(
            xcur, sc_a, sh_a, g_norm_mix[l][None, :], pos2, tabs,
            g_q[l][None, :], g_k[l][None, :],
            wi[:, :o0].astype(BF16), wi[:, o0:o1].astype(BF16), wi[:, o1:o2].astype(BF16),
            wi[:, o2:o3].astype(BF16), w_kw.astype(BF16), seq)

        lr = lam_re[l].astype(F32).reshape(1, gg * pp)
        li = lam_im[l].astype(F32).reshape(1, gg * pp)
        dt = jnp.broadcast_to(jnp.exp(log_dt[l].astype(F32))[:, None], (gg, pp)).reshape(1, gg * pp)
        bre_t = b_re[l].astype(F32).transpose(2, 0, 1).reshape(hh, gg * pp)
        bim_t = b_im[l].astype(F32).transpose(2, 0, 1).reshape(hh, gg * pp)
        pw_r, pw_i, bb_r, bb_i = _s5prep_call(lr, li, dt, bre_t, bim_t, chunk)
        bbr = bb_r.reshape(hh, nblk, S5_GB, pp).transpose(1, 2, 0, 3)
        bbi = bb_i.reshape(hh, nblk, S5_GB, pp).transpose(1, 2, 0, 3)
        bw = jnp.concatenate([_block_diag(bbr), _block_diag(bbi)], axis=2).astype(BF16)
        cr = c_re[l].astype(F32).reshape(nblk, S5_GB, hh, pp).transpose(0, 1, 3, 2)
        ci = c_im[l].astype(F32).reshape(nblk, S5_GB, hh, pp).transpose(0, 1, 3, 2)
        cw = jnp.concatenate([_block_diag(cr), -_block_diag(ci)], axis=1).astype(BF16)
        ns = S5_GB * pp
        pw = jnp.concatenate([pw_r.reshape(chunk, nblk, ns), pw_i.reshape(chunk, nblk, ns)],
                             axis=2).transpose(1, 0, 2)
        u_t = u.reshape(bsz, nchunks, chunk, D_SSM).transpose(0, 2, 1, 3)
        y_t = _s5_call(u_t, bw, cw, pw, d_skip[l].astype(F32)[None, :])
        y_ssm = y_t.transpose(0, 2, 1, 3).reshape(n, D_SSM)

        v_t = v.reshape(bsz, seq // DSA_TK, v.shape[1], DSA_TK)
        y_attn = _dsa_call(q.reshape(bsz, seq, -1), qi.reshape(bsz, seq, -1),
                           wv.reshape(bsz, seq, -1), k.reshape(bsz, seq, -1),
                           v_t, ki.reshape(bsz, seq, -1), topk)
        y_attn = y_attn.reshape(n, -1)

        x1, h2 = _outproj_call(
            xcur, y_ssm, y_attn, gt_a, sc_m, sh_m,
            w_glu[l].astype(BF16), b_glu[l].astype(F32)[None, :],
            g_out_ssm[l].astype(F32)[None, :], g_out_attn[l].astype(F32)[None, :],
            w_out[l].astype(BF16), g_norm_mlp[l].astype(F32)[None, :], seq)

        xcur = _mlp_call(h2, x1, gt_m, w_mlp_in[l].astype(BF16), w_mlp_out[l].astype(BF16), seq)

    return xcur.reshape(bsz, seq, d).astype(x.dtype)
```

```python
import functools
import math

import numpy as np
import jax
import jax.numpy as jnp
from jax import lax
from jax.experimental import pallas as pl
from jax.experimental.pallas import tpu as pltpu

F32 = jnp.float32
BF16 = jnp.bfloat16
I32 = jnp.int32

D_MODEL = 2048
D_SSM = 1024
SSM_GROUP = 16
N_SSM_GROUPS = 64
SSM_STATE = 64
HEAD_DIM = 128
N_HEADS = 8
N_KV_HEADS = 2
KV_REP = N_HEADS // N_KV_HEADS
N_IDX_HEADS = 16
IDX_DIM = 64
TOPK_MAX = 256
ROPE_FRAC = 4
ROPE_THETA = 500000.0
D_FF = 4 * D_MODEL
EPS = 1e-6
N_MOD = 6

LANES = 128
VMEM_LIMIT = 56 * 1024 * 1024

ADA_TN = 1024
INPROJ_TM = 512
S5_CHUNK = 32
S5_GB = 16
DSA_TQ = 256
DSA_TK = 256
DSA_ONES = 16
OUT_TM = 256
MLP_TM = 1024
MLP_TF = 512

NEG_BIG = -1e30
INT_MIN = -2147483648
KEY_NEG_INF = -2139095041
KEY16_NEG_INF = KEY_NEG_INF >> 16


def _cparams(sem):
    return pltpu.CompilerParams(dimension_semantics=sem, vmem_limit_bytes=VMEM_LIMIT)


def _const_spec(shape):
    nd = len(shape)
    return pl.BlockSpec(shape, lambda *_: (0,) * nd, pipeline_mode=pl.Buffered(1))


def _ada_kernel(c_ref, w_ref, b_ref, o_ref):
    c = c_ref[...]
    act = c / (1.0 + jnp.exp(-c))
    o_ref[...] = jnp.dot(act.astype(BF16), w_ref[...].astype(BF16),
                         preferred_element_type=F32) + b_ref[...]


def _ada_call(c_pad, w_ada, b_ada):
    rows, d = c_pad.shape
    n = w_ada.shape[1]
    return pl.pallas_call(
        _ada_kernel,
        grid=(n // ADA_TN,),
        in_specs=[pl.BlockSpec((rows, d), lambda j: (0, 0)),
                  pl.BlockSpec((d, ADA_TN), lambda j: (0, j)),
                  pl.BlockSpec((1, ADA_TN), lambda j: (0, j))],
        out_specs=pl.BlockSpec((rows, ADA_TN), lambda j: (0, j)),
        out_shape=jax.ShapeDtypeStruct((rows, n), F32),
        compiler_params=_cparams(("arbitrary",)),
        name="ada",
    )(c_pad, w_ada, b_ada)


def _rope(x, cos, sin_lo, sin_hi, half):
    n = x.shape[-1]
    return (x * cos + pltpu.roll(x, n - half, 1) * sin_lo + pltpu.roll(x, half, 1) * sin_hi)


def _inproj_kernel(x_ref, sc_ref, sh_ref, g_ref, pos_ref, inv_qk_ref, inv_ix_ref,
                   mlo_qk_ref, mhi_qk_ref, mlo_ix_ref, mhi_ix_ref, gq_ref, gk_ref,
                   wu_ref, wq_ref, wkv_ref, wqi_ref, wkw_ref,
                   u_out, q_out, k_out, v_out, qi_out, ki_out, wv_out):
    x = x_ref[...]
    ms = jnp.mean(x * x, axis=-1, keepdims=True)
    h = x * lax.rsqrt(ms + EPS) * g_ref[...]
    h = h * (1.0 + sc_ref[...]) + sh_ref[...]
    hb = h.astype(BF16)

    u_out[...] = jnp.dot(hb, wu_ref[...], preferred_element_type=F32).astype(BF16)

    pos = pos_ref[...].astype(F32)
    ang = pos * inv_qk_ref[...]
    cos_qk = jnp.cos(ang)
    sin_qk = jnp.sin(ang)
    slo_qk = -sin_qk * mlo_qk_ref[...]
    shi_qk = sin_qk * mhi_qk_ref[...]
    ang = pos * inv_ix_ref[...]
    cos_ix = jnp.cos(ang)
    sin_ix = jnp.sin(ang)
    slo_ix = -sin_ix * mlo_ix_ref[...]
    shi_ix = sin_ix * mhi_ix_ref[...]
    half_qk = HEAD_DIM // ROPE_FRAC // 2
    half_ix = IDX_DIM // ROPE_FRAC // 2

    def head_norm(t, g):
        return t * lax.rsqrt(jnp.mean(t * t, axis=-1, keepdims=True) + EPS) * g

    q = jnp.dot(hb, wq_ref[...], preferred_element_type=F32)
    attn_scale = HEAD_DIM ** -0.5
    for hd in range(N_HEADS):
        sl = slice(hd * HEAD_DIM, (hd + 1) * HEAD_DIM)
        t = _rope(head_norm(q[:, sl], gq_ref[...]), cos_qk, slo_qk, shi_qk, half_qk)
        q_out[:, sl] = (t * attn_scale).astype(BF16)

    kv = jnp.dot(hb, wkv_ref[...], preferred_element_type=F32)
    nk = N_KV_HEADS * HEAD_DIM
    for hd in range(N_KV_HEADS):
        sl = slice(hd * HEAD_DIM, (hd + 1) * HEAD_DIM)
        t = _rope(head_norm(kv[:, sl], gk_ref[...]), cos_qk, slo_qk, shi_qk, half_qk)
        k_out[:, sl] = t.astype(BF16)
    vrows = HEAD_DIM + DSA_ONES
    ones = jnp.ones((DSA_ONES, DSA_TK), BF16)
    for t in range(v_out.shape[0]):
        for hd in range(N_KV_HEADS):
            vh = kv[t * DSA_TK:(t + 1) * DSA_TK, nk + hd * HEAD_DIM:nk + (hd + 1) * HEAD_DIM]
            v_out[t, hd * vrows:hd * vrows + HEAD_DIM, :] = vh.T.astype(BF16)
            v_out[t, hd * vrows + HEAD_DIM:(hd + 1) * vrows, :] = ones

    qi = jnp.dot(hb, wqi_ref[...], preferred_element_type=F32)
    idx_scale = IDX_DIM ** -0.5
    for pr in range(N_IDX_HEADS * IDX_DIM // LANES):
        sl = slice(pr * LANES, (pr + 1) * LANES)
        t = _rope(qi[:, sl], cos_ix, slo_ix, shi_ix, half_ix)
        qi_out[:, sl] = (t * idx_scale).astype(BF16)

    kw = jnp.dot(hb, wkw_ref[...], preferred_element_type=F32)
    ki_out[...] = _rope(kw[:, :LANES], cos_ix, slo_ix, shi_ix, half_ix).astype(BF16)
    wv_out[...] = kw[:, LANES:] * (N_IDX_HEADS ** -0.5)


def _inproj_call(x2, sc, sh, g, pos2, tabs, gq, gk, wu, wq, wkv, wqi, wkw, seq):
    n, d = x2.shape
    tm = INPROJ_TM
    per_b = seq // tm
    row = lambda i: (i, 0)
    modmap = lambda i: (i // per_b, 0, 0)
    small = [_const_spec(t.shape) for t in tabs]
    in_specs = ([pl.BlockSpec((tm, d), row),
                 pl.BlockSpec((None, 1, d), modmap),
                 pl.BlockSpec((None, 1, d), modmap),
                 _const_spec(g.shape),
                 pl.BlockSpec((tm, 1), row)]
                + small
                + [_const_spec(gq.shape), _const_spec(gk.shape),
                   _const_spec(wu.shape), _const_spec(wq.shape), _const_spec(wkv.shape),
                   _const_spec(wqi.shape), _const_spec(wkw.shape)])
    widths = (D_SSM, N_HEADS * HEAD_DIM, N_KV_HEADS * HEAD_DIM, N_KV_HEADS * HEAD_DIM,
              N_IDX_HEADS * IDX_DIM, LANES, LANES)
    dtypes = (BF16, BF16, BF16, BF16, BF16, BF16, F32)
    vt_rows = N_KV_HEADS * (HEAD_DIM + DSA_ONES)
    out_specs = [pl.BlockSpec((tm, w), row) for w in widths]
    out_shape = [jax.ShapeDtypeStruct((n, w), dt) for w, dt in zip(widths, dtypes)]
    out_specs[3] = pl.BlockSpec((tm // DSA_TK, vt_rows, DSA_TK), lambda i: (i, 0, 0))
    out_shape[3] = jax.ShapeDtypeStruct((n // DSA_TK, vt_rows, DSA_TK), BF16)
    return pl.pallas_call(
        _inproj_kernel,
        grid=(n // tm,),
        in_specs=in_specs,
        out_specs=out_specs,
        out_shape=out_shape,
        compiler_params=_cparams(("parallel",)),
        name="inproj",
    )(x2, sc, sh, g, pos2, *tabs, gq, gk, wu, wq, wkv, wqi, wkw)


def _s5prep_kernel(lr_ref, li_ref, dt_ref, bre_ref, bim_ref, pr_ref, pi_ref, bbr_ref, bbi_ref):
    lr = jnp.minimum(lr_ref[...], -1e-4)
    li = li_ref[...]
    dt = dt_ref[...]
    mag = jnp.exp(lr * dt)
    ab_r = mag * jnp.cos(li * dt)
    ab_i = mag * jnp.sin(li * dt)
    den = lr * lr + li * li
    fr = ((ab_r - 1.0) * lr + ab_i * li) / den
    fi = (ab_i * lr - (ab_r - 1.0) * li) / den
    bbr_ref[...] = fr * bre_ref[...] - fi * bim_ref[...]
    bbi_ref[...] = fr * bim_ref[...] + fi * bre_ref[...]
    steps = (lax.broadcasted_iota(I32, pr_ref.shape, 0) + 1).astype(F32)
    magn = jnp.exp(steps * (lr * dt))
    pr_ref[...] = magn * jnp.cos(steps * (li * dt))
    pi_ref[...] = magn * jnp.sin(steps * (li * dt))


def _s5prep_call(lr, li, dt, bre_t, bim_t, chunk):
    gp = lr.shape[1]
    hh = bre_t.shape[0]
    outs = [jax.ShapeDtypeStruct((chunk, gp), F32)] * 2 + [jax.ShapeDtypeStruct((hh, gp), F32)] * 2
    return pl.pallas_call(_s5prep_kernel, out_shape=outs, name="s5prep")(lr, li, dt, bre_t, bim_t)


def _s5_kernel(u_ref, bw_ref, cw_ref, pw_ref, dsk_ref, y_ref, st_ref, cin_ref, xs_ref):
    tt, rr, _ = u_ref.shape
    ns = st_ref.shape[1] // 2
    ar = pw_ref[0:1, :ns]
    ai = pw_ref[0:1, ns:]
    st_ref[...] = jnp.zeros(st_ref.shape, F32)

    def cproj(zr, zi):
        return (jnp.dot(zr.astype(BF16), cw_ref[:ns, :], preferred_element_type=F32)
                + jnp.dot(zi.astype(BF16), cw_ref[ns:, :], preferred_element_type=F32))

    xs_ref[0] = jnp.dot(u_ref[0], bw_ref[...], preferred_element_type=F32)

    def local_step(i, carry):
        x = xs_ref[i % 2]
        nxt = jnp.minimum(i + 1, tt - 1)
        xs_ref[(i + 1) % 2] = jnp.dot(u_ref[nxt], bw_ref[...], preferred_element_type=F32)
        sr = st_ref[:, :ns]
        si = st_ref[:, ns:]
        nr = ar * sr - ai * si + x[:, :ns]
        ni = ar * si + ai * sr + x[:, ns:]
        st_ref[:, :ns] = nr
        st_ref[:, ns:] = ni
        y_ref[i] = cproj(nr, ni) + dsk_ref[...] * u_ref[i].astype(F32)
        return carry

    lax.fori_loop(0, tt, local_step, 0)

    atr = pw_ref[tt - 1:tt, :ns]
    ati = pw_ref[tt - 1:tt, ns:]

    def chain(r, carry):
        cr, ci = carry
        cin_ref[pl.ds(r, 1), :ns] = cr
        cin_ref[pl.ds(r, 1), ns:] = ci
        er = st_ref[pl.ds(r, 1), :ns]
        ei = st_ref[pl.ds(r, 1), ns:]
        return (atr * cr - ati * ci + er, atr * ci + ati * cr + ei)

    zero = jnp.zeros((1, ns), F32)
    lax.fori_loop(0, rr, chain, (zero, zero))

    def carry_step(i, carry):
        pr = pw_ref[pl.ds(i, 1), :ns]
        pi = pw_ref[pl.ds(i, 1), ns:]
        cr = cin_ref[:, :ns]
        ci = cin_ref[:, ns:]
        y_ref[i] += cproj(pr * cr - pi * ci, pr * ci + pi * cr)
        return carry

    lax.fori_loop(0, tt, carry_step, 0)


def _s5_call(u_t, bw, cw, pw, dsk):
    bsz, tt, rr, dch = u_t.shape
    nb = bw.shape[0]
    cb = bw.shape[1]
    ns2 = bw.shape[2]
    return pl.pallas_call(
        _s5_kernel,
        grid=(bsz, nb),
        in_specs=[pl.BlockSpec((None, tt, rr, cb), lambda b, j: (b, 0, 0, j)),
                  pl.BlockSpec((None, cb, ns2), lambda b, j: (j, 0, 0)),
                  pl.BlockSpec((None, ns2, cb), lambda b, j: (j, 0, 0)),
                  pl.BlockSpec((None, tt, ns2), lambda b, j: (j, 0, 0)),
                  pl.BlockSpec((1, cb), lambda b, j: (0, j))],
        out_specs=pl.BlockSpec((None, tt, rr, cb), lambda b, j: (b, 0, 0, j)),
        out_shape=jax.ShapeDtypeStruct((bsz, tt, rr, dch), F32),
        scratch_shapes=[pltpu.VMEM((rr, ns2), F32), pltpu.VMEM((rr, ns2), F32),
                        pltpu.VMEM((2, rr, ns2), F32)],
        compiler_params=_cparams(("parallel", "parallel")),
        name="s5scan",
    )(u_t, bw, cw, pw, dsk)


def _order_key(x):
    bits = pltpu.bitcast(x, I32)
    return jnp.where(bits < 0, bits ^ jnp.int32(0x7FFFFFFF), bits)


def _dsa_kernel(q_ref, qi_ref, wv_ref, k_ref, vt_ref, ki_ref, o_ref,
                key_sc, kb_sc, lhs_sc, wt_sc, qs_sc, m_sc, acc_sc, s_sc, p_sc, al_sc, *, topk):
    tq = q_ref.shape[0]
    tk = key_sc.shape[1]
    qb = pl.program_id(1)
    nkt = (qb * tq + tq + tk - 1) // tk

    lane = lax.broadcasted_iota(I32, (tq, LANES), 1)
    kiota = lax.broadcasted_iota(I32, (tk, tq), 0)
    qpos = qb * tq + lax.broadcasted_iota(I32, (tk, tq), 1)

    for hd in range(N_IDX_HEADS):
        pair = qi_ref[:, (hd // 2) * LANES:(hd // 2 + 1) * LANES].astype(F32)
        keep = (lane < IDX_DIM) if hd % 2 == 0 else (lane >= IDX_DIM)
        lhs_sc[hd] = jnp.where(keep, pair, 0.0).T.astype(BF16)
    wt_sc[...] = wv_ref[...].T

    def score_tile(kt, carry):
        k0 = pl.multiple_of(kt * tk, tk)
        kit = ki_ref[pl.ds(k0, tk), :]
        acc = jnp.zeros((tk, tq), F32)
        for hd in range(N_IDX_HEADS):
            lg = jnp.dot(kit, lhs_sc[hd], preferred_element_type=F32)
            acc = acc + jnp.maximum(lg, 0.0) * wt_sc[hd:hd + 1, :]
        acc = jnp.where(kiota + k0 <= qpos, acc, -jnp.inf)
        bits = pltpu.bitcast(acc, I32)
        key_sc[kt] = jnp.where(bits < 0, bits ^ jnp.int32(0x7FFFFFFF), bits)
        kb_sc[kt] = pltpu.bitcast(bits & jnp.int32(-65536), F32).astype(BF16)
        return carry

    lax.fori_loop(0, nkt, score_tile, 0)

    def search(count, bit0, t0, n0):
        def cond(st):
            bit, _, cnt = st
            return jnp.logical_and(bit >= 0, jnp.max(jnp.abs(cnt - topk)) > 0)

        def body(st):
            bit, t, cnt = st
            for _ in range(2):
                cand = t + jnp.left_shift(jnp.int32(1), jnp.maximum(bit, 0))
                c = count(cand)
                ok = jnp.logical_and(c >= topk, bit >= 0)
                t = jnp.where(ok, cand, t)
                cnt = jnp.where(ok, c, cnt)
                bit = bit - 1
            return bit, t, cnt

        _, t, cnt = lax.while_loop(cond, body, (jnp.int32(bit0), t0, n0))
        return t, cnt

    one_b = jnp.ones((tk, tq), BF16)
    zero_b = jnp.zeros((tk, tq), BF16)
    sub = tk // 16

    def count16(c16):
        c16 = jnp.maximum(c16, KEY16_NEG_INF)
        pat = jnp.where(c16 >= 0, c16, c16 ^ jnp.int32(0x7FFF))
        cand = pltpu.bitcast(jnp.left_shift(pat, 16), F32).astype(BF16)

        def body(kt, acc):
            hit = jnp.where(kb_sc[kt] >= cand, one_b, zero_b)
            part = hit[0:16]
            for j in range(1, sub):
                part = part + hit[j * 16:(j + 1) * 16]
            return acc + part.astype(F32)

        acc = lax.fori_loop(0, nkt, body, jnp.zeros((16, tq), F32))
        return jnp.sum(acc, axis=0, keepdims=True).astype(I32)

    def count32(cand):
        def body(kt, acc):
            hit = jnp.where(key_sc[kt] >= cand, 1, 0).astype(I32)
            return acc + jnp.sum(hit.reshape(tk // 8, 8, tq), axis=0)

        acc = lax.fori_loop(0, nkt, body, jnp.zeros((8, tq), I32))
        return jnp.sum(acc, axis=0, keepdims=True)

    c0 = count16(jnp.zeros((1, tq), I32))
    pos_ok = c0 >= topk
    t16, _ = search(count16, 14,
                    jnp.where(pos_ok, 0, -32768).astype(I32),
                    jnp.where(pos_ok, c0, nkt * tk).astype(I32))
    t32 = jnp.left_shift(t16, 16)
    c32 = count32(t32)
    short = c32 < topk
    t32 = jnp.where(short, jnp.int32(-65536), t32)
    c32 = jnp.where(short, nkt * tk + 1, c32).astype(I32)
    thr, _ = search(count32, 15, t32, c32)
    thr = jnp.maximum(thr, KEY_NEG_INF + 1)

    for g in range(N_KV_HEADS):
        for r in range(KV_REP):
            hd = g * KV_REP + r
            qh = q_ref[:, hd * HEAD_DIM:(hd + 1) * HEAD_DIM].astype(F32)
            qs_sc[g, :, r * tq:(r + 1) * tq] = qh.T.astype(BF16)

    m_sc[...] = jnp.full(m_sc.shape, NEG_BIG, F32)
    acc_sc[...] = jnp.zeros(acc_sc.shape, F32)
    p_sc[1] = jnp.zeros(p_sc.shape[1:], BF16)
    al_sc[1] = jnp.ones(al_sc.shape[1:], F32)
    vrows = acc_sc.shape[1]

    def qk(kt, g):
        k0 = pl.multiple_of(kt * tk, tk)
        kg = k_ref[pl.ds(k0, tk), g * HEAD_DIM:(g + 1) * HEAD_DIM]
        s_sc[g] = jnp.dot(kg, qs_sc[g], preferred_element_type=F32)

    def softmax(g, bias):
        s = s_sc[g] + bias
        m_old = m_sc[g]
        m_new = jnp.maximum(m_old, jnp.max(s, axis=0, keepdims=True))
        al_sc[g] = jnp.exp(m_old - m_new)
        p_sc[g] = jnp.exp(s - m_new).astype(BF16)
        m_sc[g] = m_new

    def pv(kt, g):
        vtg = vt_ref[kt, g * vrows:(g + 1) * vrows, :]
        acc_sc[g] = al_sc[g] * acc_sc[g] + jnp.dot(vtg, p_sc[g], preferred_element_type=F32)

    def attn_tile(kt, carry):
        bias = jnp.where(key_sc[kt] >= thr, 0.0, NEG_BIG)
        bias = jnp.concatenate([bias] * KV_REP, axis=1)
        qk(kt, 1)
        softmax(0, bias)
        pv(jnp.maximum(kt - 1, 0), 1)
        qk(jnp.minimum(kt + 1, nkt - 1), 0)
        softmax(1, bias)
        pv(kt, 0)
        return carry

    qk(0, 0)
    lax.fori_loop(0, nkt, attn_tile, 0)
    pv(nkt - 1, 1)

    for g in range(N_KV_HEADS):
        out = acc_sc[g, :HEAD_DIM, :] / acc_sc[g, HEAD_DIM:HEAD_DIM + 1, :]
        for r in range(KV_REP):
            hd = g * KV_REP + r
            o_ref[:, hd * HEAD_DIM:(hd + 1) * HEAD_DIM] = out[:, r * tq:(r + 1) * tq].T.astype(o_ref.dtype)


def _dsa_call(q, qi, wv, k, vt, ki, topk):
    bsz, seq, dq = q.shape
    tq = DSA_TQ
    nkt, dv, tk = vt.shape[1:]
    nq = seq // tq
    qmap = lambda b, i: (b, i, 0)

    def whole(shape):
        return pl.BlockSpec((None,) + tuple(shape), lambda b, i: (b,) + (0,) * len(shape),
                            pipeline_mode=pl.Buffered(1))

    return pl.pallas_call(
        functools.partial(_dsa_kernel, topk=topk),
        grid=(bsz, nq),
        in_specs=[pl.BlockSpec((None, tq, dq), qmap),
                  pl.BlockSpec((None, tq, qi.shape[2]), qmap),
                  pl.BlockSpec((None, tq, wv.shape[2]), qmap),
                  whole(k.shape[1:]), whole(vt.shape[1:]), whole(ki.shape[1:])],
        out_specs=pl.BlockSpec((None, tq, dq), qmap),
        out_shape=jax.ShapeDtypeStruct((bsz, seq, dq), BF16),
        scratch_shapes=[pltpu.VMEM((nkt, tk, tq), I32),
                        pltpu.VMEM((nkt, tk, tq), BF16),
                        pltpu.VMEM((N_IDX_HEADS, LANES, tq), BF16),
                        pltpu.VMEM((LANES, tq), F32),
                        pltpu.VMEM((N_KV_HEADS, HEAD_DIM, KV_REP * tq), BF16),
                        pltpu.VMEM((N_KV_HEADS, 1, KV_REP * tq), F32),
                        pltpu.VMEM((N_KV_HEADS, dv // N_KV_HEADS, KV_REP * tq), F32),
                        pltpu.VMEM((N_KV_HEADS, tk, KV_REP * tq), F32),
                        pltpu.VMEM((N_KV_HEADS, tk, KV_REP * tq), BF16),
                        pltpu.VMEM((N_KV_HEADS, 1, KV_REP * tq), F32)],
        compiler_params=_cparams(("parallel", "arbitrary")),
        name="dsa",
    )(q, qi, wv, k, vt, ki)


def _outproj_kernel(x_ref, ys_ref, ya_ref, gt_ref, sc_ref, sh_ref, wglu_ref, bglu_ref,
                    gs_ref, ga_ref, wout_ref, gm_ref, x1_out, h2_out):
    def rms(t, g):
        return t * lax.rsqrt(jnp.mean(t * t, axis=-1, keepdims=True) + EPS) * g

    y = ys_ref[...]
    ya = 0.5 * y * (1.0 + jnp.tanh(math.sqrt(2.0 / math.pi) * (y + 0.044715 * (y * y * y))))
    z = jnp.dot(ya.astype(BF16), wglu_ref[...], preferred_element_type=F32) + bglu_ref[...]
    o = ya / (1.0 + jnp.exp(-z))
    n1 = rms(o, gs_ref[...]).astype(BF16)
    n2 = rms(ya_ref[...].astype(F32), ga_ref[...]).astype(BF16)
    d1 = n1.shape[1]
    mixw = (jnp.dot(n1, wout_ref[:d1, :], preferred_element_type=F32)
            + jnp.dot(n2, wout_ref[d1:, :], preferred_element_type=F32))
    x1 = x_ref[...] + gt_ref[...] * mixw
    x1_out[...] = x1
    h2 = rms(x1, gm_ref[...]) * (1.0 + sc_ref[...]) + sh_ref[...]
    h2_out[...] = h2.astype(BF16)


def _outproj_call(x2, ys, ya, gt, sc, sh, wglu, bglu, gs, ga, wout, gm, seq):
    n, d = x2.shape
    tm = OUT_TM
    per_b = seq // tm
    row = lambda i: (i, 0)
    modmap = lambda i: (i // per_b, 0, 0)
    return pl.pallas_call(
        _outproj_kernel,
        grid=(n // tm,),
        in_specs=[pl.BlockSpec((tm, d), row),
                  pl.BlockSpec((tm, ys.shape[1]), row),
                  pl.BlockSpec((tm, ya.shape[1]), row),
                  pl.BlockSpec((None, 1, d), modmap),
                  pl.BlockSpec((None, 1, d), modmap),
                  pl.BlockSpec((None, 1, d), modmap),
                  _const_spec(wglu.shape), _const_spec(bglu.shape),
                  _const_spec(gs.shape), _const_spec(ga.shape),
                  _const_spec(wout.shape), _const_spec(gm.shape)],
        out_specs=[pl.BlockSpec((tm, d), row), pl.BlockSpec((tm, d), row)],
        out_shape=[jax.ShapeDtypeStruct((n, d), F32), jax.ShapeDtypeStruct((n, d), BF16)],
        compiler_params=_cparams(("parallel",)),
        name="outproj",
    )(x2, ys, ya, gt, sc, sh, wglu, bglu, gs, ga, wout, gm)


def _mlp_kernel(h_ref, x1_ref, gt_ref, w1_ref, w2_ref, o_ref):
    j = pl.program_id(1)

    @pl.when(j == 0)
    def _():
        o_ref[...] = jnp.zeros(o_ref.shape, F32)

    a = jnp.maximum(jnp.dot(h_ref[...], w1_ref[...], preferred_element_type=F32), 0.0)
    o_ref[...] += jnp.dot((a * a).astype(BF16), w2_ref[...], preferred_element_type=F32)

    @pl.when(j == pl.num_programs(1) - 1)
    def _():
        o_ref[...] = x1_ref[...] + gt_ref[...] * o_ref[...]


def _mlp_call(h2, x1, gt, w1, w2, seq):
    n, d = x1.shape
    dff = w1.shape[1]
    tm, tf = MLP_TM, MLP_TF
    per_b = seq // tm
    return pl.pallas_call(
        _mlp_kernel,
        grid=(n // tm, dff // tf),
        in_specs=[pl.BlockSpec((tm, d), lambda i, j: (i, 0), pipeline_mode=pl.Buffered(1)),
                  pl.BlockSpec((tm, d), lambda i, j: (i, 0), pipeline_mode=pl.Buffered(1)),
                  pl.BlockSpec((None, 1, d), lambda i, j: (i // per_b, 0, 0)),
                  pl.BlockSpec((d, tf), lambda i, j: (0, j)),
                  pl.BlockSpec((tf, d), lambda i, j: (j, 0))],
        out_specs=pl.BlockSpec((tm, d), lambda i, j: (i, 0)),
        out_shape=jax.ShapeDtypeStruct((n, d), F32),
        compiler_params=_cparams(("parallel", "arbitrary")),
        name="mlp",
    )(h2, x1, gt, w1, w2)


def _rope_tables():
    def tables(head_dim, per_vreg):
        r = head_dim // ROPE_FRAC
        half = r // 2
        inv = ROPE_THETA ** (-jnp.arange(half, dtype=F32) / half)
        lane = np.arange(LANES) % head_dim
        inv_l = jnp.where(lane < r, inv[lane % half], 0.0).astype(F32)[None, :]
        lo = jnp.asarray((lane < half).astype(np.float32))[None, :]
        hi = jnp.asarray(((lane >= half) & (lane < r)).astype(np.float32))[None, :]
        return inv_l, lo, hi

    inv_qk, lo_qk, hi_qk = tables(HEAD_DIM, 1)
    inv_ix, lo_ix, hi_ix = tables(IDX_DIM, 2)
    return [inv_qk, inv_ix, lo_qk, hi_qk, lo_ix, hi_ix]


def _block_diag(m):
    nb, g, a, b = m.shape
    eye = jnp.eye(g, dtype=m.dtype)
    return jnp.einsum('ngab,gk->ngakb', m, eye).reshape(nb, g * a, g * b)


def kernel(x, c, positions, w_ada, b_ada, g_norm_mix, w_in, lam_re, lam_im, log_dt,
           b_re, b_im, c_re, c_im, d_skip, w_glu, b_glu, g_q, g_k, g_out_ssm, g_out_attn,
           w_out, g_norm_mlp, w_mlp_in, w_mlp_out):
    bsz, seq, d = x.shape
    depth = w_ada.shape[0]
    n = bsz * seq
    topk = min(TOPK_MAX, seq // 4)
    gg, pp, hh = N_SSM_GROUPS, SSM_STATE, SSM_GROUP
    nblk = gg // S5_GB
    chunk = S5_CHUNK
    nchunks = seq // chunk

    c_pad = jnp.zeros((8, d), F32).at[:bsz].set(c.astype(F32))
    pos2 = positions.reshape(n, 1).astype(I32)
    tabs = _rope_tables()
    xcur = x.reshape(n, d).astype(F32)

    for l in range(depth):
        mod = _ada_call(c_pad, w_ada[l], b_ada[l][None, :])[:bsz]
        mod = mod.reshape(bsz, N_MOD, 1, d)
        sh_a, sc_a, gt_a, sh_m, sc_m, gt_m = [mod[:, i] for i in range(N_MOD)]

        wi = w_in[l]
        o0 = D_SSM
        o1 = o0 + N_HEADS * HEAD_DIM
        o2 = o1 + 2 * N_KV_HEADS * HEAD_DIM
        o3 = o2 + N_IDX_HEADS * IDX_DIM
        o4 = o3 + IDX_DIM
        w_ki = wi[:, o3:o4]
        w_wi = wi[:, o4:]
        w_kw = jnp.concatenate(
            [w_ki, w_ki, w_wi, jnp.zeros((d, 2 * LANES - 2 * IDX_DIM - N_IDX_HEADS), wi.dtype)], axis=1)
        u, q, k, v, qi, ki, wv = _inproj_call(
            xcur, sc_a, sh_a, g_norm_mix[l][None, :], pos2, tabs,
            g_q[l][None, :], g_k[l][None, :],
            wi[:, :o0].astype(BF16), wi[:, o0:o1].astype(BF16), wi[:, o1:o2].astype(BF16),
            wi[:, o2:o3].astype(BF16), w_kw.astype(BF16), seq)

        lr = lam_re[l].astype(F32).reshape(1, gg * pp)
        li = lam_im[l].astype(F32).reshape(1, gg * pp)
        dt = jnp.broadcast_to(jnp.exp(log_dt[l].astype(F32))[:, None], (gg, pp)).reshape(1, gg * pp)
        bre_t = b_re[l].astype(F32).transpose(2, 0, 1).reshape(hh, gg * pp)
        bim_t = b_im[l].astype(F32).transpose(2, 0, 1).reshape(hh, gg * pp)
        pw_r, pw_i, bb_r, bb_i = _s5prep_call(lr, li, dt, bre_t, bim_t, chunk)
        bbr = bb_r.reshape(hh, nblk, S5_GB, pp).transpose(1, 2, 0, 3)
        bbi = bb_i.reshape(hh, nblk, S5_GB, pp).transpose(1, 2, 0, 3)
        bw = jnp.concatenate([_block_diag(bbr), _block_diag(bbi)], axis=2).astype(BF16)
        cr = c_re[l].astype(F32).reshape(nblk, S5_GB, hh, pp).transpose(0, 1, 3, 2)
        ci = c_im[l].astype(F32).reshape(nblk, S5_GB, hh, pp).transpose(0, 1, 3, 2)
        cw = jnp.concatenate([_block_diag(cr), -_block_diag(ci)], axis=1).astype(BF16)
        ns = S5_GB * pp
        pw = jnp.concatenate([pw_r.reshape(chunk, nblk, ns), pw_i.reshape(chunk, nblk, ns)],
                             axis=2).transpose(1, 0, 2)
        u_t = u.reshape(bsz, nchunks, chunk, D_SSM).transpose(0, 2, 1, 3)
        y_t = _s5_call(u_t, bw, cw, pw, d_skip[l].astype(F32)[None, :])
        y_ssm = y_t.transpose(0, 2, 1, 3).reshape(n, D_SSM)

        v_t = v.reshape(bsz, seq // DSA_TK, v.shape[1], DSA_TK)
        y_attn = _dsa_call(q.reshape(bsz, seq, -1), qi.reshape(bsz, seq, -1),
                           wv.reshape(bsz, seq, -1), k.reshape(bsz, seq, -1),
                           v_t, ki.reshape(bsz, seq, -1), topk)
        y_attn = y_attn.reshape(n, -1)

        x1, h2 = _outproj_call(
            xcur, y_ssm, y_attn, gt_a, sc_m, sh_m,
            w_glu[l].astype(BF16), b_glu[l].astype(F32)[None, :],
            g_out_ssm[l].astype(F32)[None, :], g_out_attn[l].astype(F32)[None, :],
            w_out[l].astype(BF16), g_norm_mlp[l].astype(F32)[None, :], seq)

        xcur = _mlp_call(h2, x1, gt_m, w_mlp_in[l].astype(BF16), w_mlp_out[l].astype(BF16), seq)

    return xcur.reshape(bsz, seq, d).astype(x.dtype)
```

```python
import functools
import math

import numpy as np
import jax
import jax.numpy as jnp
from jax import lax
from jax.experimental import pallas as pl
from jax.experimental.pallas import tpu as pltpu

F32 = jnp.float32
BF16 = jnp.bfloat16
I32 = jnp.int32

D_MODEL = 2048
D_SSM = 1024
SSM_GROUP = 16
N_SSM_GROUPS = 64
SSM_STATE = 64
HEAD_DIM = 128
N_HEADS = 8
N_KV_HEADS = 2
KV_REP = N_HEADS // N_KV_HEADS
N_IDX_HEADS = 16
IDX_DIM = 64
TOPK_MAX = 256
ROPE_FRAC = 4
ROPE_THETA = 500000.0
D_FF = 4 * D_MODEL
EPS = 1e-6
N_MOD = 6

LANES = 128
VMEM_LIMIT = 56 * 1024 * 1024

ADA_TN = 1024
INPROJ_TM = 512
S5_CHUNK = 32
S5_GB = 16
DSA_TQ = 256
DSA_TK = 256
DSA_ONES = 16
OUT_TM = 256
MLP_TM = 1024
MLP_TF = 512

NEG_BIG = -1e30
INT_MIN = -2147483648
KEY_NEG_INF = -2139095041
KEY16_NEG_INF = KEY_NEG_INF >> 16
KEY16_BAND_LO = -129
KEY16_BAND_HI = 127
SHIFT_MARGIN = 1.01
MAX_SPAN = 80.0


def _cparams(sem):
    return pltpu.CompilerParams(dimension_semantics=sem, vmem_limit_bytes=VMEM_LIMIT)


def _const_spec(shape):
    nd = len(shape)
    return pl.BlockSpec(shape, lambda *_: (0,) * nd, pipeline_mode=pl.Buffered(1))


def _ada_kernel(c_ref, w_ref, b_ref, o_ref):
    c = c_ref[...]
    act = c / (1.0 + jnp.exp(-c))
    o_ref[...] = jnp.dot(act.astype(BF16), w_ref[...].astype(BF16),
                         preferred_element_type=F32) + b_ref[...]


def _ada_call(c_pad, w_ada, b_ada):
    rows, d = c_pad.shape
    n = w_ada.shape[1]
    return pl.pallas_call(
        _ada_kernel,
        grid=(n // ADA_TN,),
        in_specs=[pl.BlockSpec((rows, d), lambda j: (0, 0)),
                  pl.BlockSpec((d, ADA_TN), lambda j: (0, j)),
                  pl.BlockSpec((1, ADA_TN), lambda j: (0, j))],
        out_specs=pl.BlockSpec((rows, ADA_TN), lambda j: (0, j)),
        out_shape=jax.ShapeDtypeStruct((rows, n), F32),
        compiler_params=_cparams(("arbitrary",)),
        name="ada",
    )(c_pad, w_ada, b_ada)


def _rope(x, cos, sin_lo, sin_hi, half):
    n = x.shape[-1]
    return (x * cos + pltpu.roll(x, n - half, 1) * sin_lo + pltpu.roll(x, half, 1) * sin_hi)


def _inproj_kernel(x_ref, sc_ref, sh_ref, g_ref, pos_ref, inv_qk_ref, inv_ix_ref,
                   mlo_qk_ref, mhi_qk_ref, mlo_ix_ref, mhi_ix_ref, gq_ref, gk_ref,
                   wu_ref, wq_ref, wkv_ref, wqi_ref, wkw_ref,
                   u_out, q_out, k_out, v_out, qi_out, ki_out, wv_out):
    x = x_ref[...]
    ms = jnp.mean(x * x, axis=-1, keepdims=True)
    h = x * lax.rsqrt(ms + EPS) * g_ref[...]
    h = h * (1.0 + sc_ref[...]) + sh_ref[...]
    hb = h.astype(BF16)

    u_out[...] = jnp.dot(hb, wu_ref[...], preferred_element_type=F32).astype(BF16)

    pos = pos_ref[...].astype(F32)
    ang = pos * inv_qk_ref[...]
    cos_qk = jnp.cos(ang)
    sin_qk = jnp.sin(ang)
    slo_qk = -sin_qk * mlo_qk_ref[...]
    shi_qk = sin_qk * mhi_qk_ref[...]
    ang = pos * inv_ix_ref[...]
    cos_ix = jnp.cos(ang)
    sin_ix = jnp.sin(ang)
    slo_ix = -sin_ix * mlo_ix_ref[...]
    shi_ix = sin_ix * mhi_ix_ref[...]
    half_qk = HEAD_DIM // ROPE_FRAC // 2
    half_ix = IDX_DIM // ROPE_FRAC // 2

    def head_norm(t, g):
        return t * lax.rsqrt(jnp.mean(t * t, axis=-1, keepdims=True) + EPS) * g

    q = jnp.dot(hb, wq_ref[...], preferred_element_type=F32)
    attn_scale = HEAD_DIM ** -0.5
    for hd in range(N_HEADS):
        sl = slice(hd * HEAD_DIM, (hd + 1) * HEAD_DIM)
        t = _rope(head_norm(q[:, sl], gq_ref[...]), cos_qk, slo_qk, shi_qk, half_qk)
        q_out[:, sl] = (t * attn_scale).astype(BF16)

    kv = jnp.dot(hb, wkv_ref[...], preferred_element_type=F32)
    nk = N_KV_HEADS * HEAD_DIM
    one_col = (lax.broadcasted_iota(I32, (x.shape[0], HEAD_DIM), 1) == 0).astype(BF16)
    for hd in range(N_KV_HEADS):
        sl = slice(hd * HEAD_DIM, (hd + 1) * HEAD_DIM)
        t = _rope(head_norm(kv[:, sl], gk_ref[...]), cos_qk, slo_qk, shi_qk, half_qk)
        k_out[:, 2 * hd * HEAD_DIM:(2 * hd + 1) * HEAD_DIM] = t.astype(BF16)
        k_out[:, (2 * hd + 1) * HEAD_DIM:(2 * hd + 2) * HEAD_DIM] = one_col
    vrows = HEAD_DIM + DSA_ONES
    ones = jnp.ones((DSA_ONES, DSA_TK), BF16)
    for t in range(v_out.shape[0]):
        for hd in range(N_KV_HEADS):
            vh = kv[t * DSA_TK:(t + 1) * DSA_TK, nk + hd * HEAD_DIM:nk + (hd + 1) * HEAD_DIM]
            v_out[t, hd * vrows:hd * vrows + HEAD_DIM, :] = vh.T.astype(BF16)
            v_out[t, hd * vrows + HEAD_DIM:(hd + 1) * vrows, :] = ones

    qi = jnp.dot(hb, wqi_ref[...], preferred_element_type=F32)
    idx_scale = IDX_DIM ** -0.5
    for pr in range(N_IDX_HEADS * IDX_DIM // LANES):
        sl = slice(pr * LANES, (pr + 1) * LANES)
        t = _rope(qi[:, sl], cos_ix, slo_ix, shi_ix, half_ix)
        qi_out[:, sl] = (t * idx_scale).astype(BF16)

    kw = jnp.dot(hb, wkw_ref[...], preferred_element_type=F32)
    ki_out[...] = _rope(kw[:, :LANES], cos_ix, slo_ix, shi_ix, half_ix).astype(BF16)
    wv_out[...] = kw[:, LANES:] * (N_IDX_HEADS ** -0.5)


def _inproj_call(x2, sc, sh, g, pos2, tabs, gq, gk, wu, wq, wkv, wqi, wkw, seq):
    n, d = x2.shape
    tm = INPROJ_TM
    per_b = seq // tm
    row = lambda i: (i, 0)
    modmap = lambda i: (i // per_b, 0, 0)
    small = [_const_spec(t.shape) for t in tabs]
    in_specs = ([pl.BlockSpec((tm, d), row),
                 pl.BlockSpec((None, 1, d), modmap),
                 pl.BlockSpec((None, 1, d), modmap),
                 _const_spec(g.shape),
                 pl.BlockSpec((tm, 1), row)]
                + small
                + [_const_spec(gq.shape), _const_spec(gk.shape),
                   _const_spec(wu.shape), _const_spec(wq.shape), _const_spec(wkv.shape),
                   _const_spec(wqi.shape), _const_spec(wkw.shape)])
    widths = (D_SSM, N_HEADS * HEAD_DIM, 2 * N_KV_HEADS * HEAD_DIM, N_KV_HEADS * HEAD_DIM,
              N_IDX_HEADS * IDX_DIM, LANES, LANES)
    dtypes = (BF16, BF16, BF16, BF16, BF16, BF16, F32)
    vt_rows = N_KV_HEADS * (HEAD_DIM + DSA_ONES)
    out_specs = [pl.BlockSpec((tm, w), row) for w in widths]
    out_shape = [jax.ShapeDtypeStruct((n, w), dt) for w, dt in zip(widths, dtypes)]
    out_specs[3] = pl.BlockSpec((tm // DSA_TK, vt_rows, DSA_TK), lambda i: (i, 0, 0))
    out_shape[3] = jax.ShapeDtypeStruct((n // DSA_TK, vt_rows, DSA_TK), BF16)
    return pl.pallas_call(
        _inproj_kernel,
        grid=(n // tm,),
        in_specs=in_specs,
        out_specs=out_specs,
        out_shape=out_shape,
        compiler_params=_cparams(("parallel",)),
        name="inproj",
    )(x2, sc, sh, g, pos2, *tabs, gq, gk, wu, wq, wkv, wqi, wkw)


def _s5prep_kernel(lr_ref, li_ref, dt_ref, bre_ref, bim_ref, pr_ref, pi_ref, bbr_ref, bbi_ref):
    lr = jnp.minimum(lr_ref[...], -1e-4)
    li = li_ref[...]
    dt = dt_ref[...]
    mag = jnp.exp(lr * dt)
    ab_r = mag * jnp.cos(li * dt)
    ab_i = mag * jnp.sin(li * dt)
    den = lr * lr + li * li
    fr = ((ab_r - 1.0) * lr + ab_i * li) / den
    fi = (ab_i * lr - (ab_r - 1.0) * li) / den
    bbr_ref[...] = fr * bre_ref[...] - fi * bim_ref[...]
    bbi_ref[...] = fr * bim_ref[...] + fi * bre_ref[...]
    steps = (lax.broadcasted_iota(I32, pr_ref.shape, 0) + 1).astype(F32)
    magn = jnp.exp(steps * (lr * dt))
    pr_ref[...] = magn * jnp.cos(steps * (li * dt))
    pi_ref[...] = magn * jnp.sin(steps * (li * dt))


def _s5prep_call(lr, li, dt, bre_t, bim_t, chunk):
    gp = lr.shape[1]
    hh = bre_t.shape[0]
    outs = [jax.ShapeDtypeStruct((chunk, gp), F32)] * 2 + [jax.ShapeDtypeStruct((hh, gp), F32)] * 2
    return pl.pallas_call(_s5prep_kernel, out_shape=outs, name="s5prep")(lr, li, dt, bre_t, bim_t)


def _s5_kernel(u_ref, bw_ref, cw_ref, pw_ref, dsk_ref, y_ref, st_ref, cin_ref, xs_ref):
    tt, rr, _ = u_ref.shape
    ns = st_ref.shape[1] // 2
    ar = pw_ref[0:1, :ns]
    ai = pw_ref[0:1, ns:]
    st_ref[...] = jnp.zeros(st_ref.shape, F32)

    def cproj(zr, zi):
        return (jnp.dot(zr.astype(BF16), cw_ref[:ns, :], preferred_element_type=F32)
                + jnp.dot(zi.astype(BF16), cw_ref[ns:, :], preferred_element_type=F32))

    xs_ref[0] = jnp.dot(u_ref[0], bw_ref[...], preferred_element_type=F32)

    def local_step(i, carry):
        x = xs_ref[i % 2]
        nxt = jnp.minimum(i + 1, tt - 1)
        xs_ref[(i + 1) % 2] = jnp.dot(u_ref[nxt], bw_ref[...], preferred_element_type=F32)
        sr = st_ref[:, :ns]
        si = st_ref[:, ns:]
        nr = ar * sr - ai * si + x[:, :ns]
        ni = ar * si + ai * sr + x[:, ns:]
        st_ref[:, :ns] = nr
        st_ref[:, ns:] = ni
        y_ref[i] = cproj(nr, ni) + dsk_ref[...] * u_ref[i].astype(F32)
        return carry

    lax.fori_loop(0, tt, local_step, 0)

    atr = pw_ref[tt - 1:tt, :ns]
    ati = pw_ref[tt - 1:tt, ns:]

    def chain(r, carry):
        cr, ci = carry
        cin_ref[pl.ds(r, 1), :ns] = cr
        cin_ref[pl.ds(r, 1), ns:] = ci
        er = st_ref[pl.ds(r, 1), :ns]
        ei = st_ref[pl.ds(r, 1), ns:]
        return (atr * cr - ati * ci + er, atr * ci + ati * cr + ei)

    zero = jnp.zeros((1, ns), F32)
    lax.fori_loop(0, rr, chain, (zero, zero))

    def carry_step(i, carry):
        pr = pw_ref[pl.ds(i, 1), :ns]
        pi = pw_ref[pl.ds(i, 1), ns:]
        cr = cin_ref[:, :ns]
        ci = cin_ref[:, ns:]
        y_ref[i] += cproj(pr * cr - pi * ci, pr * ci + pi * cr)
        return carry

    lax.fori_loop(0, tt, carry_step, 0)


def _s5_call(u_t, bw, cw, pw, dsk):
    bsz, tt, rr, dch = u_t.shape
    nb = bw.shape[0]
    cb = bw.shape[1]
    ns2 = bw.shape[2]
    return pl.pallas_call(
        _s5_kernel,
        grid=(bsz, nb),
        in_specs=[pl.BlockSpec((None, tt, rr, cb), lambda b, j: (b, 0, 0, j)),
                  pl.BlockSpec((None, cb, ns2), lambda b, j: (j, 0, 0)),
                  pl.BlockSpec((None, ns2, cb), lambda b, j: (j, 0, 0)),
                  pl.BlockSpec((None, tt, ns2), lambda b, j: (j, 0, 0)),
                  pl.BlockSpec((1, cb), lambda b, j: (0, j))],
        out_specs=pl.BlockSpec((None, tt, rr, cb), lambda b, j: (b, 0, 0, j)),
        out_shape=jax.ShapeDtypeStruct((bsz, tt, rr, dch), F32),
        scratch_shapes=[pltpu.VMEM((rr, ns2), F32), pltpu.VMEM((rr, ns2), F32),
                        pltpu.VMEM((2, rr, ns2), F32)],
        compiler_params=_cparams(("parallel", "parallel")),
        name="s5scan",
    )(u_t, bw, cw, pw, dsk)


def _order_key(x):
    bits = pltpu.bitcast(x, I32)
    return jnp.where(bits < 0, bits ^ jnp.int32(0x7FFFFFFF), bits)


def _dsa_kernel(q_ref, qi_ref, wv_ref, k_ref, vt_ref, ki_ref, o_ref,
                key_sc, kb_sc, lhs_sc, wt_sc, qs_sc, m_sc, acc_sc, s_sc, p_sc, al_sc, kn_sm,
                *, topk):
    tq = q_ref.shape[0]
    tk = key_sc.shape[1]
    qb = pl.program_id(1)
    nkt = (qb * tq + tq + tk - 1) // tk

    lane = lax.broadcasted_iota(I32, (tq, LANES), 1)
    kiota = lax.broadcasted_iota(I32, (tk, tq), 0)
    qpos = qb * tq + lax.broadcasted_iota(I32, (tk, tq), 1)

    @pl.when(qb == 0)
    def _():
        rows = min(k_ref.shape[0], 1024)
        for g in range(N_KV_HEADS):
            def norm_max(c, best):
                r0 = pl.multiple_of(c * rows, rows)
                kk = k_ref[pl.ds(r0, rows), g * 2 * HEAD_DIM:g * 2 * HEAD_DIM + HEAD_DIM].astype(F32)
                return jnp.maximum(best, jnp.max(jnp.sum(kk * kk, axis=1, keepdims=True)))

            kn_sm[g] = lax.fori_loop(0, k_ref.shape[0] // rows, norm_max, jnp.float32(0.0))

    for hd in range(N_IDX_HEADS):
        pair = qi_ref[:, (hd // 2) * LANES:(hd // 2 + 1) * LANES].astype(F32)
        keep = (lane < IDX_DIM) if hd % 2 == 0 else (lane >= IDX_DIM)
        lhs_sc[hd] = jnp.where(keep, pair, 0.0).T.astype(BF16)
    wt_sc[...] = wv_ref[...].T

    def score_tile(kt, carry):
        k0 = pl.multiple_of(kt * tk, tk)
        kit = ki_ref[pl.ds(k0, tk), :]
        acc = jnp.zeros((tk, tq), F32)
        for hd in range(N_IDX_HEADS):
            lg = jnp.dot(kit, lhs_sc[hd], preferred_element_type=F32)
            acc = acc + jnp.maximum(lg, 0.0) * wt_sc[hd:hd + 1, :]
        acc = jnp.where(kiota + k0 <= qpos, acc, -jnp.inf)
        bits = pltpu.bitcast(acc, I32)
        key_sc[kt] = jnp.where(bits < 0, bits ^ jnp.int32(0x7FFFFFFF), bits)
        kb_sc[kt] = pltpu.bitcast(bits & jnp.int32(-65536), F32).astype(BF16)
        return carry

    lax.fori_loop(0, nkt, score_tile, 0)

    def search(count, bit0, t0, n0):
        def cond(st):
            bit, _, cnt = st
            return jnp.logical_and(bit >= 0, jnp.max(jnp.abs(cnt - topk)) > 0)

        def body(st):
            bit, t, cnt = st
            for _ in range(2):
                cand = t + jnp.left_shift(jnp.int32(1), jnp.maximum(bit, 0))
                c = count(cand)
                ok = jnp.logical_and(c >= topk, bit >= 0)
                t = jnp.where(ok, cand, t)
                cnt = jnp.where(ok, c, cnt)
                bit = bit - 1
            return bit, t, cnt

        _, t, cnt = lax.while_loop(cond, body, (jnp.asarray(bit0, I32), t0, n0))
        return t, cnt

    one_b = jnp.ones((tk, tq), BF16)
    zero_b = jnp.zeros((tk, tq), BF16)
    sub = tk // 16

    def count16(c16):
        c16 = jnp.maximum(c16, KEY16_NEG_INF)
        pat = jnp.where(c16 >= 0, c16, c16 ^ jnp.int32(0x7FFF))
        cand = pltpu.bitcast(jnp.left_shift(pat, 16), F32).astype(BF16)

        def body(kt, acc):
            hit = jnp.where(kb_sc[kt] >= cand, one_b, zero_b)
            part = hit[0:16]
            for j in range(1, sub):
                part = part + hit[j * 16:(j + 1) * 16]
            return acc + part.astype(F32)

        acc = lax.fori_loop(0, nkt, body, jnp.zeros((16, tq), F32))
        return jnp.sum(acc, axis=0, keepdims=True).astype(I32)

    def count32(cand):
        def body(kt, acc):
            hit = jnp.where(key_sc[kt] >= cand, 1, 0).astype(I32)
            return acc + jnp.sum(hit.reshape(tk // 8, 8, tq), axis=0)

        acc = lax.fori_loop(0, nkt, body, jnp.zeros((8, tq), I32))
        return jnp.sum(acc, axis=0, keepdims=True)

    c0 = count16(jnp.zeros((1, tq), I32))
    pos_ok = c0 >= topk
    t16, c16 = search(count16, 14,
                      jnp.where(pos_ok, 0, -32768).astype(I32),
                      jnp.where(pos_ok, c0, nkt * tk).astype(I32))
    band = jnp.logical_and(t16 >= KEY16_BAND_LO, t16 <= KEY16_BAND_HI)
    t32 = jnp.left_shift(jnp.where(band, KEY16_BAND_LO, t16), 16)
    c32 = jnp.where(band, nkt * tk + 1, c16).astype(I32)
    any_band = jnp.max(band.astype(I32)) > 0
    thr, cnt = search(count32, jnp.where(any_band, 24, 15), t32, c32)
    thr = jnp.maximum(thr, KEY_NEG_INF + 1)

    @pl.when(jnp.max(cnt) > topk)
    def _():
        need = topk - count32(thr + 1)
        tri = (lax.broadcasted_iota(I32, (tk, tk), 1)
               <= lax.broadcasted_iota(I32, (tk, tk), 0)).astype(BF16)

        def demote(kt, seen):
            keys = key_sc[kt]
            eq = keys == thr
            rank = seen + jnp.dot(tri, jnp.where(eq, 1.0, 0.0).astype(BF16),
                                  preferred_element_type=F32)
            key_sc[kt] = jnp.where(jnp.logical_and(eq, rank > need.astype(F32)), thr - 1, keys)
            return rank[tk - 1:tk, :]

        lax.fori_loop(0, nkt, demote, jnp.zeros((1, tq), F32))

    kv_w = 2 * HEAD_DIM
    shifts = []
    for g in range(N_KV_HEADS):
        parts = []
        for r in range(KV_REP):
            hd = g * KV_REP + r
            qht = q_ref[:, hd * HEAD_DIM:(hd + 1) * HEAD_DIM].astype(F32).T
            qs_sc[g, :HEAD_DIM, r * tq:(r + 1) * tq] = qht.astype(BF16)
            qn2 = jnp.sum(qht * qht, axis=0, keepdims=True)
            parts.append(jnp.sqrt(qn2 * kn_sm[g]) * SHIFT_MARGIN)
        shifts.append(jnp.concatenate(parts, axis=1))
    bound = jnp.maximum(jnp.max(shifts[0]), jnp.max(shifts[1]))
    fast = 2.0 * bound <= MAX_SPAN
    row0 = lax.broadcasted_iota(I32, (HEAD_DIM, KV_REP * tq), 0) == 0
    for g in range(N_KV_HEADS):
        shift_row = jnp.where(fast, -shifts[g], 0.0)
        qs_sc[g, HEAD_DIM:, :] = jnp.where(row0, shift_row, 0.0).astype(BF16)

    vrows = acc_sc.shape[1]

    def qk(kt, g):
        k0 = pl.multiple_of(kt * tk, tk)
        kg = k_ref[pl.ds(k0, tk), g * kv_w:(g + 1) * kv_w]
        s_sc[g] = jnp.dot(kg, qs_sc[g], preferred_element_type=F32)

    def softmax_shifted(g, bias):
        p_sc[g] = jnp.exp(s_sc[g] + bias).astype(BF16)

    def pv_shifted(kt, g):
        vtg = vt_ref[kt, g * vrows:(g + 1) * vrows, :]
        acc_sc[g] += jnp.dot(vtg, p_sc[g], preferred_element_type=F32)

    def softmax_online(g, bias):
        s = s_sc[g] + bias
        m_old = m_sc[g]
        m_new = jnp.maximum(m_old, jnp.max(s, axis=0, keepdims=True))
        al_sc[g] = jnp.exp(m_old - m_new)
        p_sc[g] = jnp.exp(s - m_new).astype(BF16)
        m_sc[g] = m_new

    def pv_online(kt, g):
        vtg = vt_ref[kt, g * vrows:(g + 1) * vrows, :]
        acc_sc[g] = al_sc[g] * acc_sc[g] + jnp.dot(vtg, p_sc[g], preferred_element_type=F32)

    def attend(softmax, pv):
        m_sc[...] = jnp.full(m_sc.shape, NEG_BIG, F32)
        acc_sc[...] = jnp.zeros(acc_sc.shape, F32)
        p_sc[1] = jnp.zeros(p_sc.shape[1:], BF16)
        al_sc[1] = jnp.ones(al_sc.shape[1:], F32)

        def attn_tile(kt, carry):
            bias = jnp.where(key_sc[kt] >= thr, 0.0, NEG_BIG)
            bias = jnp.concatenate([bias] * KV_REP, axis=1)
            qk(kt, 1)
            softmax(0, bias)
            pv(jnp.maximum(kt - 1, 0), 1)
            qk(jnp.minimum(kt + 1, nkt - 1), 0)
            softmax(1, bias)
            pv(kt, 0)
            return carry

        qk(0, 0)
        lax.fori_loop(0, nkt, attn_tile, 0)
        pv(nkt - 1, 1)

    @pl.when(fast)
    def _():
        attend(softmax_shifted, pv_shifted)

    @pl.when(jnp.logical_not(fast))
    def _():
        attend(softmax_online, pv_online)

    for g in range(N_KV_HEADS):
        out = acc_sc[g, :HEAD_DIM, :] / acc_sc[g, HEAD_DIM:HEAD_DIM + 1, :]
        for r in range(KV_REP):
            hd = g * KV_REP + r
            o_ref[:, hd * HEAD_DIM:(hd + 1) * HEAD_DIM] = out[:, r * tq:(r + 1) * tq].T.astype(o_ref.dtype)


def _dsa_call(q, qi, wv, k, vt, ki, topk):
    bsz, seq, dq = q.shape
    tq = DSA_TQ
    nkt, dv, tk = vt.shape[1:]
    nq = seq // tq
    qmap = lambda b, i: (b, i, 0)

    def whole(shape):
        return pl.BlockSpec((None,) + tuple(shape), lambda b, i: (b,) + (0,) * len(shape),
                            pipeline_mode=pl.Buffered(1))

    return pl.pallas_call(
        functools.partial(_dsa_kernel, topk=topk),
        grid=(bsz, nq),
        in_specs=[pl.BlockSpec((None, tq, dq), qmap),
                  pl.BlockSpec((None, tq, qi.shape[2]), qmap),
                  pl.BlockSpec((None, tq, wv.shape[2]), qmap),
                  whole(k.shape[1:]), whole(vt.shape[1:]), whole(ki.shape[1:])],
        out_specs=pl.BlockSpec((None, tq, dq), qmap),
        out_shape=jax.ShapeDtypeStruct((bsz, seq, dq), BF16),
        scratch_shapes=[pltpu.VMEM((nkt, tk, tq), I32),
                        pltpu.VMEM((nkt, tk, tq), BF16),
                        pltpu.VMEM((N_IDX_HEADS, LANES, tq), BF16),
                        pltpu.VMEM((LANES, tq), F32),
                        pltpu.VMEM((N_KV_HEADS, 2 * HEAD_DIM, KV_REP * tq), BF16),
                        pltpu.VMEM((N_KV_HEADS, 1, KV_REP * tq), F32),
                        pltpu.VMEM((N_KV_HEADS, dv // N_KV_HEADS, KV_REP * tq), F32),
                        pltpu.VMEM((N_KV_HEADS, tk, KV_REP * tq), F32),
                        pltpu.VMEM((N_KV_HEADS, tk, KV_REP * tq), BF16),
                        pltpu.VMEM((N_KV_HEADS, 1, KV_REP * tq), F32),
                        pltpu.SMEM((N_KV_HEADS,), F32)],
        compiler_params=_cparams(("parallel", "arbitrary")),
        name="dsa",
    )(q, qi, wv, k, vt, ki)


def _outproj_kernel(x_ref, ys_ref, ya_ref, gt_ref, sc_ref, sh_ref, wglu_ref, bglu_ref,
                    gs_ref, ga_ref, wout_ref, gm_ref, x1_out, h2_out):
    def rms(t, g):
        return t * lax.rsqrt(jnp.mean(t * t, axis=-1, keepdims=True) + EPS) * g

    y = ys_ref[...]
    ya = 0.5 * y * (1.0 + jnp.tanh(math.sqrt(2.0 / math.pi) * (y + 0.044715 * (y * y * y))))
    z = jnp.dot(ya.astype(BF16), wglu_ref[...], preferred_element_type=F32) + bglu_ref[...]
    o = ya / (1.0 + jnp.exp(-z))
    n1 = rms(o, gs_ref[...]).astype(BF16)
    n2 = rms(ya_ref[...].astype(F32), ga_ref[...]).astype(BF16)
    d1 = n1.shape[1]
    mixw = (jnp.dot(n1, wout_ref[:d1, :], preferred_element_type=F32)
            + jnp.dot(n2, wout_ref[d1:, :], preferred_element_type=F32))
    x1 = x_ref[...] + gt_ref[...] * mixw
    x1_out[...] = x1
    h2 = rms(x1, gm_ref[...]) * (1.0 + sc_ref[...]) + sh_ref[...]
    h2_out[...] = h2.astype(BF16)


def _outproj_call(x2, ys, ya, gt, sc, sh, wglu, bglu, gs, ga, wout, gm, seq):
    n, d = x2.shape
    tm = OUT_TM
    per_b = seq // tm
    row = lambda i: (i, 0)
    modmap = lambda i: (i // per_b, 0, 0)
    return pl.pallas_call(
        _outproj_kernel,
        grid=(n // tm,),
        in_specs=[pl.BlockSpec((tm, d), row),
                  pl.BlockSpec((tm, ys.shape[1]), row),
                  pl.BlockSpec((tm, ya.shape[1]), row),
                  pl.BlockSpec((None, 1, d), modmap),
                  pl.BlockSpec((None, 1, d), modmap),
                  pl.BlockSpec((None, 1, d), modmap),
                  _const_spec(wglu.shape), _const_spec(bglu.shape),
                  _const_spec(gs.shape), _const_spec(ga.shape),
                  _const_spec(wout.shape), _const_spec(gm.shape)],
        out_specs=[pl.BlockSpec((tm, d), row), pl.BlockSpec((tm, d), row)],
        out_shape=[jax.ShapeDtypeStruct((n, d), F32), jax.ShapeDtypeStruct((n, d), BF16)],
        compiler_params=_cparams(("parallel",)),
        name="outproj",
    )(x2, ys, ya, gt, sc, sh, wglu, bglu, gs, ga, wout, gm)


def _mlp_kernel(h_ref, x1_ref, gt_ref, w1_ref, w2_ref, o_ref):
    j = pl.program_id(1)

    @pl.when(j == 0)
    def _():
        o_ref[...] = jnp.zeros(o_ref.shape, F32)

    a = jnp.maximum(jnp.dot(h_ref[...], w1_ref[...], preferred_element_type=F32), 0.0)
    o_ref[...] += jnp.dot((a * a).astype(BF16), w2_ref[...], preferred_element_type=F32)

    @pl.when(j == pl.num_programs(1) - 1)
    def _():
        o_ref[...] = x1_ref[...] + gt_ref[...] * o_ref[...]


def _mlp_call(h2, x1, gt, w1, w2, seq):
    n, d = x1.shape
    dff = w1.shape[1]
    tm, tf = MLP_TM, MLP_TF
    per_b = seq // tm
    return pl.pallas_call(
        _mlp_kernel,
        grid=(n // tm, dff // tf),
        in_specs=[pl.BlockSpec((tm, d), lambda i, j: (i, 0), pipeline_mode=pl.Buffered(1)),
                  pl.BlockSpec((tm, d), lambda i, j: (i, 0), pipeline_mode=pl.Buffered(1)),
                  pl.BlockSpec((None, 1, d), lambda i, j: (i // per_b, 0, 0)),
                  pl.BlockSpec((d, tf), lambda i, j: (0, j)),
                  pl.BlockSpec((tf, d), lambda i, j: (j, 0))],
        out_specs=pl.BlockSpec((tm, d), lambda i, j: (i, 0)),
        out_shape=jax.ShapeDtypeStruct((n, d), F32),
        compiler_params=_cparams(("parallel", "arbitrary")),
        name="mlp",
    )(h2, x1, gt, w1, w2)


def _rope_tables():
    def tables(head_dim, per_vreg):
        r = head_dim // ROPE_FRAC
        half = r // 2
        inv = ROPE_THETA ** (-jnp.arange(half, dtype=F32) / half)
        lane = np.arange(LANES) % head_dim
        inv_l = jnp.where(lane < r, inv[lane % half], 0.0).astype(F32)[None, :]
        lo = jnp.asarray((lane < half).astype(np.float32))[None, :]
        hi = jnp.asarray(((lane >= half) & (lane < r)).astype(np.float32))[None, :]
        return inv_l, lo, hi

    inv_qk, lo_qk, hi_qk = tables(HEAD_DIM, 1)
    inv_ix, lo_ix, hi_ix = tables(IDX_DIM, 2)
    return [inv_qk, inv_ix, lo_qk, hi_qk, lo_ix, hi_ix]


def _block_diag(m):
    nb, g, a, b = m.shape
    eye = jnp.eye(g, dtype=m.dtype)
    return jnp.einsum('ngab,gk->ngakb', m, eye).reshape(nb, g * a, g * b)


def kernel(x, c, positions, w_ada, b_ada, g_norm_mix, w_in, lam_re, lam_im, log_dt,
           b_re, b_im, c_re, c_im, d_skip, w_glu, b_glu, g_q, g_k, g_out_ssm, g_out_attn,
           w_out, g_norm_mlp, w_mlp_in, w_mlp_out):
    bsz, seq, d = x.shape
    depth = w_ada.shape[0]
    n = bsz * seq
    topk = min(TOPK_MAX, seq // 4)
    gg, pp, hh = N_SSM_GROUPS, SSM_STATE, SSM_GROUP
    nblk = gg // S5_GB
    chunk = S5_CHUNK
    nchunks = seq // chunk

    c_pad = jnp.zeros((8, d), F32).at[:bsz].set(c.astype(F32))
    pos2 = positions.reshape(n, 1).astype(I32)
    tabs = _rope_tables()
    xcur = x.reshape(n, d).astype(F32)

    for l in range(depth):
        mod = _ada_call(c_pad, w_ada[l], b_ada[l][None, :])[:bsz]
        mod = mod.reshape(bsz, N_MOD, 1, d)
        sh_a, sc_a, gt_a, sh_m, sc_m, gt_m = [mod[:, i] for i in range(N_MOD)]

        wi = w_in[l]
        o0 = D_SSM
        o1 = o0 + N_HEADS * HEAD_DIM
        o2 = o1 + 2 * N_KV_HEADS * HEAD_DIM
        o3 = o2 + N_IDX_HEADS * IDX_DIM
        o4 = o3 + IDX_DIM
        w_ki = wi[:, o3:o4]
        w_wi = wi[:, o4:]
        w_kw = jnp.concatenate(
            [w_ki, w_ki, w_wi, jnp.zeros((d, 2 * LANES - 2 * IDX_DIM - N_IDX_HEADS), wi.dtype)], axis=1)
        u, q, k, v, qi, ki, wv = _inproj_call(
            xcur, sc_a, sh_a, g_norm_mix[l][None, :], pos2, tabs,
            g_q[l][None, :], g_k[l][None, :],
            wi[:, :o0].astype(BF16), wi[:, o0:o1].astype(BF16), wi[:, o1:o2].astype(BF16),
            wi[:, o2:o3].astype(BF16), w_kw.astype(BF16), seq)

        lr = lam_re[l].astype(F32).reshape(1, gg * pp)
        li = lam_im[l].astype(F32).reshape(1, gg * pp)
        dt = jnp.broadcast_to(jnp.exp(log_dt[l].astype(F32))[:, None], (gg, pp)).reshape(1, gg * pp)
        bre_t = b_re[l].astype(F32).transpose(2, 0, 1).reshape(hh, gg * pp)
        bim_t = b_im[l].astype(F32).transpose(2, 0, 1).reshape(hh, gg * pp)
        pw_r, pw_i, bb_r, bb_i = _s5prep_call(lr, li, dt, bre_t, bim_t, chunk)
        bbr = bb_r.reshape(hh, nblk, S5_GB, pp).transpose(1, 2, 0, 3)
        bbi = bb_i.reshape(hh, nblk, S5_GB, pp).transpose(1, 2, 0, 3)
        bw = jnp.concatenate([_block_diag(bbr), _block_diag(bbi)], axis=2).astype(BF16)
        cr = c_re[l].astype(F32).reshape(nblk, S5_GB, hh, pp).transpose(0, 1, 3, 2)
        ci = c_im[l].astype(F32).reshape(nblk, S5_GB, hh, pp).transpose(0, 1, 3, 2)
        cw = jnp.concatenate([_block_diag(cr), -_block_diag(ci)], axis=1).astype(BF16)
        ns = S5_GB * pp
        pw = jnp.concatenate([pw_r.reshape(chunk, nblk, ns), pw_i.reshape(chunk, nblk, ns)],
                             axis=2).transpose(1, 0, 2)
        u_t = u.reshape(bsz, nchunks, chunk, D_SSM).transpose(0, 2, 1, 3)
        y_t = _s5_call(u_t, bw, cw, pw, d_skip[l].astype(F32)[None, :])
        y_ssm = y_t.transpose(0, 2, 1, 3).reshape(n, D_SSM)

        v_t = v.reshape(bsz, seq // DSA_TK, v.shape[1], DSA_TK)
        y_attn = _dsa_call(q.reshape(bsz, seq, -1), qi.reshape(bsz, seq, -1),
                           wv.reshape(bsz, seq, -1), k.reshape(bsz, seq, -1),
                           v_t, ki.reshape(bsz, seq, -1), topk)
        y_attn = y_attn.reshape(n, -1)

        x1, h2 = _outproj_call(
            xcur, y_ssm, y_attn, gt_a, sc_m, sh_m,
            w_glu[l].astype(BF16), b_glu[l].astype(F32)[None, :],
            g_out_ssm[l].astype(F32)[None, :], g_out_attn[l].astype(F32)[None, :],
            w_out[l].astype(BF16), g_norm_mlp[l].astype(F32)[None, :], seq)

        xcur = _mlp_call(h2, x1, gt_m, w_mlp_in[l].astype(BF16), w_mlp_out[l].astype(BF16), seq)

    return xcur.reshape(bsz, seq, d).astype(x.dtype)
```

```python
import functools
import math

import numpy as np
import jax
import jax.numpy as jnp
from jax import lax
from jax.experimental import pallas as pl
from jax.experimental.pallas import tpu as pltpu

F32 = jnp.float32
BF16 = jnp.bfloat16
I32 = jnp.int32

D_MODEL = 2048
D_SSM = 1024
SSM_GROUP = 16
N_SSM_GROUPS = 64
SSM_STATE = 64
HEAD_DIM = 128
N_HEADS = 8
N_KV_HEADS = 2
KV_REP = N_HEADS // N_KV_HEADS
N_IDX_HEADS = 16
IDX_DIM = 64
TOPK_MAX = 256
ROPE_FRAC = 4
ROPE_THETA = 500000.0
D_FF = 4 * D_MODEL
EPS = 1e-6
N_MOD = 6

LANES = 128
VMEM_LIMIT = 56 * 1024 * 1024

ADA_TN = 1024
INPROJ_TM = 512
S5_CHUNK = 32
S5_GB = 16
DSA_TQ = 256
DSA_TK = 256
DSA_ONES = 16
OUT_TM = 256
MLP_TM = 1024
MLP_TF = 512

NEG_BIG = -1e30
INT_MIN = -2147483648
KEY_NEG_INF = -2139095041
KEY16_NEG_INF = KEY_NEG_INF >> 16
KEY16_BAND_LO = -129
KEY16_BAND_HI = 127
SHIFT_MARGIN = 1.01
MAX_SPAN = 80.0


def _cparams(sem):
    return pltpu.CompilerParams(dimension_semantics=sem, vmem_limit_bytes=VMEM_LIMIT)


def _const_spec(shape):
    nd = len(shape)
    return pl.BlockSpec(shape, lambda *_: (0,) * nd, pipeline_mode=pl.Buffered(1))


def _ada_kernel(c_ref, w_ref, b_ref, o_ref):
    c = c_ref[...]
    act = c / (1.0 + jnp.exp(-c))
    o_ref[...] = jnp.dot(act.astype(BF16), w_ref[...].astype(BF16),
                         preferred_element_type=F32) + b_ref[...]


def _ada_call(c_pad, w_ada, b_ada):
    rows, d = c_pad.shape
    n = w_ada.shape[1]
    return pl.pallas_call(
        _ada_kernel,
        grid=(n // ADA_TN,),
        in_specs=[pl.BlockSpec((rows, d), lambda j: (0, 0)),
                  pl.BlockSpec((d, ADA_TN), lambda j: (0, j)),
                  pl.BlockSpec((1, ADA_TN), lambda j: (0, j))],
        out_specs=pl.BlockSpec((rows, ADA_TN), lambda j: (0, j)),
        out_shape=jax.ShapeDtypeStruct((rows, n), F32),
        compiler_params=_cparams(("arbitrary",)),
        name="ada",
    )(c_pad, w_ada, b_ada)


def _rope(x, cos, sin_lo, sin_hi, half):
    n = x.shape[-1]
    return (x * cos + pltpu.roll(x, n - half, 1) * sin_lo + pltpu.roll(x, half, 1) * sin_hi)


def _inproj_kernel(x_ref, sc_ref, sh_ref, g_ref, pos_ref, inv_qk_ref, inv_ix_ref,
                   mlo_qk_ref, mhi_qk_ref, mlo_ix_ref, mhi_ix_ref, gq_ref, gk_ref,
                   wu_ref, wq_ref, wkv_ref, wqi_ref, wkw_ref,
                   u_out, q_out, k_out, v_out, qi_out, ki_out, wv_out):
    x = x_ref[...]
    ms = jnp.mean(x * x, axis=-1, keepdims=True)
    h = x * lax.rsqrt(ms + EPS) * g_ref[...]
    h = h * (1.0 + sc_ref[...]) + sh_ref[...]
    hb = h.astype(BF16)

    u_out[...] = jnp.dot(hb, wu_ref[...], preferred_element_type=F32).astype(BF16)

    pos = pos_ref[...].astype(F32)
    ang = pos * inv_qk_ref[...]
    cos_qk = jnp.cos(ang)
    sin_qk = jnp.sin(ang)
    slo_qk = -sin_qk * mlo_qk_ref[...]
    shi_qk = sin_qk * mhi_qk_ref[...]
    ang = pos * inv_ix_ref[...]
    cos_ix = jnp.cos(ang)
    sin_ix = jnp.sin(ang)
    slo_ix = -sin_ix * mlo_ix_ref[...]
    shi_ix = sin_ix * mhi_ix_ref[...]
    half_qk = HEAD_DIM // ROPE_FRAC // 2
    half_ix = IDX_DIM // ROPE_FRAC // 2

    def head_norm(t, g):
        return t * lax.rsqrt(jnp.mean(t * t, axis=-1, keepdims=True) + EPS) * g

    q = jnp.dot(hb, wq_ref[...], preferred_element_type=F32)
    attn_scale = HEAD_DIM ** -0.5
    for hd in range(N_HEADS):
        sl = slice(hd * HEAD_DIM, (hd + 1) * HEAD_DIM)
        t = _rope(head_norm(q[:, sl], gq_ref[...]), cos_qk, slo_qk, shi_qk, half_qk)
        q_out[:, sl] = (t * attn_scale).astype(BF16)

    kv = jnp.dot(hb, wkv_ref[...], preferred_element_type=F32)
    nk = N_KV_HEADS * HEAD_DIM
    one_col = (lax.broadcasted_iota(I32, (x.shape[0], HEAD_DIM), 1) == 0).astype(BF16)
    for hd in range(N_KV_HEADS):
        sl = slice(hd * HEAD_DIM, (hd + 1) * HEAD_DIM)
        t = _rope(head_norm(kv[:, sl], gk_ref[...]), cos_qk, slo_qk, shi_qk, half_qk)
        k_out[:, 2 * hd * HEAD_DIM:(2 * hd + 1) * HEAD_DIM] = t.astype(BF16)
        k_out[:, (2 * hd + 1) * HEAD_DIM:(2 * hd + 2) * HEAD_DIM] = one_col
    vrows = HEAD_DIM + DSA_ONES
    ones = jnp.ones((DSA_ONES, DSA_TK), BF16)
    for t in range(v_out.shape[0]):
        for hd in range(N_KV_HEADS):
            vh = kv[t * DSA_TK:(t + 1) * DSA_TK, nk + hd * HEAD_DIM:nk + (hd + 1) * HEAD_DIM]
            v_out[t, hd * vrows:hd * vrows + HEAD_DIM, :] = vh.T.astype(BF16)
            v_out[t, hd * vrows + HEAD_DIM:(hd + 1) * vrows, :] = ones

    qi = jnp.dot(hb, wqi_ref[...], preferred_element_type=F32)
    idx_scale = IDX_DIM ** -0.5
    for pr in range(N_IDX_HEADS * IDX_DIM // LANES):
        sl = slice(pr * LANES, (pr + 1) * LANES)
        t = _rope(qi[:, sl], cos_ix, slo_ix, shi_ix, half_ix)
        qi_out[:, sl] = (t * idx_scale).astype(BF16)

    kw = jnp.dot(hb, wkw_ref[...], preferred_element_type=F32)
    ki_out[...] = _rope(kw[:, :LANES], cos_ix, slo_ix, shi_ix, half_ix).astype(BF16)
    wv_out[...] = kw[:, LANES:] * (N_IDX_HEADS ** -0.5)


def _inproj_call(x2, sc, sh, g, pos2, tabs, gq, gk, wu, wq, wkv, wqi, wkw, seq):
    n, d = x2.shape
    tm = INPROJ_TM
    per_b = seq // tm
    row = lambda i: (i, 0)
    modmap = lambda i: (i // per_b, 0, 0)
    small = [_const_spec(t.shape) for t in tabs]
    in_specs = ([pl.BlockSpec((tm, d), row),
                 pl.BlockSpec((None, 1, d), modmap),
                 pl.BlockSpec((None, 1, d), modmap),
                 _const_spec(g.shape),
                 pl.BlockSpec((tm, 1), row)]
                + small
                + [_const_spec(gq.shape), _const_spec(gk.shape),
                   _const_spec(wu.shape), _const_spec(wq.shape), _const_spec(wkv.shape),
                   _const_spec(wqi.shape), _const_spec(wkw.shape)])
    widths = (D_SSM, N_HEADS * HEAD_DIM, 2 * N_KV_HEADS * HEAD_DIM, N_KV_HEADS * HEAD_DIM,
              N_IDX_HEADS * IDX_DIM, LANES, LANES)
    dtypes = (BF16, BF16, BF16, BF16, BF16, BF16, F32)
    vt_rows = N_KV_HEADS * (HEAD_DIM + DSA_ONES)
    out_specs = [pl.BlockSpec((tm, w), row) for w in widths]
    out_shape = [jax.ShapeDtypeStruct((n, w), dt) for w, dt in zip(widths, dtypes)]
    out_specs[3] = pl.BlockSpec((tm // DSA_TK, vt_rows, DSA_TK), lambda i: (i, 0, 0))
    out_shape[3] = jax.ShapeDtypeStruct((n // DSA_TK, vt_rows, DSA_TK), BF16)
    return pl.pallas_call(
        _inproj_kernel,
        grid=(n // tm,),
        in_specs=in_specs,
        out_specs=out_specs,
        out_shape=out_shape,
        compiler_params=_cparams(("parallel",)),
        name="inproj",
    )(x2, sc, sh, g, pos2, *tabs, gq, gk, wu, wq, wkv, wqi, wkw)


def _s5prep_kernel(lr_ref, li_ref, dt_ref, bre_ref, bim_ref, pr_ref, pi_ref, bbr_ref, bbi_ref):
    lr = jnp.minimum(lr_ref[...], -1e-4)
    li = li_ref[...]
    dt = dt_ref[...]
    mag = jnp.exp(lr * dt)
    ab_r = mag * jnp.cos(li * dt)
    ab_i = mag * jnp.sin(li * dt)
    den = lr * lr + li * li
    fr = ((ab_r - 1.0) * lr + ab_i * li) / den
    fi = (ab_i * lr - (ab_r - 1.0) * li) / den
    bbr_ref[...] = fr * bre_ref[...] - fi * bim_ref[...]
    bbi_ref[...] = fr * bim_ref[...] + fi * bre_ref[...]
    steps = (lax.broadcasted_iota(I32, pr_ref.shape, 0) + 1).astype(F32)
    magn = jnp.exp(steps * (lr * dt))
    pr_ref[...] = magn * jnp.cos(steps * (li * dt))
    pi_ref[...] = magn * jnp.sin(steps * (li * dt))


def _s5prep_call(lr, li, dt, bre_t, bim_t, chunk):
    gp = lr.shape[1]
    hh = bre_t.shape[0]
    outs = [jax.ShapeDtypeStruct((chunk, gp), F32)] * 2 + [jax.ShapeDtypeStruct((hh, gp), F32)] * 2
    return pl.pallas_call(_s5prep_kernel, out_shape=outs, name="s5prep")(lr, li, dt, bre_t, bim_t)


def _s5_kernel(u_ref, bw_ref, cw_ref, pw_ref, dsk_ref, y_ref, st_ref, cin_ref, xs_ref):
    tt, rr, _ = u_ref.shape
    ns = st_ref.shape[1] // 2
    ar = pw_ref[0:1, :ns]
    ai = pw_ref[0:1, ns:]
    st_ref[...] = jnp.zeros(st_ref.shape, F32)

    def cproj(zr, zi):
        return (jnp.dot(zr.astype(BF16), cw_ref[:ns, :], preferred_element_type=F32)
                + jnp.dot(zi.astype(BF16), cw_ref[ns:, :], preferred_element_type=F32))

    xs_ref[0] = jnp.dot(u_ref[0], bw_ref[...], preferred_element_type=F32)

    def local_step(i, carry):
        x = xs_ref[i % 2]
        nxt = jnp.minimum(i + 1, tt - 1)
        xs_ref[(i + 1) % 2] = jnp.dot(u_ref[nxt], bw_ref[...], preferred_element_type=F32)
        sr = st_ref[:, :ns]
        si = st_ref[:, ns:]
        nr = ar * sr - ai * si + x[:, :ns]
        ni = ar * si + ai * sr + x[:, ns:]
        st_ref[:, :ns] = nr
        st_ref[:, ns:] = ni
        y_ref[i] = cproj(nr, ni) + dsk_ref[...] * u_ref[i].astype(F32)
        return carry

    lax.fori_loop(0, tt, local_step, 0)

    atr = pw_ref[tt - 1:tt, :ns]
    ati = pw_ref[tt - 1:tt, ns:]

    def chain(r, carry):
        cr, ci = carry
        cin_ref[pl.ds(r, 1), :ns] = cr
        cin_ref[pl.ds(r, 1), ns:] = ci
        er = st_ref[pl.ds(r, 1), :ns]
        ei = st_ref[pl.ds(r, 1), ns:]
        return (atr * cr - ati * ci + er, atr * ci + ati * cr + ei)

    zero = jnp.zeros((1, ns), F32)
    lax.fori_loop(0, rr, chain, (zero, zero))

    def carry_step(i, carry):
        pr = pw_ref[pl.ds(i, 1), :ns]
        pi = pw_ref[pl.ds(i, 1), ns:]
        cr = cin_ref[:, :ns]
        ci = cin_ref[:, ns:]
        y_ref[i] += cproj(pr * cr - pi * ci, pr * ci + pi * cr)
        return carry

    lax.fori_loop(0, tt, carry_step, 0)


def _s5_call(u_t, bw, cw, pw, dsk):
    bsz, tt, rr, dch = u_t.shape
    nb = bw.shape[0]
    cb = bw.shape[1]
    ns2 = bw.shape[2]
    return pl.pallas_call(
        _s5_kernel,
        grid=(bsz, nb),
        in_specs=[pl.BlockSpec((None, tt, rr, cb), lambda b, j: (b, 0, 0, j)),
                  pl.BlockSpec((None, cb, ns2), lambda b, j: (j, 0, 0)),
                  pl.BlockSpec((None, ns2, cb), lambda b, j: (j, 0, 0)),
                  pl.BlockSpec((None, tt, ns2), lambda b, j: (j, 0, 0)),
                  pl.BlockSpec((1, cb), lambda b, j: (0, j))],
        out_specs=pl.BlockSpec((None, tt, rr, cb), lambda b, j: (b, 0, 0, j)),
        out_shape=jax.ShapeDtypeStruct((bsz, tt, rr, dch), F32),
        scratch_shapes=[pltpu.VMEM((rr, ns2), F32), pltpu.VMEM((rr, ns2), F32),
                        pltpu.VMEM((2, rr, ns2), F32)],
        compiler_params=_cparams(("parallel", "parallel")),
        name="s5scan",
    )(u_t, bw, cw, pw, dsk)


def _order_key(x):
    bits = pltpu.bitcast(x, I32)
    return jnp.where(bits < 0, bits ^ jnp.int32(0x7FFFFFFF), bits)


def _dsa_kernel(q_ref, qi_ref, wv_ref, k_ref, vt_ref, ki_ref, o_ref,
                key_sc, kb_sc, lhs_sc, wt_sc, qs_sc, m_sc, acc_sc, s_sc, p_sc, al_sc, kn_sm,
                *, topk):
    tq = q_ref.shape[0]
    tk = key_sc.shape[1]
    qb = pl.program_id(1)
    nkt = (qb * tq + tq + tk - 1) // tk

    lane = lax.broadcasted_iota(I32, (tq, LANES), 1)
    kiota = lax.broadcasted_iota(I32, (tk, tq), 0)
    qpos = qb * tq + lax.broadcasted_iota(I32, (tk, tq), 1)

    @pl.when(qb == 0)
    def _():
        rows = min(k_ref.shape[0], 1024)
        for g in range(N_KV_HEADS):
            def norm_max(c, best):
                r0 = pl.multiple_of(c * rows, rows)
                kk = k_ref[pl.ds(r0, rows), g * 2 * HEAD_DIM:g * 2 * HEAD_DIM + HEAD_DIM].astype(F32)
                return jnp.maximum(best, jnp.max(jnp.sum(kk * kk, axis=1, keepdims=True)))

            kn_sm[g] = lax.fori_loop(0, k_ref.shape[0] // rows, norm_max, jnp.float32(0.0))

    for hd in range(N_IDX_HEADS):
        pair = qi_ref[:, (hd // 2) * LANES:(hd // 2 + 1) * LANES].astype(F32)
        keep = (lane < IDX_DIM) if hd % 2 == 0 else (lane >= IDX_DIM)
        lhs_sc[hd] = jnp.where(keep, pair, 0.0).T.astype(BF16)
    wt_sc[...] = wv_ref[...].T

    def score_tiles(kt, ntile):
        k0 = pl.multiple_of(kt * tk, tk)
        kit = ki_ref[pl.ds(k0, ntile * tk), :]
        acc = jnp.zeros((ntile * tk, tq), F32)
        for hd in range(N_IDX_HEADS):
            lg = jnp.dot(kit, lhs_sc[hd], preferred_element_type=F32)
            acc = acc + jnp.maximum(lg, 0.0) * wt_sc[hd:hd + 1, :]
        for j in range(ntile):
            a = jnp.where(kiota + (k0 + j * tk) <= qpos, acc[j * tk:(j + 1) * tk], -jnp.inf)
            bits = pltpu.bitcast(a, I32)
            key_sc[kt + j] = jnp.where(bits < 0, bits ^ jnp.int32(0x7FFFFFFF), bits)
            kb_sc[kt + j] = pltpu.bitcast(bits & jnp.int32(-65536), F32).astype(BF16)

    def score_quad(kq, carry):
        score_tiles(4 * kq, 4)
        return carry

    lax.fori_loop(0, nkt // 4, score_quad, 0)

    @pl.when(nkt % 4 >= 2)
    def _():
        score_tiles((nkt // 4) * 4, 2)

    @pl.when(nkt % 2 == 1)
    def _():
        score_tiles(nkt - 1, 1)

    def search(count, bit0, t0, n0):
        def cond(st):
            bit, _, cnt = st
            return jnp.logical_and(bit >= 0, jnp.max(jnp.abs(cnt - topk)) > 0)

        def body(st):
            bit, t, cnt = st
            for _ in range(2):
                cand = t + jnp.left_shift(jnp.int32(1), jnp.maximum(bit, 0))
                c = count(cand)
                ok = jnp.logical_and(c >= topk, bit >= 0)
                t = jnp.where(ok, cand, t)
                cnt = jnp.where(ok, c, cnt)
                bit = bit - 1
            return bit, t, cnt

        _, t, cnt = lax.while_loop(cond, body, (jnp.asarray(bit0, I32), t0, n0))
        return t, cnt

    one_b = jnp.ones((tk, tq), BF16)
    zero_b = jnp.zeros((tk, tq), BF16)
    sub = tk // 16

    def count16(c16):
        c16 = jnp.maximum(c16, KEY16_NEG_INF)
        pat = jnp.where(c16 >= 0, c16, c16 ^ jnp.int32(0x7FFF))
        cand = pltpu.bitcast(jnp.left_shift(pat, 16), F32).astype(BF16)

        def hits(kt):
            hit = jnp.where(kb_sc[kt] >= cand, one_b, zero_b)
            part = hit[0:16]
            for j in range(1, sub):
                part = part + hit[j * 16:(j + 1) * 16]
            return part

        def quad(kq, acc):
            part = hits(4 * kq)
            for j in range(1, 4):
                part = part + hits(4 * kq + j)
            return acc + part.astype(F32)

        def single(kt, acc):
            return acc + hits(kt).astype(F32)

        acc = lax.fori_loop(0, nkt // 4, quad, jnp.zeros((16, tq), F32))
        acc = lax.fori_loop((nkt // 4) * 4, nkt, single, acc)
        return jnp.sum(acc, axis=0, keepdims=True).astype(I32)

    def count32(cand):
        def single(kt, acc):
            hit = jnp.where(key_sc[kt] >= cand, 1, 0).astype(I32)
            return acc + jnp.sum(hit.reshape(tk // 8, 8, tq), axis=0)

        def quad(kq, acc):
            for j in range(4):
                acc = single(4 * kq + j, acc)
            return acc

        acc = lax.fori_loop(0, nkt // 4, quad, jnp.zeros((8, tq), I32))
        acc = lax.fori_loop((nkt // 4) * 4, nkt, single, acc)
        return jnp.sum(acc, axis=0, keepdims=True)

    c0 = count16(jnp.zeros((1, tq), I32))
    pos_ok = c0 >= topk
    t16, c16 = search(count16, 14,
                      jnp.where(pos_ok, 0, -32768).astype(I32),
                      jnp.where(pos_ok, c0, nkt * tk).astype(I32))
    band = jnp.logical_and(t16 >= KEY16_BAND_LO, t16 <= KEY16_BAND_HI)
    t32 = jnp.left_shift(jnp.where(band, KEY16_BAND_LO, t16), 16)
    c32 = jnp.where(band, nkt * tk + 1, c16).astype(I32)
    any_band = jnp.max(band.astype(I32)) > 0
    thr, cnt = search(count32, jnp.where(any_band, 24, 15), t32, c32)
    thr = jnp.maximum(thr, KEY_NEG_INF + 1)

    @pl.when(jnp.max(cnt) > topk)
    def _():
        need = topk - count32(thr + 1)
        tri = (lax.broadcasted_iota(I32, (tk, tk), 1)
               <= lax.broadcasted_iota(I32, (tk, tk), 0)).astype(BF16)

        def demote(kt, seen):
            keys = key_sc[kt]
            eq = keys == thr
            rank = seen + jnp.dot(tri, jnp.where(eq, 1.0, 0.0).astype(BF16),
                                  preferred_element_type=F32)
            key_sc[kt] = jnp.where(jnp.logical_and(eq, rank > need.astype(F32)), thr - 1, keys)
            return rank[tk - 1:tk, :]

        lax.fori_loop(0, nkt, demote, jnp.zeros((1, tq), F32))

    kv_w = 2 * HEAD_DIM
    shifts = []
    for g in range(N_KV_HEADS):
        parts = []
        for r in range(KV_REP):
            hd = g * KV_REP + r
            qht = q_ref[:, hd * HEAD_DIM:(hd + 1) * HEAD_DIM].astype(F32).T
            qs_sc[g, :HEAD_DIM, r * tq:(r + 1) * tq] = qht.astype(BF16)
            qn2 = jnp.sum(qht * qht, axis=0, keepdims=True)
            parts.append(jnp.sqrt(qn2 * kn_sm[g]) * SHIFT_MARGIN)
        shifts.append(jnp.concatenate(parts, axis=1))
    bound = jnp.maximum(jnp.max(shifts[0]), jnp.max(shifts[1]))
    fast = 2.0 * bound <= MAX_SPAN
    row0 = lax.broadcasted_iota(I32, (HEAD_DIM, KV_REP * tq), 0) == 0
    for g in range(N_KV_HEADS):
        shift_row = jnp.where(fast, -shifts[g], 0.0)
        qs_sc[g, HEAD_DIM:, :] = jnp.where(row0, shift_row, 0.0).astype(BF16)

    vrows = acc_sc.shape[1]

    def qk(kt, g):
        k0 = pl.multiple_of(kt * tk, tk)
        kg = k_ref[pl.ds(k0, tk), g * kv_w:(g + 1) * kv_w]
        s_sc[g] = jnp.dot(kg, qs_sc[g], preferred_element_type=F32)

    def softmax_shifted(g, bias):
        p_sc[g] = jnp.exp(s_sc[g] + bias).astype(BF16)

    def pv_shifted(kt, g):
        vtg = vt_ref[kt, g * vrows:(g + 1) * vrows, :]
        acc_sc[g] += jnp.dot(vtg, p_sc[g], preferred_element_type=F32)

    def softmax_online(g, bias):
        s = s_sc[g] + bias
        m_old = m_sc[g]
        m_new = jnp.maximum(m_old, jnp.max(s, axis=0, keepdims=True))
        al_sc[g] = jnp.exp(m_old - m_new)
        p_sc[g] = jnp.exp(s - m_new).astype(BF16)
        m_sc[g] = m_new

    def pv_online(kt, g):
        vtg = vt_ref[kt, g * vrows:(g + 1) * vrows, :]
        acc_sc[g] = al_sc[g] * acc_sc[g] + jnp.dot(vtg, p_sc[g], preferred_element_type=F32)

    def attend(softmax, pv):
        m_sc[...] = jnp.full(m_sc.shape, NEG_BIG, F32)
        acc_sc[...] = jnp.zeros(acc_sc.shape, F32)
        p_sc[1] = jnp.zeros(p_sc.shape[1:], BF16)
        al_sc[1] = jnp.ones(al_sc.shape[1:], F32)

        def attn_tile(kt, carry):
            bias = jnp.where(key_sc[kt] >= thr, 0.0, NEG_BIG)
            bias = jnp.concatenate([bias] * KV_REP, axis=1)
            qk(kt, 1)
            softmax(0, bias)
            pv(jnp.maximum(kt - 1, 0), 1)
            qk(jnp.minimum(kt + 1, nkt - 1), 0)
            softmax(1, bias)
            pv(kt, 0)
            return carry

        qk(0, 0)
        lax.fori_loop(0, nkt, attn_tile, 0)
        pv(nkt - 1, 1)

    @pl.when(fast)
    def _():
        attend(softmax_shifted, pv_shifted)

    @pl.when(jnp.logical_not(fast))
    def _():
        attend(softmax_online, pv_online)

    for g in range(N_KV_HEADS):
        out = acc_sc[g, :HEAD_DIM, :] / acc_sc[g, HEAD_DIM:HEAD_DIM + 1, :]
        for r in range(KV_REP):
            hd = g * KV_REP + r
            o_ref[:, hd * HEAD_DIM:(hd + 1) * HEAD_DIM] = out[:, r * tq:(r + 1) * tq].T.astype(o_ref.dtype)


def _dsa_call(q, qi, wv, k, vt, ki, topk):
    bsz, seq, dq = q.shape
    tq = DSA_TQ
    nkt, dv, tk = vt.shape[1:]
    nq = seq // tq
    qmap = lambda b, i: (b, i, 0)

    def whole(shape):
        return pl.BlockSpec((None,) + tuple(shape), lambda b, i: (b,) + (0,) * len(shape),
                            pipeline_mode=pl.Buffered(1))

    return pl.pallas_call(
        functools.partial(_dsa_kernel, topk=topk),
        grid=(bsz, nq),
        in_specs=[pl.BlockSpec((None, tq, dq), qmap),
                  pl.BlockSpec((None, tq, qi.shape[2]), qmap),
                  pl.BlockSpec((None, tq, wv.shape[2]), qmap),
                  whole(k.shape[1:]), whole(vt.shape[1:]), whole(ki.shape[1:])],
        out_specs=pl.BlockSpec((None, tq, dq), qmap),
        out_shape=jax.ShapeDtypeStruct((bsz, seq, dq), BF16),
        scratch_shapes=[pltpu.VMEM((nkt, tk, tq), I32),
                        pltpu.VMEM((nkt, tk, tq), BF16),
                        pltpu.VMEM((N_IDX_HEADS, LANES, tq), BF16),
                        pltpu.VMEM((LANES, tq), F32),
                        pltpu.VMEM((N_KV_HEADS, 2 * HEAD_DIM, KV_REP * tq), BF16),
                        pltpu.VMEM((N_KV_HEADS, 1, KV_REP * tq), F32),
                        pltpu.VMEM((N_KV_HEADS, dv // N_KV_HEADS, KV_REP * tq), F32),
                        pltpu.VMEM((N_KV_HEADS, tk, KV_REP * tq), F32),
                        pltpu.VMEM((N_KV_HEADS, tk, KV_REP * tq), BF16),
                        pltpu.VMEM((N_KV_HEADS, 1, KV_REP * tq), F32),
                        pltpu.SMEM((N_KV_HEADS,), F32)],
        compiler_params=_cparams(("parallel", "arbitrary")),
        name="dsa",
    )(q, qi, wv, k, vt, ki)


def _outproj_kernel(x_ref, ys_ref, ya_ref, gt_ref, sc_ref, sh_ref, wglu_ref, bglu_ref,
                    gs_ref, ga_ref, wout_ref, gm_ref, x1_out, h2_out):
    def rms(t, g):
        return t * lax.rsqrt(jnp.mean(t * t, axis=-1, keepdims=True) + EPS) * g

    y = ys_ref[...]
    ya = 0.5 * y * (1.0 + jnp.tanh(math.sqrt(2.0 / math.pi) * (y + 0.044715 * (y * y * y))))
    z = jnp.dot(ya.astype(BF16), wglu_ref[...], preferred_element_type=F32) + bglu_ref[...]
    o = ya / (1.0 + jnp.exp(-z))
    n1 = rms(o, gs_ref[...]).astype(BF16)
    n2 = rms(ya_ref[...].astype(F32), ga_ref[...]).astype(BF16)
    d1 = n1.shape[1]
    mixw = (jnp.dot(n1, wout_ref[:d1, :], preferred_element_type=F32)
            + jnp.dot(n2, wout_ref[d1:, :], preferred_element_type=F32))
    x1 = x_ref[...] + gt_ref[...] * mixw
    x1_out[...] = x1
    h2 = rms(x1, gm_ref[...]) * (1.0 + sc_ref[...]) + sh_ref[...]
    h2_out[...] = h2.astype(BF16)


def _outproj_call(x2, ys, ya, gt, sc, sh, wglu, bglu, gs, ga, wout, gm, seq):
    n, d = x2.shape
    tm = OUT_TM
    per_b = seq // tm
    row = lambda i: (i, 0)
    modmap = lambda i: (i // per_b, 0, 0)
    return pl.pallas_call(
        _outproj_kernel,
        grid=(n // tm,),
        in_specs=[pl.BlockSpec((tm, d), row),
                  pl.BlockSpec((tm, ys.shape[1]), row),
                  pl.BlockSpec((tm, ya.shape[1]), row),
                  pl.BlockSpec((None, 1, d), modmap),
                  pl.BlockSpec((None, 1, d), modmap),
                  pl.BlockSpec((None, 1, d), modmap),
                  _const_spec(wglu.shape), _const_spec(bglu.shape),
                  _const_spec(gs.shape), _const_spec(ga.shape),
                  _const_spec(wout.shape), _const_spec(gm.shape)],
        out_specs=[pl.BlockSpec((tm, d), row), pl.BlockSpec((tm, d), row)],
        out_shape=[jax.ShapeDtypeStruct((n, d), F32), jax.ShapeDtypeStruct((n, d), BF16)],
        compiler_params=_cparams(("parallel",)),
        name="outproj",
    )(x2, ys, ya, gt, sc, sh, wglu, bglu, gs, ga, wout, gm)


def _mlp_kernel(h_ref, x1_ref, gt_ref, w1_ref, w2_ref, o_ref):
    j = pl.program_id(1)

    @pl.when(j == 0)
    def _():
        o_ref[...] = jnp.zeros(o_ref.shape, F32)

    a = jnp.maximum(jnp.dot(h_ref[...], w1_ref[...], preferred_element_type=F32), 0.0)
    o_ref[...] += jnp.dot((a * a).astype(BF16), w2_ref[...], preferred_element_type=F32)

    @pl.when(j == pl.num_programs(1) - 1)
    def _():
        o_ref[...] = x1_ref[...] + gt_ref[...] * o_ref[...]


def _mlp_call(h2, x1, gt, w1, w2, seq):
    n, d = x1.shape
    dff = w1.shape[1]
    tm, tf = MLP_TM, MLP_TF
    per_b = seq // tm
    return pl.pallas_call(
        _mlp_kernel,
        grid=(n // tm, dff // tf),
        in_specs=[pl.BlockSpec((tm, d), lambda i, j: (i, 0), pipeline_mode=pl.Buffered(1)),
                  pl.BlockSpec((tm, d), lambda i, j: (i, 0), pipeline_mode=pl.Buffered(1)),
                  pl.BlockSpec((None, 1, d), lambda i, j: (i // per_b, 0, 0)),
                  pl.BlockSpec((d, tf), lambda i, j: (0, j)),
                  pl.BlockSpec((tf, d), lambda i, j: (j, 0))],
        out_specs=pl.BlockSpec((tm, d), lambda i, j: (i, 0)),
        out_shape=jax.ShapeDtypeStruct((n, d), F32),
        compiler_params=_cparams(("parallel", "arbitrary")),
        name="mlp",
    )(h2, x1, gt, w1, w2)


def _rope_tables():
    def tables(head_dim, per_vreg):
        r = head_dim // ROPE_FRAC
        half = r // 2
        inv = ROPE_THETA ** (-jnp.arange(half, dtype=F32) / half)
        lane = np.arange(LANES) % head_dim
        inv_l = jnp.where(lane < r, inv[lane % half], 0.0).astype(F32)[None, :]
        lo = jnp.asarray((lane < half).astype(np.float32))[None, :]
        hi = jnp.asarray(((lane >= half) & (lane < r)).astype(np.float32))[None, :]
        return inv_l, lo, hi

    inv_qk, lo_qk, hi_qk = tables(HEAD_DIM, 1)
    inv_ix, lo_ix, hi_ix = tables(IDX_DIM, 2)
    return [inv_qk, inv_ix, lo_qk, hi_qk, lo_ix, hi_ix]


def _block_diag(m):
    nb, g, a, b = m.shape
    eye = jnp.eye(g, dtype=m.dtype)
    return jnp.einsum('ngab,gk->ngakb', m, eye).reshape(nb, g * a, g * b)


def kernel(x, c, positions, w_ada, b_ada, g_norm_mix, w_in, lam_re, lam_im, log_dt,
           b_re, b_im, c_re, c_im, d_skip, w_glu, b_glu, g_q, g_k, g_out_ssm, g_out_attn,
           w_out, g_norm_mlp, w_mlp_in, w_mlp_out):
    bsz, seq, d = x.shape
    depth = w_ada.shape[0]
    n = bsz * seq
    topk = min(TOPK_MAX, seq // 4)
    gg, pp, hh = N_SSM_GROUPS, SSM_STATE, SSM_GROUP
    nblk = gg // S5_GB
    chunk = S5_CHUNK
    nchunks = seq // chunk

    c_pad = jnp.zeros((8, d), F32).at[:bsz].set(c.astype(F32))
    pos2 = positions.reshape(n, 1).astype(I32)
    tabs = _rope_tables()
    xcur = x.reshape(n, d).astype(F32)

    for l in range(depth):
        mod = _ada_call(c_pad, w_ada[l], b_ada[l][None, :])[:bsz]
        mod = mod.reshape(bsz, N_MOD, 1, d)
        sh_a, sc_a, gt_a, sh_m, sc_m, gt_m = [mod[:, i] for i in range(N_MOD)]

        wi = w_in[l]
        o0 = D_SSM
        o1 = o0 + N_HEADS * HEAD_DIM
        o2 = o1 + 2 * N_KV_HEADS * HEAD_DIM
        o3 = o2 + N_IDX_HEADS * IDX_DIM
        o4 = o3 + IDX_DIM
        w_ki = wi[:, o3:o4]
        w_wi = wi[:, o4:]
        w_kw = jnp.concatenate(
            [w_ki, w_ki, w_wi, jnp.zeros((d, 2 * LANES - 2 * IDX_DIM - N_IDX_HEADS), wi.dtype)], axis=1)
        u, q, k, v, qi, ki, wv = _inproj_call(
            xcur, sc_a, sh_a, g_norm_mix[l][None, :], pos2, tabs,
            g_q[l][None, :], g_k[l][None, :],
            wi[:, :o0].astype(BF16), wi[:, o0:o1].astype(BF16), wi[:, o1:o2].astype(BF16),
            wi[:, o2:o3].astype(BF16), w_kw.astype(BF16), seq)

        lr = lam_re[l].astype(F32).reshape(1, gg * pp)
        li = lam_im[l].astype(F32).reshape(1, gg * pp)
        dt = jnp.broadcast_to(jnp.exp(log_dt[l].astype(F32))[:, None], (gg, pp)).reshape(1, gg * pp)
        bre_t = b_re[l].astype(F32).transpose(2, 0, 1).reshape(hh, gg * pp)
        bim_t = b_im[l].astype(F32).transpose(2, 0, 1).reshape(hh, gg * pp)
        pw_r, pw_i, bb_r, bb_i = _s5prep_call(lr, li, dt, bre_t, bim_t, chunk)
        bbr = bb_r.reshape(hh, nblk, S5_GB, pp).transpose(1, 2, 0, 3)
        bbi = bb_i.reshape(hh, nblk, S5_GB, pp).transpose(1, 2, 0, 3)
        bw = jnp.concatenate([_block_diag(bbr), _block_diag(bbi)], axis=2).astype(BF16)
        cr = c_re[l].astype(F32).reshape(nblk, S5_GB, hh, pp).transpose(0, 1, 3, 2)
        ci = c_im[l].astype(F32).reshape(nblk, S5_GB, hh, pp).transpose(0, 1, 3, 2)
        cw = jnp.concatenate([_block_diag(cr), -_block_diag(ci)], axis=1).astype(BF16)
        ns = S5_GB * pp
        pw = jnp.concatenate([pw_r.reshape(chunk, nblk, ns), pw_i.reshape(chunk, nblk, ns)],
                             axis=2).transpose(1, 0, 2)
        u_t = u.reshape(bsz, nchunks, chunk, D_SSM).transpose(0, 2, 1, 3)
        y_t = _s5_call(u_t, bw, cw, pw, d_skip[l].astype(F32)[None, :])
        y_ssm = y_t.transpose(0, 2, 1, 3).reshape(n, D_SSM)

        v_t = v.reshape(bsz, seq // DSA_TK, v.shape[1], DSA_TK)
        y_attn = _dsa_call(q.reshape(bsz, seq, -1), qi.reshape(bsz, seq, -1),
                           wv.reshape(bsz, seq, -1), k.reshape(bsz, seq, -1),
                           v_t, ki.reshape(bsz, seq, -1), topk)
        y_attn = y_attn.reshape(n, -1)

        x1, h2 = _outproj_call(
            xcur, y_ssm, y_attn, gt_a, sc_m, sh_m,
            w_glu[l].astype(BF16), b_glu[l].astype(F32)[None, :],
            g_out_ssm[l].astype(F32)[None, :], g_out_attn[l].astype(F32)[None, :],
            w_out[l].astype(BF16), g_norm_mlp[l].astype(F32)[None, :], seq)

        xcur = _mlp_call(h2, x1, gt_m, w_mlp_in[l].astype(BF16), w_mlp_out[l].astype(BF16), seq)

    return xcur.reshape(bsz, seq, d).astype(x.dtype)
```

```python
import functools
import math

import numpy as np
import jax
import jax.numpy as jnp
from jax import lax
from jax.experimental import pallas as pl
from jax.experimental.pallas import tpu as pltpu

F32 = jnp.float32
BF16 = jnp.bfloat16
I32 = jnp.int32

D_MODEL = 2048
D_SSM = 1024
SSM_GROUP = 16
N_SSM_GROUPS = 64
SSM_STATE = 64
HEAD_DIM = 128
N_HEADS = 8
N_KV_HEADS = 2
KV_REP = N_HEADS // N_KV_HEADS
N_IDX_HEADS = 16
IDX_DIM = 64
TOPK_MAX = 256
ROPE_FRAC = 4
ROPE_THETA = 500000.0
D_FF = 4 * D_MODEL
EPS = 1e-6
N_MOD = 6

LANES = 128
VMEM_LIMIT = 56 * 1024 * 1024

ADA_TN = 1024
INPROJ_TM = 512
S5_CHUNK = 32
S5_GB = 16
DSA_TQ = 256
DSA_TK = 256
DSA_ONES = 16
OUT_TM = 256
MLP_TM = 1024
MLP_TF = 512

NEG_BIG = -1e30
INT_MIN = -2147483648
KEY_NEG_INF = -2139095041
KEY16_NEG_INF = KEY_NEG_INF >> 16
KEY16_BAND_LO = -129
KEY16_BAND_HI = 127
SHIFT_MARGIN = 1.01
MAX_SPAN = 80.0


def _cparams(sem):
    return pltpu.CompilerParams(dimension_semantics=sem, vmem_limit_bytes=VMEM_LIMIT)


def _const_spec(shape):
    nd = len(shape)
    return pl.BlockSpec(shape, lambda *_: (0,) * nd, pipeline_mode=pl.Buffered(1))


def _ada_kernel(c_ref, w_ref, b_ref, o_ref):
    c = c_ref[...]
    act = c / (1.0 + jnp.exp(-c))
    o_ref[...] = jnp.dot(act.astype(BF16), w_ref[...].astype(BF16),
                         preferred_element_type=F32) + b_ref[...]


def _ada_call(c_pad, w_ada, b_ada):
    rows, d = c_pad.shape
    n = w_ada.shape[1]
    return pl.pallas_call(
        _ada_kernel,
        grid=(n // ADA_TN,),
        in_specs=[pl.BlockSpec((rows, d), lambda j: (0, 0)),
                  pl.BlockSpec((d, ADA_TN), lambda j: (0, j)),
                  pl.BlockSpec((1, ADA_TN), lambda j: (0, j))],
        out_specs=pl.BlockSpec((rows, ADA_TN), lambda j: (0, j)),
        out_shape=jax.ShapeDtypeStruct((rows, n), F32),
        compiler_params=_cparams(("arbitrary",)),
        name="ada",
    )(c_pad, w_ada, b_ada)


def _rope(x, cos, sin_lo, sin_hi, half):
    n = x.shape[-1]
    return (x * cos + pltpu.roll(x, n - half, 1) * sin_lo + pltpu.roll(x, half, 1) * sin_hi)


def _inproj_kernel(x_ref, sc_ref, sh_ref, g_ref, pos_ref, inv_qk_ref, inv_ix_ref,
                   mlo_qk_ref, mhi_qk_ref, mlo_ix_ref, mhi_ix_ref, gq_ref, gk_ref,
                   wu_ref, wq_ref, wkv_ref, wqi_ref, wkw_ref,
                   u_out, q_out, k_out, v_out, qi_out, ki_out, wv_out):
    x = x_ref[...]
    ms = jnp.mean(x * x, axis=-1, keepdims=True)
    h = x * lax.rsqrt(ms + EPS) * g_ref[...]
    h = h * (1.0 + sc_ref[...]) + sh_ref[...]
    hb = h.astype(BF16)

    u_out[...] = jnp.dot(hb, wu_ref[...], preferred_element_type=F32).astype(BF16)

    pos = pos_ref[...].astype(F32)
    ang = pos * inv_qk_ref[...]
    cos_qk = jnp.cos(ang)
    sin_qk = jnp.sin(ang)
    slo_qk = -sin_qk * mlo_qk_ref[...]
    shi_qk = sin_qk * mhi_qk_ref[...]
    ang = pos * inv_ix_ref[...]
    cos_ix = jnp.cos(ang)
    sin_ix = jnp.sin(ang)
    slo_ix = -sin_ix * mlo_ix_ref[...]
    shi_ix = sin_ix * mhi_ix_ref[...]
    half_qk = HEAD_DIM // ROPE_FRAC // 2
    half_ix = IDX_DIM // ROPE_FRAC // 2

    def head_norm(t, g):
        return t * lax.rsqrt(jnp.mean(t * t, axis=-1, keepdims=True) + EPS) * g

    q = jnp.dot(hb, wq_ref[...], preferred_element_type=F32)
    attn_scale = HEAD_DIM ** -0.5
    for hd in range(N_HEADS):
        sl = slice(hd * HEAD_DIM, (hd + 1) * HEAD_DIM)
        t = _rope(head_norm(q[:, sl], gq_ref[...]), cos_qk, slo_qk, shi_qk, half_qk)
        q_out[:, sl] = (t * attn_scale).astype(BF16)

    kv = jnp.dot(hb, wkv_ref[...], preferred_element_type=F32)
    nk = N_KV_HEADS * HEAD_DIM
    one_col = (lax.broadcasted_iota(I32, (x.shape[0], HEAD_DIM), 1) == 0).astype(BF16)
    for hd in range(N_KV_HEADS):
        sl = slice(hd * HEAD_DIM, (hd + 1) * HEAD_DIM)
        t = _rope(head_norm(kv[:, sl], gk_ref[...]), cos_qk, slo_qk, shi_qk, half_qk)
        k_out[:, 2 * hd * HEAD_DIM:(2 * hd + 1) * HEAD_DIM] = t.astype(BF16)
        k_out[:, (2 * hd + 1) * HEAD_DIM:(2 * hd + 2) * HEAD_DIM] = one_col
    vrows = HEAD_DIM + DSA_ONES
    ones = jnp.ones((DSA_ONES, DSA_TK), BF16)
    for t in range(v_out.shape[0]):
        for hd in range(N_KV_HEADS):
            vh = kv[t * DSA_TK:(t + 1) * DSA_TK, nk + hd * HEAD_DIM:nk + (hd + 1) * HEAD_DIM]
            v_out[t, hd * vrows:hd * vrows + HEAD_DIM, :] = vh.T.astype(BF16)
            v_out[t, hd * vrows + HEAD_DIM:(hd + 1) * vrows, :] = ones

    qi = jnp.dot(hb, wqi_ref[...], preferred_element_type=F32)
    idx_scale = IDX_DIM ** -0.5
    for pr in range(N_IDX_HEADS * IDX_DIM // LANES):
        sl = slice(pr * LANES, (pr + 1) * LANES)
        t = _rope(qi[:, sl], cos_ix, slo_ix, shi_ix, half_ix)
        qi_out[:, sl] = (t * idx_scale).astype(BF16)

    kw = jnp.dot(hb, wkw_ref[...], preferred_element_type=F32)
    ki_out[...] = _rope(kw[:, :LANES], cos_ix, slo_ix, shi_ix, half_ix).astype(BF16)
    wv_out[...] = kw[:, LANES:] * (N_IDX_HEADS ** -0.5)


def _inproj_call(x2, sc, sh, g, pos2, tabs, gq, gk, wu, wq, wkv, wqi, wkw, seq):
    n, d = x2.shape
    tm = INPROJ_TM
    per_b = seq // tm
    row = lambda i: (i, 0)
    modmap = lambda i: (i // per_b, 0, 0)
    small = [_const_spec(t.shape) for t in tabs]
    in_specs = ([pl.BlockSpec((tm, d), row),
                 pl.BlockSpec((None, 1, d), modmap),
                 pl.BlockSpec((None, 1, d), modmap),
                 _const_spec(g.shape),
                 pl.BlockSpec((tm, 1), row)]
                + small
                + [_const_spec(gq.shape), _const_spec(gk.shape),
                   _const_spec(wu.shape), _const_spec(wq.shape), _const_spec(wkv.shape),
                   _const_spec(wqi.shape), _const_spec(wkw.shape)])
    widths = (D_SSM, N_HEADS * HEAD_DIM, 2 * N_KV_HEADS * HEAD_DIM, N_KV_HEADS * HEAD_DIM,
              N_IDX_HEADS * IDX_DIM, LANES, LANES)
    dtypes = (BF16, BF16, BF16, BF16, BF16, BF16, F32)
    vt_rows = N_KV_HEADS * (HEAD_DIM + DSA_ONES)
    out_specs = [pl.BlockSpec((tm, w), row) for w in widths]
    out_shape = [jax.ShapeDtypeStruct((n, w), dt) for w, dt in zip(widths, dtypes)]
    out_specs[3] = pl.BlockSpec((tm // DSA_TK, vt_rows, DSA_TK), lambda i: (i, 0, 0))
    out_shape[3] = jax.ShapeDtypeStruct((n // DSA_TK, vt_rows, DSA_TK), BF16)
    return pl.pallas_call(
        _inproj_kernel,
        grid=(n // tm,),
        in_specs=in_specs,
        out_specs=out_specs,
        out_shape=out_shape,
        compiler_params=_cparams(("parallel",)),
        name="inproj",
    )(x2, sc, sh, g, pos2, *tabs, gq, gk, wu, wq, wkv, wqi, wkw)


def _s5prep_kernel(lr_ref, li_ref, dt_ref, bre_ref, bim_ref, pr_ref, pi_ref, bbr_ref, bbi_ref):
    lr = jnp.minimum(lr_ref[...], -1e-4)
    li = li_ref[...]
    dt = dt_ref[...]
    mag = jnp.exp(lr * dt)
    ab_r = mag * jnp.cos(li * dt)
    ab_i = mag * jnp.sin(li * dt)
    den = lr * lr + li * li
    fr = ((ab_r - 1.0) * lr + ab_i * li) / den
    fi = (ab_i * lr - (ab_r - 1.0) * li) / den
    bbr_ref[...] = fr * bre_ref[...] - fi * bim_ref[...]
    bbi_ref[...] = fr * bim_ref[...] + fi * bre_ref[...]
    steps = (lax.broadcasted_iota(I32, pr_ref.shape, 0) + 1).astype(F32)
    magn = jnp.exp(steps * (lr * dt))
    pr_ref[...] = magn * jnp.cos(steps * (li * dt))
    pi_ref[...] = magn * jnp.sin(steps * (li * dt))


def _s5prep_call(lr, li, dt, bre_t, bim_t, chunk):
    gp = lr.shape[1]
    hh = bre_t.shape[0]
    outs = [jax.ShapeDtypeStruct((chunk, gp), F32)] * 2 + [jax.ShapeDtypeStruct((hh, gp), F32)] * 2
    return pl.pallas_call(_s5prep_kernel, out_shape=outs, name="s5prep")(lr, li, dt, bre_t, bim_t)


def _s5_kernel(u_ref, bw_ref, cw_ref, pw_ref, dsk_ref, y_ref, st_ref, cin_ref, xs_ref):
    tt, rr, _ = u_ref.shape
    ns = st_ref.shape[1] // 2
    ar = pw_ref[0:1, :ns]
    ai = pw_ref[0:1, ns:]
    st_ref[...] = jnp.zeros(st_ref.shape, F32)

    def cproj(zr, zi):
        return (jnp.dot(zr.astype(BF16), cw_ref[:ns, :], preferred_element_type=F32)
                + jnp.dot(zi.astype(BF16), cw_ref[ns:, :], preferred_element_type=F32))

    xs_ref[0] = jnp.dot(u_ref[0], bw_ref[...], preferred_element_type=F32)

    def local_step(i, carry):
        x = xs_ref[i % 2]
        nxt = jnp.minimum(i + 1, tt - 1)
        xs_ref[(i + 1) % 2] = jnp.dot(u_ref[nxt], bw_ref[...], preferred_element_type=F32)
        sr = st_ref[:, :ns]
        si = st_ref[:, ns:]
        nr = ar * sr - ai * si + x[:, :ns]
        ni = ar * si + ai * sr + x[:, ns:]
        st_ref[:, :ns] = nr
        st_ref[:, ns:] = ni
        y_ref[i] = cproj(nr, ni) + dsk_ref[...] * u_ref[i].astype(F32)
        return carry

    lax.fori_loop(0, tt, local_step, 0)

    atr = pw_ref[tt - 1:tt, :ns]
    ati = pw_ref[tt - 1:tt, ns:]

    def chain(r, carry):
        cr, ci = carry
        cin_ref[pl.ds(r, 1), :ns] = cr
        cin_ref[pl.ds(r, 1), ns:] = ci
        er = st_ref[pl.ds(r, 1), :ns]
        ei = st_ref[pl.ds(r, 1), ns:]
        return (atr * cr - ati * ci + er, atr * ci + ati * cr + ei)

    zero = jnp.zeros((1, ns), F32)
    lax.fori_loop(0, rr, chain, (zero, zero))

    def carry_step(i, carry):
        pr = pw_ref[pl.ds(i, 1), :ns]
        pi = pw_ref[pl.ds(i, 1), ns:]
        cr = cin_ref[:, :ns]
        ci = cin_ref[:, ns:]
        y_ref[i] += cproj(pr * cr - pi * ci, pr * ci + pi * cr)
        return carry

    lax.fori_loop(0, tt, carry_step, 0)


def _s5_call(u_t, bw, cw, pw, dsk):
    bsz, tt, rr, dch = u_t.shape
    nb = bw.shape[0]
    cb = bw.shape[1]
    ns2 = bw.shape[2]
    return pl.pallas_call(
        _s5_kernel,
        grid=(bsz, nb),
        in_specs=[pl.BlockSpec((None, tt, rr, cb), lambda b, j: (b, 0, 0, j)),
                  pl.BlockSpec((None, cb, ns2), lambda b, j: (j, 0, 0)),
                  pl.BlockSpec((None, ns2, cb), lambda b, j: (j, 0, 0)),
                  pl.BlockSpec((None, tt, ns2), lambda b, j: (j, 0, 0)),
                  pl.BlockSpec((1, cb), lambda b, j: (0, j))],
        out_specs=pl.BlockSpec((None, tt, rr, cb), lambda b, j: (b, 0, 0, j)),
        out_shape=jax.ShapeDtypeStruct((bsz, tt, rr, dch), F32),
        scratch_shapes=[pltpu.VMEM((rr, ns2), F32), pltpu.VMEM((rr, ns2), F32),
                        pltpu.VMEM((2, rr, ns2), F32)],
        compiler_params=_cparams(("parallel", "parallel")),
        name="s5scan",
    )(u_t, bw, cw, pw, dsk)


def _order_key(x):
    bits = pltpu.bitcast(x, I32)
    return jnp.where(bits < 0, bits ^ jnp.int32(0x7FFFFFFF), bits)


def _dsa_kernel(q_ref, qi_ref, wv_ref, k_ref, vt_ref, ki_ref, o_ref,
                key_sc, kb_sc, lhs_sc, wt_sc, qs_sc, m_sc, acc_sc, s_sc, p_sc, al_sc, kn_sm,
                *, topk):
    tq = q_ref.shape[0]
    tk = key_sc.shape[1]
    qb = pl.program_id(1)
    nkt = (qb * tq + tq + tk - 1) // tk

    lane = lax.broadcasted_iota(I32, (tq, LANES), 1)
    kiota = lax.broadcasted_iota(I32, (tk, tq), 0)
    qpos = qb * tq + lax.broadcasted_iota(I32, (tk, tq), 1)

    @pl.when(qb == 0)
    def _():
        rows = min(k_ref.shape[0], 1024)
        for g in range(N_KV_HEADS):
            def norm_max(c, best):
                r0 = pl.multiple_of(c * rows, rows)
                kk = k_ref[pl.ds(r0, rows), g * 2 * HEAD_DIM:g * 2 * HEAD_DIM + HEAD_DIM].astype(F32)
                return jnp.maximum(best, jnp.max(jnp.sum(kk * kk, axis=1, keepdims=True)))

            kn_sm[g] = lax.fori_loop(0, k_ref.shape[0] // rows, norm_max, jnp.float32(0.0))

    for hd in range(N_IDX_HEADS):
        pair = qi_ref[:, (hd // 2) * LANES:(hd // 2 + 1) * LANES].astype(F32)
        keep = (lane < IDX_DIM) if hd % 2 == 0 else (lane >= IDX_DIM)
        lhs_sc[hd] = jnp.where(keep, pair, 0.0).T.astype(BF16)
    wt_sc[...] = wv_ref[...].T

    def score_tiles(kt, ntile):
        k0 = pl.multiple_of(kt * tk, tk)
        kit = ki_ref[pl.ds(k0, ntile * tk), :]
        acc = jnp.zeros((ntile * tk, tq), F32)
        for hd in range(N_IDX_HEADS):
            lg = jnp.dot(kit, lhs_sc[hd], preferred_element_type=F32)
            acc = acc + jnp.maximum(lg, 0.0) * wt_sc[hd:hd + 1, :]
        for j in range(ntile):
            a = jnp.where(kiota + (k0 + j * tk) <= qpos, acc[j * tk:(j + 1) * tk], -jnp.inf)
            bits = pltpu.bitcast(a, I32)
            key_sc[kt + j] = jnp.where(bits < 0, bits ^ jnp.int32(0x7FFFFFFF), bits)
            kb_sc[kt + j] = pltpu.bitcast(bits & jnp.int32(-65536), F32).astype(BF16)

    def score_quad(kq, carry):
        score_tiles(4 * kq, 4)
        return carry

    lax.fori_loop(0, nkt // 4, score_quad, 0)

    @pl.when(nkt % 4 >= 2)
    def _():
        score_tiles((nkt // 4) * 4, 2)

    @pl.when(nkt % 2 == 1)
    def _():
        score_tiles(nkt - 1, 1)

    def search(count, bit0, t0, n0):
        def cond(st):
            bit, _, cnt = st
            return jnp.logical_and(bit >= 0, jnp.max(jnp.abs(cnt - topk)) > 0)

        def body(st):
            bit, t, cnt = st
            for _ in range(2):
                cand = t + jnp.left_shift(jnp.int32(1), jnp.maximum(bit, 0))
                c = count(cand)
                ok = jnp.logical_and(c >= topk, bit >= 0)
                t = jnp.where(ok, cand, t)
                cnt = jnp.where(ok, c, cnt)
                bit = bit - 1
            return bit, t, cnt

        _, t, cnt = lax.while_loop(cond, body, (jnp.asarray(bit0, I32), t0, n0))
        return t, cnt

    one_b = jnp.ones((tk, tq), BF16)
    zero_b = jnp.zeros((tk, tq), BF16)
    sub = tk // 16

    def count16(c16):
        c16 = jnp.maximum(c16, KEY16_NEG_INF)
        pat = jnp.where(c16 >= 0, c16, c16 ^ jnp.int32(0x7FFF))
        cand = pltpu.bitcast(jnp.left_shift(pat, 16), F32).astype(BF16)

        def hits(kt):
            hit = jnp.where(kb_sc[kt] >= cand, one_b, zero_b)
            part = hit[0:16]
            for j in range(1, sub):
                part = part + hit[j * 16:(j + 1) * 16]
            return part

        def quad(kq, acc):
            part = hits(4 * kq)
            for j in range(1, 4):
                part = part + hits(4 * kq + j)
            return acc + part.astype(F32)

        def single(kt, acc):
            return acc + hits(kt).astype(F32)

        acc = lax.fori_loop(0, nkt // 4, quad, jnp.zeros((16, tq), F32))
        acc = lax.fori_loop((nkt // 4) * 4, nkt, single, acc)
        return jnp.sum(acc, axis=0, keepdims=True).astype(I32)

    def count32(cand):
        def single(kt, acc):
            hit = jnp.where(key_sc[kt] >= cand, 1, 0).astype(I32)
            return acc + jnp.sum(hit.reshape(tk // 8, 8, tq), axis=0)

        def quad(kq, acc):
            for j in range(4):
                acc = single(4 * kq + j, acc)
            return acc

        acc = lax.fori_loop(0, nkt // 4, quad, jnp.zeros((8, tq), I32))
        acc = lax.fori_loop((nkt // 4) * 4, nkt, single, acc)
        return jnp.sum(acc, axis=0, keepdims=True)

    c0 = count16(jnp.zeros((1, tq), I32))
    pos_ok = c0 >= topk
    def upper_bit(i, st):
        t, cnt = st
        cand = t + jnp.left_shift(jnp.int32(1), 14 - i)
        c = count16(cand)
        ok = c >= topk
        return jnp.where(ok, cand, t), jnp.where(ok, c, cnt)

    t16, c16 = lax.fori_loop(0, 15, upper_bit,
                             (jnp.where(pos_ok, 0, -32768).astype(I32),
                              jnp.where(pos_ok, c0, nkt * tk).astype(I32)))
    band = jnp.logical_and(t16 >= KEY16_BAND_LO, t16 <= KEY16_BAND_HI)
    t32 = jnp.left_shift(jnp.where(band, KEY16_BAND_LO, t16), 16)
    c32 = jnp.where(band, nkt * tk + 1, c16).astype(I32)
    any_band = jnp.max(band.astype(I32)) > 0
    thr, cnt = search(count32, jnp.where(any_band, 24, 15), t32, c32)
    thr = jnp.maximum(thr, KEY_NEG_INF + 1)

    @pl.when(jnp.max(cnt) > topk)
    def _():
        need = topk - count32(thr + 1)
        tri = (lax.broadcasted_iota(I32, (tk, tk), 1)
               <= lax.broadcasted_iota(I32, (tk, tk), 0)).astype(BF16)

        def demote(kt, seen):
            keys = key_sc[kt]
            eq = keys == thr
            rank = seen + jnp.dot(tri, jnp.where(eq, 1.0, 0.0).astype(BF16),
                                  preferred_element_type=F32)
            key_sc[kt] = jnp.where(jnp.logical_and(eq, rank > need.astype(F32)), thr - 1, keys)
            return rank[tk - 1:tk, :]

        lax.fori_loop(0, nkt, demote, jnp.zeros((1, tq), F32))

    kv_w = 2 * HEAD_DIM
    shifts = []
    for g in range(N_KV_HEADS):
        parts = []
        for r in range(KV_REP):
            hd = g * KV_REP + r
            qht = q_ref[:, hd * HEAD_DIM:(hd + 1) * HEAD_DIM].astype(F32).T
            qs_sc[g, :HEAD_DIM, r * tq:(r + 1) * tq] = qht.astype(BF16)
            qn2 = jnp.sum(qht * qht, axis=0, keepdims=True)
            parts.append(jnp.sqrt(qn2 * kn_sm[g]) * SHIFT_MARGIN)
        shifts.append(jnp.concatenate(parts, axis=1))
    bound = jnp.maximum(jnp.max(shifts[0]), jnp.max(shifts[1]))
    fast = 2.0 * bound <= MAX_SPAN
    row0 = lax.broadcasted_iota(I32, (HEAD_DIM, KV_REP * tq), 0) == 0
    for g in range(N_KV_HEADS):
        shift_row = jnp.where(fast, -shifts[g], 0.0)
        qs_sc[g, HEAD_DIM:, :] = jnp.where(row0, shift_row, 0.0).astype(BF16)

    vrows = acc_sc.shape[1]

    def qk(kt, g):
        k0 = pl.multiple_of(kt * tk, tk)
        kg = k_ref[pl.ds(k0, tk), g * kv_w:(g + 1) * kv_w]
        s_sc[g] = jnp.dot(kg, qs_sc[g], preferred_element_type=F32)

    def softmax_shifted(g, bias):
        p_sc[g] = jnp.exp(s_sc[g] + bias).astype(BF16)

    def pv_shifted(kt, g):
        vtg = vt_ref[kt, g * vrows:(g + 1) * vrows, :]
        acc_sc[g] += jnp.dot(vtg, p_sc[g], preferred_element_type=F32)

    def softmax_online(g, bias):
        s = s_sc[g] + bias
        m_old = m_sc[g]
        m_new = jnp.maximum(m_old, jnp.max(s, axis=0, keepdims=True))
        al_sc[g] = jnp.exp(m_old - m_new)
        p_sc[g] = jnp.exp(s - m_new).astype(BF16)
        m_sc[g] = m_new

    def pv_online(kt, g):
        vtg = vt_ref[kt, g * vrows:(g + 1) * vrows, :]
        acc_sc[g] = al_sc[g] * acc_sc[g] + jnp.dot(vtg, p_sc[g], preferred_element_type=F32)

    def attend(softmax, pv, per_trip):
        m_sc[...] = jnp.full(m_sc.shape, NEG_BIG, F32)
        acc_sc[...] = jnp.zeros(acc_sc.shape, F32)
        p_sc[1] = jnp.zeros(p_sc.shape[1:], BF16)
        al_sc[1] = jnp.ones(al_sc.shape[1:], F32)

        def attn_tile(kt, carry):
            bias = jnp.where(key_sc[kt] >= thr, 0.0, NEG_BIG)
            bias = jnp.concatenate([bias] * KV_REP, axis=1)
            qk(kt, 1)
            softmax(0, bias)
            pv(jnp.maximum(kt - 1, 0), 1)
            qk(jnp.minimum(kt + 1, nkt - 1), 0)
            softmax(1, bias)
            pv(kt, 0)
            return carry

        def attn_group(kg, carry):
            for j in range(per_trip):
                attn_tile(per_trip * kg + j, carry)
            return carry

        qk(0, 0)
        lax.fori_loop(0, nkt // per_trip, attn_group, 0)
        lax.fori_loop((nkt // per_trip) * per_trip, nkt, attn_tile, 0)
        pv(nkt - 1, 1)

    @pl.when(fast)
    def _():
        attend(softmax_shifted, pv_shifted, 4)

    @pl.when(jnp.logical_not(fast))
    def _():
        attend(softmax_online, pv_online, 1)

    for g in range(N_KV_HEADS):
        out = acc_sc[g, :HEAD_DIM, :] / acc_sc[g, HEAD_DIM:HEAD_DIM + 1, :]
        for r in range(KV_REP):
            hd = g * KV_REP + r
            o_ref[:, hd * HEAD_DIM:(hd + 1) * HEAD_DIM] = out[:, r * tq:(r + 1) * tq].T.astype(o_ref.dtype)


def _dsa_call(q, qi, wv, k, vt, ki, topk):
    bsz, seq, dq = q.shape
    tq = DSA_TQ
    nkt, dv, tk = vt.shape[1:]
    nq = seq // tq
    qmap = lambda b, i: (b, i, 0)

    def whole(shape):
        return pl.BlockSpec((None,) + tuple(shape), lambda b, i: (b,) + (0,) * len(shape),
                            pipeline_mode=pl.Buffered(1))

    return pl.pallas_call(
        functools.partial(_dsa_kernel, topk=topk),
        grid=(bsz, nq),
        in_specs=[pl.BlockSpec((None, tq, dq), qmap),
                  pl.BlockSpec((None, tq, qi.shape[2]), qmap),
                  pl.BlockSpec((None, tq, wv.shape[2]), qmap),
                  whole(k.shape[1:]), whole(vt.shape[1:]), whole(ki.shape[1:])],
        out_specs=pl.BlockSpec((None, tq, dq), qmap),
        out_shape=jax.ShapeDtypeStruct((bsz, seq, dq), BF16),
        scratch_shapes=[pltpu.VMEM((nkt, tk, tq), I32),
                        pltpu.VMEM((nkt, tk, tq), BF16),
                        pltpu.VMEM((N_IDX_HEADS, LANES, tq), BF16),
                        pltpu.VMEM((LANES, tq), F32),
                        pltpu.VMEM((N_KV_HEADS, 2 * HEAD_DIM, KV_REP * tq), BF16),
                        pltpu.VMEM((N_KV_HEADS, 1, KV_REP * tq), F32),
                        pltpu.VMEM((N_KV_HEADS, dv // N_KV_HEADS, KV_REP * tq), F32),
                        pltpu.VMEM((N_KV_HEADS, tk, KV_REP * tq), F32),
                        pltpu.VMEM((N_KV_HEADS, tk, KV_REP * tq), BF16),
                        pltpu.VMEM((N_KV_HEADS, 1, KV_REP * tq), F32),
                        pltpu.SMEM((N_KV_HEADS,), F32)],
        compiler_params=_cparams(("parallel", "arbitrary")),
        name="dsa",
    )(q, qi, wv, k, vt, ki)


def _outproj_kernel(x_ref, ys_ref, ya_ref, gt_ref, sc_ref, sh_ref, wglu_ref, bglu_ref,
                    gs_ref, ga_ref, wout_ref, gm_ref, x1_out, h2_out):
    def rms(t, g):
        return t * lax.rsqrt(jnp.mean(t * t, axis=-1, keepdims=True) + EPS) * g

    y = ys_ref[...]
    ya = 0.5 * y * (1.0 + jnp.tanh(math.sqrt(2.0 / math.pi) * (y + 0.044715 * (y * y * y))))
    z = jnp.dot(ya.astype(BF16), wglu_ref[...], preferred_element_type=F32) + bglu_ref[...]
    o = ya / (1.0 + jnp.exp(-z))
    n1 = rms(o, gs_ref[...]).astype(BF16)
    n2 = rms(ya_ref[...].astype(F32), ga_ref[...]).astype(BF16)
    d1 = n1.shape[1]
    mixw = (jnp.dot(n1, wout_ref[:d1, :], preferred_element_type=F32)
            + jnp.dot(n2, wout_ref[d1:, :], preferred_element_type=F32))
    x1 = x_ref[...] + gt_ref[...] * mixw
    x1_out[...] = x1
    h2 = rms(x1, gm_ref[...]) * (1.0 + sc_ref[...]) + sh_ref[...]
    h2_out[...] = h2.astype(BF16)


def _outproj_call(x2, ys, ya, gt, sc, sh, wglu, bglu, gs, ga, wout, gm, seq):
    n, d = x2.shape
    tm = OUT_TM
    per_b = seq // tm
    row = lambda i: (i, 0)
    modmap = lambda i: (i // per_b, 0, 0)
    return pl.pallas_call(
        _outproj_kernel,
        grid=(n // tm,),
        in_specs=[pl.BlockSpec((tm, d), row),
                  pl.BlockSpec((tm, ys.shape[1]), row),
                  pl.BlockSpec((tm, ya.shape[1]), row),
                  pl.BlockSpec((None, 1, d), modmap),
                  pl.BlockSpec((None, 1, d), modmap),
                  pl.BlockSpec((None, 1, d), modmap),
                  _const_spec(wglu.shape), _const_spec(bglu.shape),
                  _const_spec(gs.shape), _const_spec(ga.shape),
                  _const_spec(wout.shape), _const_spec(gm.shape)],
        out_specs=[pl.BlockSpec((tm, d), row), pl.BlockSpec((tm, d), row)],
        out_shape=[jax.ShapeDtypeStruct((n, d), F32), jax.ShapeDtypeStruct((n, d), BF16)],
        compiler_params=_cparams(("parallel",)),
        name="outproj",
    )(x2, ys, ya, gt, sc, sh, wglu, bglu, gs, ga, wout, gm)


def _mlp_kernel(h_ref, x1_ref, gt_ref, w1_ref, w2_ref, o_ref):
    j = pl.program_id(1)

    @pl.when(j == 0)
    def _():
        o_ref[...] = jnp.zeros(o_ref.shape, F32)

    a = jnp.maximum(jnp.dot(h_ref[...], w1_ref[...], preferred_element_type=F32), 0.0)
    o_ref[...] += jnp.dot((a * a).astype(BF16), w2_ref[...], preferred_element_type=F32)

    @pl.when(j == pl.num_programs(1) - 1)
    def _():
        o_ref[...] = x1_ref[...] + gt_ref[...] * o_ref[...]


def _mlp_call(h2, x1, gt, w1, w2, seq):
    n, d = x1.shape
    dff = w1.shape[1]
    tm, tf = MLP_TM, MLP_TF
    per_b = seq // tm
    return pl.pallas_call(
        _mlp_kernel,
        grid=(n // tm, dff // tf),
        in_specs=[pl.BlockSpec((tm, d), lambda i, j: (i, 0), pipeline_mode=pl.Buffered(1)),
                  pl.BlockSpec((tm, d), lambda i, j: (i, 0), pipeline_mode=pl.Buffered(1)),
                  pl.BlockSpec((None, 1, d), lambda i, j: (i // per_b, 0, 0)),
                  pl.BlockSpec((d, tf), lambda i, j: (0, j)),
                  pl.BlockSpec((tf, d), lambda i, j: (j, 0))],
        out_specs=pl.BlockSpec((tm, d), lambda i, j: (i, 0)),
        out_shape=jax.ShapeDtypeStruct((n, d), F32),
        compiler_params=_cparams(("parallel", "arbitrary")),
        name="mlp",
    )(h2, x1, gt, w1, w2)


def _rope_tables():
    def tables(head_dim, per_vreg):
        r = head_dim // ROPE_FRAC
        half = r // 2
        inv = ROPE_THETA ** (-jnp.arange(half, dtype=F32) / half)
        lane = np.arange(LANES) % head_dim
        inv_l = jnp.where(lane < r, inv[lane % half], 0.0).astype(F32)[None, :]
        lo = jnp.asarray((lane < half).astype(np.float32))[None, :]
        hi = jnp.asarray(((lane >= half) & (lane < r)).astype(np.float32))[None, :]
        return inv_l, lo, hi

    inv_qk, lo_qk, hi_qk = tables(HEAD_DIM, 1)
    inv_ix, lo_ix, hi_ix = tables(IDX_DIM, 2)
    return [inv_qk, inv_ix, lo_qk, hi_qk, lo_ix, hi_ix]


def _block_diag(m):
    nb, g, a, b = m.shape
    eye = jnp.eye(g, dtype=m.dtype)
    return jnp.einsum('ngab,gk->ngakb', m, eye).reshape(nb, g * a, g * b)


def kernel(x, c, positions, w_ada, b_ada, g_norm_mix, w_in, lam_re, lam_im, log_dt,
           b_re, b_im, c_re, c_im, d_skip, w_glu, b_glu, g_q, g_k, g_out_ssm, g_out_attn,
           w_out, g_norm_mlp, w_mlp_in, w_mlp_out):
    bsz, seq, d = x.shape
    depth = w_ada.shape[0]
    n = bsz * seq
    topk = min(TOPK_MAX, seq // 4)
    gg, pp, hh = N_SSM_GROUPS, SSM_STATE, SSM_GROUP
    nblk = gg // S5_GB
    chunk = S5_CHUNK
    nchunks = seq // chunk

    c_pad = jnp.zeros((8, d), F32).at[:bsz].set(c.astype(F32))
    pos2 = positions.reshape(n, 1).astype(I32)
    tabs = _rope_tables()
    xcur = x.reshape(n, d).astype(F32)

    for l in range(depth):
        mod = _ada_call(c_pad, w_ada[l], b_ada[l][None, :])[:bsz]
        mod = mod.reshape(bsz, N_MOD, 1, d)
        sh_a, sc_a, gt_a, sh_m, sc_m, gt_m = [mod[:, i] for i in range(N_MOD)]

        wi = w_in[l]
        o0 = D_SSM
        o1 = o0 + N_HEADS * HEAD_DIM
        o2 = o1 + 2 * N_KV_HEADS * HEAD_DIM
        o3 = o2 + N_IDX_HEADS * IDX_DIM
        o4 = o3 + IDX_DIM
        w_ki = wi[:, o3:o4]
        w_wi = wi[:, o4:]
        w_kw = jnp.concatenate(
            [w_ki, w_ki, w_wi, jnp.zeros((d, 2 * LANES - 2 * IDX_DIM - N_IDX_HEADS), wi.dtype)], axis=1)
        u, q, k, v, qi, ki, wv = _inproj_call(
            xcur, sc_a, sh_a, g_norm_mix[l][None, :], pos2, tabs,
            g_q[l][None, :], g_k[l][None, :],
            wi[:, :o0].astype(BF16), wi[:, o0:o1].astype(BF16), wi[:, o1:o2].astype(BF16),
            wi[:, o2:o3].astype(BF16), w_kw.astype(BF16), seq)

        lr = lam_re[l].astype(F32).reshape(1, gg * pp)
        li = lam_im[l].astype(F32).reshape(1, gg * pp)
        dt = jnp.broadcast_to(jnp.exp(log_dt[l].astype(F32))[:, None], (gg, pp)).reshape(1, gg * pp)
        bre_t = b_re[l].astype(F32).transpose(2, 0, 1).reshape(hh, gg * pp)
        bim_t = b_im[l].astype(F32).transpose(2, 0, 1).reshape(hh, gg * pp)
        pw_r, pw_i, bb_r, bb_i = _s5prep_call(lr, li, dt, bre_t, bim_t, chunk)
        bbr = bb_r.reshape(hh, nblk, S5_GB, pp).transpose(1, 2, 0, 3)
        bbi = bb_i.reshape(hh, nblk, S5_GB, pp).transpose(1, 2, 0, 3)
        bw = jnp.concatenate([_block_diag(bbr), _block_diag(bbi)], axis=2).astype(BF16)
        cr = c_re[l].astype(F32).reshape(nblk, S5_GB, hh, pp).transpose(0, 1, 3, 2)
        ci = c_im[l].astype(F32).reshape(nblk, S5_GB, hh, pp).transpose(0, 1, 3, 2)
        cw = jnp.concatenate([_block_diag(cr), -_block_diag(ci)], axis=1).astype(BF16)
        ns = S5_GB * pp
        pw = jnp.concatenate([pw_r.reshape(chunk, nblk, ns), pw_i.reshape(chunk, nblk, ns)],
                             axis=2).transpose(1, 0, 2)
        u_t = u.reshape(bsz, nchunks, chunk, D_SSM).transpose(0, 2, 1, 3)
        y_t = _s5_call(u_t, bw, cw, pw, d_skip[l].astype(F32)[None, :])
        y_ssm = y_t.transpose(0, 2, 1, 3).reshape(n, D_SSM)

        v_t = v.reshape(bsz, seq // DSA_TK, v.shape[1], DSA_TK)
        y_attn = _dsa_call(q.reshape(bsz, seq, -1), qi.reshape(bsz, seq, -1),
                           wv.reshape(bsz, seq, -1), k.reshape(bsz, seq, -1),
                           v_t, ki.reshape(bsz, seq, -1), topk)
        y_attn = y_attn.reshape(n, -1)

        x1, h2 = _outproj_call(
            xcur, y_ssm, y_attn, gt_a, sc_m, sh_m,
            w_glu[l].astype(BF16), b_glu[l].astype(F32)[None, :],
            g_out_ssm[l].astype(F32)[None, :], g_out_attn[l].astype(F32)[None, :],
            w_out[l].astype(BF16), g_norm_mlp[l].astype(F32)[None, :], seq)

        xcur = _mlp_call(h2, x1, gt_m, w_mlp_in[l].astype(BF16), w_mlp_out[l].astype(BF16), seq)

    return xcur.reshape(bsz, seq, d).astype(x.dtype)
```

```python
import functools
import math

import numpy as np
import jax
import jax.numpy as jnp
from jax import lax
from jax.experimental import pallas as pl
from jax.experimental.pallas import tpu as pltpu

F32 = jnp.float32
BF16 = jnp.bfloat16
I32 = jnp.int32

D_MODEL = 2048
D_SSM = 1024
SSM_GROUP = 16
N_SSM_GROUPS = 64
SSM_STATE = 64
HEAD_DIM = 128
N_HEADS = 8
N_KV_HEADS = 2
KV_REP = N_HEADS // N_KV_HEADS
N_IDX_HEADS = 16
IDX_DIM = 64
TOPK_MAX = 256
ROPE_FRAC = 4
ROPE_THETA = 500000.0
D_FF = 4 * D_MODEL
EPS = 1e-6
N_MOD = 6

LANES = 128
VMEM_LIMIT = 56 * 1024 * 1024

ADA_TN = 1024
INPROJ_TM = 512
S5_CHUNK = 32
S5_GB = 16
DSA_TQ = 256
DSA_TK = 256
DSA_ONES = 16
OUT_TM = 256
MLP_TM = 1024
MLP_TF = 512

NEG_BIG = -1e30
INT_MIN = -2147483648
KEY_NEG_INF = -2139095041
KEY16_NEG_INF = KEY_NEG_INF >> 16
KEY16_BAND_LO = -129
KEY16_BAND_HI = 127
SHIFT_MARGIN = 1.01
MAX_SPAN = 80.0


def _cparams(sem):
    return pltpu.CompilerParams(dimension_semantics=sem, vmem_limit_bytes=VMEM_LIMIT)


def _const_spec(shape):
    nd = len(shape)
    return pl.BlockSpec(shape, lambda *_: (0,) * nd, pipeline_mode=pl.Buffered(1))


def _ada_kernel(c_ref, w_ref, b_ref, o_ref):
    c = c_ref[...]
    act = c / (1.0 + jnp.exp(-c))
    o_ref[...] = jnp.dot(act.astype(BF16), w_ref[...].astype(BF16),
                         preferred_element_type=F32) + b_ref[...]


def _ada_call(c_pad, w_ada, b_ada):
    rows, d = c_pad.shape
    n = w_ada.shape[1]
    return pl.pallas_call(
        _ada_kernel,
        grid=(n // ADA_TN,),
        in_specs=[pl.BlockSpec((rows, d), lambda j: (0, 0)),
                  pl.BlockSpec((d, ADA_TN), lambda j: (0, j)),
                  pl.BlockSpec((1, ADA_TN), lambda j: (0, j))],
        out_specs=pl.BlockSpec((rows, ADA_TN), lambda j: (0, j)),
        out_shape=jax.ShapeDtypeStruct((rows, n), F32),
        compiler_params=_cparams(("arbitrary",)),
        name="ada",
    )(c_pad, w_ada, b_ada)


def _rope(x, cos, sin_lo, sin_hi, half):
    n = x.shape[-1]
    return (x * cos + pltpu.roll(x, n - half, 1) * sin_lo + pltpu.roll(x, half, 1) * sin_hi)


def _inproj_kernel(x_ref, sc_ref, sh_ref, g_ref, pos_ref, inv_qk_ref, inv_ix_ref,
                   mlo_qk_ref, mhi_qk_ref, mlo_ix_ref, mhi_ix_ref, gq_ref, gk_ref,
                   wu_ref, wq_ref, wkv_ref, wqi_ref, wkw_ref,
                   u_out, q_out, k_out, v_out, qi_out, ki_out, wv_out):
    x = x_ref[...]
    ms = jnp.mean(x * x, axis=-1, keepdims=True)
    h = x * lax.rsqrt(ms + EPS) * g_ref[...]
    h = h * (1.0 + sc_ref[...]) + sh_ref[...]
    hb = h.astype(BF16)

    u_out[...] = jnp.dot(hb, wu_ref[...], preferred_element_type=F32).astype(BF16)

    pos = pos_ref[...].astype(F32)
    ang = pos * inv_qk_ref[...]
    cos_qk = jnp.cos(ang)
    sin_qk = jnp.sin(ang)
    slo_qk = -sin_qk * mlo_qk_ref[...]
    shi_qk = sin_qk * mhi_qk_ref[...]
    ang = pos * inv_ix_ref[...]
    cos_ix = jnp.cos(ang)
    sin_ix = jnp.sin(ang)
    slo_ix = -sin_ix * mlo_ix_ref[...]
    shi_ix = sin_ix * mhi_ix_ref[...]
    half_qk = HEAD_DIM // ROPE_FRAC // 2
    half_ix = IDX_DIM // ROPE_FRAC // 2

    def head_norm(t, g):
        return t * lax.rsqrt(jnp.mean(t * t, axis=-1, keepdims=True) + EPS) * g

    q = jnp.dot(hb, wq_ref[...], preferred_element_type=F32)
    attn_scale = HEAD_DIM ** -0.5
    for hd in range(N_HEADS):
        sl = slice(hd * HEAD_DIM, (hd + 1) * HEAD_DIM)
        t = _rope(head_norm(q[:, sl], gq_ref[...]), cos_qk, slo_qk, shi_qk, half_qk)
        q_out[:, sl] = (t * attn_scale).astype(BF16)

    kv = jnp.dot(hb, wkv_ref[...], preferred_element_type=F32)
    nk = N_KV_HEADS * HEAD_DIM
    one_col = (lax.broadcasted_iota(I32, (x.shape[0], HEAD_DIM), 1) == 0).astype(BF16)
    for hd in range(N_KV_HEADS):
        sl = slice(hd * HEAD_DIM, (hd + 1) * HEAD_DIM)
        t = _rope(head_norm(kv[:, sl], gk_ref[...]), cos_qk, slo_qk, shi_qk, half_qk)
        k_out[:, 2 * hd * HEAD_DIM:(2 * hd + 1) * HEAD_DIM] = t.astype(BF16)
        k_out[:, (2 * hd + 1) * HEAD_DIM:(2 * hd + 2) * HEAD_DIM] = one_col
    vrows = HEAD_DIM + DSA_ONES
    ones = jnp.ones((DSA_ONES, DSA_TK), BF16)
    for t in range(v_out.shape[0]):
        for hd in range(N_KV_HEADS):
            vh = kv[t * DSA_TK:(t + 1) * DSA_TK, nk + hd * HEAD_DIM:nk + (hd + 1) * HEAD_DIM]
            v_out[t, hd * vrows:hd * vrows + HEAD_DIM, :] = vh.T.astype(BF16)
            v_out[t, hd * vrows + HEAD_DIM:(hd + 1) * vrows, :] = ones

    qi = jnp.dot(hb, wqi_ref[...], preferred_element_type=F32)
    idx_scale = IDX_DIM ** -0.5
    for pr in range(N_IDX_HEADS * IDX_DIM // LANES):
        sl = slice(pr * LANES, (pr + 1) * LANES)
        t = _rope(qi[:, sl], cos_ix, slo_ix, shi_ix, half_ix)
        qi_out[:, sl] = (t * idx_scale).astype(BF16)

    kw = jnp.dot(hb, wkw_ref[...], preferred_element_type=F32)
    ki_out[...] = _rope(kw[:, :LANES], cos_ix, slo_ix, shi_ix, half_ix).astype(BF16)
    wv_out[...] = kw[:, LANES:] * (N_IDX_HEADS ** -0.5)


def _inproj_call(x2, sc, sh, g, pos2, tabs, gq, gk, wu, wq, wkv, wqi, wkw, seq):
    n, d = x2.shape
    tm = INPROJ_TM
    per_b = seq // tm
    row = lambda i: (i, 0)
    modmap = lambda i: (i // per_b, 0, 0)
    small = [_const_spec(t.shape) for t in tabs]
    in_specs = ([pl.BlockSpec((tm, d), row),
                 pl.BlockSpec((None, 1, d), modmap),
                 pl.BlockSpec((None, 1, d), modmap),
                 _const_spec(g.shape),
                 pl.BlockSpec((tm, 1), row)]
                + small
                + [_const_spec(gq.shape), _const_spec(gk.shape),
                   _const_spec(wu.shape), _const_spec(wq.shape), _const_spec(wkv.shape),
                   _const_spec(wqi.shape), _const_spec(wkw.shape)])
    widths = (D_SSM, N_HEADS * HEAD_DIM, 2 * N_KV_HEADS * HEAD_DIM, N_KV_HEADS * HEAD_DIM,
              N_IDX_HEADS * IDX_DIM, LANES, LANES)
    dtypes = (BF16, BF16, BF16, BF16, BF16, BF16, F32)
    vt_rows = N_KV_HEADS * (HEAD_DIM + DSA_ONES)
    out_specs = [pl.BlockSpec((tm, w), row) for w in widths]
    out_shape = [jax.ShapeDtypeStruct((n, w), dt) for w, dt in zip(widths, dtypes)]
    out_specs[3] = pl.BlockSpec((tm // DSA_TK, vt_rows, DSA_TK), lambda i: (i, 0, 0))
    out_shape[3] = jax.ShapeDtypeStruct((n // DSA_TK, vt_rows, DSA_TK), BF16)
    return pl.pallas_call(
        _inproj_kernel,
        grid=(n // tm,),
        in_specs=in_specs,
        out_specs=out_specs,
        out_shape=out_shape,
        compiler_params=_cparams(("parallel",)),
        name="inproj",
    )(x2, sc, sh, g, pos2, *tabs, gq, gk, wu, wq, wkv, wqi, wkw)


def _s5prep_kernel(lr_ref, li_ref, dt_ref, bre_ref, bim_ref, pr_ref, pi_ref, bbr_ref, bbi_ref):
    lr = jnp.minimum(lr_ref[...], -1e-4)
    li = li_ref[...]
    dt = dt_ref[...]
    mag = jnp.exp(lr * dt)
    ab_r = mag * jnp.cos(li * dt)
    ab_i = mag * jnp.sin(li * dt)
    den = lr * lr + li * li
    fr = ((ab_r - 1.0) * lr + ab_i * li) / den
    fi = (ab_i * lr - (ab_r - 1.0) * li) / den
    bbr_ref[...] = fr * bre_ref[...] - fi * bim_ref[...]
    bbi_ref[...] = fr * bim_ref[...] + fi * bre_ref[...]
    steps = (lax.broadcasted_iota(I32, pr_ref.shape, 0) + 1).astype(F32)
    magn = jnp.exp(steps * (lr * dt))
    pr_ref[...] = magn * jnp.cos(steps * (li * dt))
    pi_ref[...] = magn * jnp.sin(steps * (li * dt))


def _s5prep_call(lr, li, dt, bre_t, bim_t, chunk):
    gp = lr.shape[1]
    hh = bre_t.shape[0]
    outs = [jax.ShapeDtypeStruct((chunk, gp), F32)] * 2 + [jax.ShapeDtypeStruct((hh, gp), F32)] * 2
    return pl.pallas_call(_s5prep_kernel, out_shape=outs, name="s5prep")(lr, li, dt, bre_t, bim_t)


def _s5_kernel(u_ref, bw_ref, cw_ref, pw_ref, dsk_ref, y_ref, st_ref, cin_ref, xs_ref):
    tt, rr, _ = u_ref.shape
    ns = st_ref.shape[1] // 2
    ar = pw_ref[0:1, :ns]
    ai = pw_ref[0:1, ns:]
    st_ref[...] = jnp.zeros(st_ref.shape, F32)

    def cproj(zr, zi):
        return (jnp.dot(zr.astype(BF16), cw_ref[:ns, :], preferred_element_type=F32)
                + jnp.dot(zi.astype(BF16), cw_ref[ns:, :], preferred_element_type=F32))

    xs_ref[0] = jnp.dot(u_ref[0], bw_ref[...], preferred_element_type=F32)

    def local_step(i, carry):
        x = xs_ref[i % 2]
        nxt = jnp.minimum(i + 1, tt - 1)
        xs_ref[(i + 1) % 2] = jnp.dot(u_ref[nxt], bw_ref[...], preferred_element_type=F32)
        sr = st_ref[:, :ns]
        si = st_ref[:, ns:]
        nr = ar * sr - ai * si + x[:, :ns]
        ni = ar * si + ai * sr + x[:, ns:]
        st_ref[:, :ns] = nr
        st_ref[:, ns:] = ni
        y_ref[i] = cproj(nr, ni) + dsk_ref[...] * u_ref[i].astype(F32)
        return carry

    lax.fori_loop(0, tt, local_step, 0)

    atr = pw_ref[tt - 1:tt, :ns]
    ati = pw_ref[tt - 1:tt, ns:]

    def chain(r, carry):
        cr, ci = carry
        cin_ref[pl.ds(r, 1), :ns] = cr
        cin_ref[pl.ds(r, 1), ns:] = ci
        er = st_ref[pl.ds(r, 1), :ns]
        ei = st_ref[pl.ds(r, 1), ns:]
        return (atr * cr - ati * ci + er, atr * ci + ati * cr + ei)

    zero = jnp.zeros((1, ns), F32)
    lax.fori_loop(0, rr, chain, (zero, zero))

    def carry_step(i, carry):
        pr = pw_ref[pl.ds(i, 1), :ns]
        pi = pw_ref[pl.ds(i, 1), ns:]
        cr = cin_ref[:, :ns]
        ci = cin_ref[:, ns:]
        y_ref[i] += cproj(pr * cr - pi * ci, pr * ci + pi * cr)
        return carry

    lax.fori_loop(0, tt, carry_step, 0)


def _s5_call(u_t, bw, cw, pw, dsk):
    bsz, tt, rr, dch = u_t.shape
    nb = bw.shape[0]
    cb = bw.shape[1]
    ns2 = bw.shape[2]
    return pl.pallas_call(
        _s5_kernel,
        grid=(bsz, nb),
        in_specs=[pl.BlockSpec((None, tt, rr, cb), lambda b, j: (b, 0, 0, j)),
                  pl.BlockSpec((None, cb, ns2), lambda b, j: (j, 0, 0)),
                  pl.BlockSpec((None, ns2, cb), lambda b, j: (j, 0, 0)),
                  pl.BlockSpec((None, tt, ns2), lambda b, j: (j, 0, 0)),
                  pl.BlockSpec((1, cb), lambda b, j: (0, j))],
        out_specs=pl.BlockSpec((None, tt, rr, cb), lambda b, j: (b, 0, 0, j)),
        out_shape=jax.ShapeDtypeStruct((bsz, tt, rr, dch), F32),
        scratch_shapes=[pltpu.VMEM((rr, ns2), F32), pltpu.VMEM((rr, ns2), F32),
                        pltpu.VMEM((2, rr, ns2), F32)],
        compiler_params=_cparams(("parallel", "parallel")),
        name="s5scan",
    )(u_t, bw, cw, pw, dsk)


def _order_key(x):
    bits = pltpu.bitcast(x, I32)
    return jnp.where(bits < 0, bits ^ jnp.int32(0x7FFFFFFF), bits)


def _dsa_kernel(q_ref, qi_ref, wv_ref, k_ref, vt_ref, ki_ref, o_ref,
                key_sc, kb_sc, lhs_sc, wt_sc, qs_sc, m_sc, acc_sc, s_sc, p_sc, al_sc, kn_sm,
                *, topk):
    tq = q_ref.shape[0]
    tk = key_sc.shape[1]
    qb = pl.program_id(1)
    nkt = (qb * tq + tq + tk - 1) // tk

    lane = lax.broadcasted_iota(I32, (tq, LANES), 1)
    kiota = lax.broadcasted_iota(I32, (tk, tq), 0)
    qpos = qb * tq + lax.broadcasted_iota(I32, (tk, tq), 1)

    @pl.when(qb == 0)
    def _():
        rows = min(k_ref.shape[0], 1024)
        for g in range(N_KV_HEADS):
            def norm_max(c, best):
                r0 = pl.multiple_of(c * rows, rows)
                kk = k_ref[pl.ds(r0, rows), g * 2 * HEAD_DIM:g * 2 * HEAD_DIM + HEAD_DIM].astype(F32)
                return jnp.maximum(best, jnp.max(jnp.sum(kk * kk, axis=1, keepdims=True)))

            kn_sm[g] = lax.fori_loop(0, k_ref.shape[0] // rows, norm_max, jnp.float32(0.0))

    for hd in range(N_IDX_HEADS):
        pair = qi_ref[:, (hd // 2) * LANES:(hd // 2 + 1) * LANES].astype(F32)
        keep = (lane < IDX_DIM) if hd % 2 == 0 else (lane >= IDX_DIM)
        lhs_sc[hd] = jnp.where(keep, pair, 0.0).T.astype(BF16)
    wt_sc[...] = wv_ref[...].T

    def score_tiles(kt, ntile):
        k0 = pl.multiple_of(kt * tk, tk)
        kit = ki_ref[pl.ds(k0, ntile * tk), :]
        acc = jnp.zeros((ntile * tk, tq), F32)
        for hd in range(N_IDX_HEADS):
            lg = jnp.dot(kit, lhs_sc[hd], preferred_element_type=F32)
            acc = acc + jnp.maximum(lg, 0.0) * wt_sc[hd:hd + 1, :]
        for j in range(ntile):
            a = jnp.where(kiota + (k0 + j * tk) <= qpos, acc[j * tk:(j + 1) * tk], -jnp.inf)
            bits = pltpu.bitcast(a, I32)
            key_sc[kt + j] = jnp.where(bits < 0, bits ^ jnp.int32(0x7FFFFFFF), bits)
            kb_sc[kt + j] = pltpu.bitcast(bits & jnp.int32(-65536), F32).astype(BF16)

    def score_quad(kq, carry):
        score_tiles(4 * kq, 4)
        return carry

    lax.fori_loop(0, nkt // 4, score_quad, 0)

    @pl.when(nkt % 4 >= 2)
    def _():
        score_tiles((nkt // 4) * 4, 2)

    @pl.when(nkt % 2 == 1)
    def _():
        score_tiles(nkt - 1, 1)

    def search(count, bit0, t0, n0):
        def cond(st):
            bit, _, cnt = st
            return jnp.logical_and(bit >= 0, jnp.max(jnp.abs(cnt - topk)) > 0)

        def body(st):
            bit, t, cnt = st
            for _ in range(2):
                cand = t + jnp.left_shift(jnp.int32(1), jnp.maximum(bit, 0))
                c = count(cand)
                ok = jnp.logical_and(c >= topk, bit >= 0)
                t = jnp.where(ok, cand, t)
                cnt = jnp.where(ok, c, cnt)
                bit = bit - 1
            return bit, t, cnt

        _, t, cnt = lax.while_loop(cond, body, (jnp.asarray(bit0, I32), t0, n0))
        return t, cnt

    one_b = jnp.ones((tk, tq), BF16)
    zero_b = jnp.zeros((tk, tq), BF16)
    sub = tk // 16

    def count16(c16):
        c16 = jnp.maximum(c16, KEY16_NEG_INF)
        pat = jnp.where(c16 >= 0, c16, c16 ^ jnp.int32(0x7FFF))
        cand = pltpu.bitcast(jnp.left_shift(pat, 16), F32).astype(BF16)

        def hits(kt):
            hit = jnp.where(kb_sc[kt] >= cand, one_b, zero_b)
            part = hit[0:16]
            for j in range(1, sub):
                part = part + hit[j * 16:(j + 1) * 16]
            return part

        def quad(kq, acc):
            part = hits(4 * kq)
            for j in range(1, 4):
                part = part + hits(4 * kq + j)
            return acc + part.astype(F32)

        def single(kt, acc):
            return acc + hits(kt).astype(F32)

        acc = lax.fori_loop(0, nkt // 4, quad, jnp.zeros((16, tq), F32))
        acc = lax.fori_loop((nkt // 4) * 4, nkt, single, acc)
        return jnp.sum(acc, axis=0, keepdims=True).astype(I32)

    def count32(cand):
        def single(kt, acc):
            hit = jnp.where(key_sc[kt] >= cand, 1, 0).astype(I32)
            return acc + jnp.sum(hit.reshape(tk // 8, 8, tq), axis=0)

        def quad(kq, acc):
            for j in range(4):
                acc = single(4 * kq + j, acc)
            return acc

        acc = lax.fori_loop(0, nkt // 4, quad, jnp.zeros((8, tq), I32))
        acc = lax.fori_loop((nkt // 4) * 4, nkt, single, acc)
        return jnp.sum(acc, axis=0, keepdims=True)

    c0 = count16(jnp.zeros((1, tq), I32))
    pos_ok = c0 >= topk
    def upper_bit(i, st):
        t, cnt = st
        cand = t + jnp.left_shift(jnp.int32(1), 14 - i)
        c = count16(cand)
        ok = c >= topk
        return jnp.where(ok, cand, t), jnp.where(ok, c, cnt)

    t16, c16 = lax.fori_loop(0, 15, upper_bit,
                             (jnp.where(pos_ok, 0, -32768).astype(I32),
                              jnp.where(pos_ok, c0, nkt * tk).astype(I32)))
    band = jnp.logical_and(t16 >= KEY16_BAND_LO, t16 <= KEY16_BAND_HI)
    t32 = jnp.left_shift(jnp.where(band, KEY16_BAND_LO, t16), 16)
    c32 = jnp.where(band, nkt * tk + 1, c16).astype(I32)
    any_band = jnp.max(band.astype(I32)) > 0
    top_bit = jnp.where(any_band, 24, 15)

    def lower_bit(i, st):
        t, cnt = st
        cand = t + jnp.left_shift(jnp.int32(1), top_bit - i)
        c = count32(cand)
        ok = c >= topk
        return jnp.where(ok, cand, t), jnp.where(ok, c, cnt)

    t32, c32 = lax.fori_loop(0, top_bit - 9, lower_bit, (t32, c32))
    thr, cnt = search(count32, 9, t32, c32)
    thr = jnp.maximum(thr, KEY_NEG_INF + 1)

    @pl.when(jnp.max(cnt) > topk)
    def _():
        need = topk - count32(thr + 1)
        tri = (lax.broadcasted_iota(I32, (tk, tk), 1)
               <= lax.broadcasted_iota(I32, (tk, tk), 0)).astype(BF16)

        def demote(kt, seen):
            keys = key_sc[kt]
            eq = keys == thr
            rank = seen + jnp.dot(tri, jnp.where(eq, 1.0, 0.0).astype(BF16),
                                  preferred_element_type=F32)
            key_sc[kt] = jnp.where(jnp.logical_and(eq, rank > need.astype(F32)), thr - 1, keys)
            return rank[tk - 1:tk, :]

        lax.fori_loop(0, nkt, demote, jnp.zeros((1, tq), F32))

    kv_w = 2 * HEAD_DIM
    shifts = []
    for g in range(N_KV_HEADS):
        parts = []
        for r in range(KV_REP):
            hd = g * KV_REP + r
            qht = q_ref[:, hd * HEAD_DIM:(hd + 1) * HEAD_DIM].astype(F32).T
            qs_sc[g, :HEAD_DIM, r * tq:(r + 1) * tq] = qht.astype(BF16)
            qn2 = jnp.sum(qht * qht, axis=0, keepdims=True)
            parts.append(jnp.sqrt(qn2 * kn_sm[g]) * SHIFT_MARGIN)
        shifts.append(jnp.concatenate(parts, axis=1))
    bound = jnp.maximum(jnp.max(shifts[0]), jnp.max(shifts[1]))
    fast = 2.0 * bound <= MAX_SPAN
    row0 = lax.broadcasted_iota(I32, (HEAD_DIM, KV_REP * tq), 0) == 0
    for g in range(N_KV_HEADS):
        shift_row = jnp.where(fast, -shifts[g], 0.0)
        qs_sc[g, HEAD_DIM:, :] = jnp.where(row0, shift_row, 0.0).astype(BF16)

    vrows = acc_sc.shape[1]

    def qk(kt, g):
        k0 = pl.multiple_of(kt * tk, tk)
        kg = k_ref[pl.ds(k0, tk), g * kv_w:(g + 1) * kv_w]
        s_sc[g] = jnp.dot(kg, qs_sc[g], preferred_element_type=F32)

    def softmax_shifted(g, bias):
        p_sc[g] = jnp.exp(s_sc[g] + bias).astype(BF16)

    def pv_shifted(kt, g):
        vtg = vt_ref[kt, g * vrows:(g + 1) * vrows, :]
        acc_sc[g] += jnp.dot(vtg, p_sc[g], preferred_element_type=F32)

    def softmax_online(g, bias):
        s = s_sc[g] + bias
        m_old = m_sc[g]
        m_new = jnp.maximum(m_old, jnp.max(s, axis=0, keepdims=True))
        al_sc[g] = jnp.exp(m_old - m_new)
        p_sc[g] = jnp.exp(s - m_new).astype(BF16)
        m_sc[g] = m_new

    def pv_online(kt, g):
        vtg = vt_ref[kt, g * vrows:(g + 1) * vrows, :]
        acc_sc[g] = al_sc[g] * acc_sc[g] + jnp.dot(vtg, p_sc[g], preferred_element_type=F32)

    def attend(softmax, pv, per_trip):
        m_sc[...] = jnp.full(m_sc.shape, NEG_BIG, F32)
        acc_sc[...] = jnp.zeros(acc_sc.shape, F32)
        p_sc[1] = jnp.zeros(p_sc.shape[1:], BF16)
        al_sc[1] = jnp.ones(al_sc.shape[1:], F32)

        def attn_tile(kt, carry):
            bias = jnp.where(key_sc[kt] >= thr, 0.0, NEG_BIG)
            bias = jnp.concatenate([bias] * KV_REP, axis=1)
            qk(kt, 1)
            softmax(0, bias)
            pv(jnp.maximum(kt - 1, 0), 1)
            qk(jnp.minimum(kt + 1, nkt - 1), 0)
            softmax(1, bias)
            pv(kt, 0)
            return carry

        def attn_group(kg, carry):
            for j in range(per_trip):
                attn_tile(per_trip * kg + j, carry)
            return carry

        qk(0, 0)
        lax.fori_loop(0, nkt // per_trip, attn_group, 0)
        lax.fori_loop((nkt // per_trip) * per_trip, nkt, attn_tile, 0)
        pv(nkt - 1, 1)

    @pl.when(fast)
    def _():
        attend(softmax_shifted, pv_shifted, 4)

    @pl.when(jnp.logical_not(fast))
    def _():
        attend(softmax_online, pv_online, 1)

    for g in range(N_KV_HEADS):
        out = acc_sc[g, :HEAD_DIM, :] / acc_sc[g, HEAD_DIM:HEAD_DIM + 1, :]
        for r in range(KV_REP):
            hd = g * KV_REP + r
            o_ref[:, hd * HEAD_DIM:(hd + 1) * HEAD_DIM] = out[:, r * tq:(r + 1) * tq].T.astype(o_ref.dtype)


def _dsa_call(q, qi, wv, k, vt, ki, topk):
    bsz, seq, dq = q.shape
    tq = DSA_TQ
    nkt, dv, tk = vt.shape[1:]
    nq = seq // tq
    qmap = lambda b, i: (b, i, 0)

    def whole(shape):
        return pl.BlockSpec((None,) + tuple(shape), lambda b, i: (b,) + (0,) * len(shape),
                            pipeline_mode=pl.Buffered(1))

    return pl.pallas_call(
        functools.partial(_dsa_kernel, topk=topk),
        grid=(bsz, nq),
        in_specs=[pl.BlockSpec((None, tq, dq), qmap),
                  pl.BlockSpec((None, tq, qi.shape[2]), qmap),
                  pl.BlockSpec((None, tq, wv.shape[2]), qmap),
                  whole(k.shape[1:]), whole(vt.shape[1:]), whole(ki.shape[1:])],
        out_specs=pl.BlockSpec((None, tq, dq), qmap),
        out_shape=jax.ShapeDtypeStruct((bsz, seq, dq), BF16),
        scratch_shapes=[pltpu.VMEM((nkt, tk, tq), I32),
                        pltpu.VMEM((nkt, tk, tq), BF16),
                        pltpu.VMEM((N_IDX_HEADS, LANES, tq), BF16),
                        pltpu.VMEM((LANES, tq), F32),
                        pltpu.VMEM((N_KV_HEADS, 2 * HEAD_DIM, KV_REP * tq), BF16),
                        pltpu.VMEM((N_KV_HEADS, 1, KV_REP * tq), F32),
                        pltpu.VMEM((N_KV_HEADS, dv // N_KV_HEADS, KV_REP * tq), F32),
                        pltpu.VMEM((N_KV_HEADS, tk, KV_REP * tq), F32),
                        pltpu.VMEM((N_KV_HEADS, tk, KV_REP * tq), BF16),
                        pltpu.VMEM((N_KV_HEADS, 1, KV_REP * tq), F32),
                        pltpu.SMEM((N_KV_HEADS,), F32)],
        compiler_params=_cparams(("parallel", "arbitrary")),
        name="dsa",
    )(q, qi, wv, k, vt, ki)


def _outproj_kernel(x_ref, ys_ref, ya_ref, gt_ref, sc_ref, sh_ref, wglu_ref, bglu_ref,
                    gs_ref, ga_ref, wout_ref, gm_ref, x1_out, h2_out):
    def rms(t, g):
        return t * lax.rsqrt(jnp.mean(t * t, axis=-1, keepdims=True) + EPS) * g

    y = ys_ref[...]
    ya = 0.5 * y * (1.0 + jnp.tanh(math.sqrt(2.0 / math.pi) * (y + 0.044715 * (y * y * y))))
    z = jnp.dot(ya.astype(BF16), wglu_ref[...], preferred_element_type=F32) + bglu_ref[...]
    o = ya / (1.0 + jnp.exp(-z))
    n1 = rms(o, gs_ref[...]).astype(BF16)
    n2 = rms(ya_ref[...].astype(F32), ga_ref[...]).astype(BF16)
    d1 = n1.shape[1]
    mixw = (jnp.dot(n1, wout_ref[:d1, :], preferred_element_type=F32)
            + jnp.dot(n2, wout_ref[d1:, :], preferred_element_type=F32))
    x1 = x_ref[...] + gt_ref[...] * mixw
    x1_out[...] = x1
    h2 = rms(x1, gm_ref[...]) * (1.0 + sc_ref[...]) + sh_ref[...]
    h2_out[...] = h2.astype(BF16)


def _outproj_call(x2, ys, ya, gt, sc, sh, wglu, bglu, gs, ga, wout, gm, seq):
    n, d = x2.shape
    tm = OUT_TM
    per_b = seq // tm
    row = lambda i: (i, 0)
    modmap = lambda i: (i // per_b, 0, 0)
    return pl.pallas_call(
        _outproj_kernel,
        grid=(n // tm,),
        in_specs=[pl.BlockSpec((tm, d), row),
                  pl.BlockSpec((tm, ys.shape[1]), row),
                  pl.BlockSpec((tm, ya.shape[1]), row),
                  pl.BlockSpec((None, 1, d), modmap),
                  pl.BlockSpec((None, 1, d), modmap),
                  pl.BlockSpec((None, 1, d), modmap),
                  _const_spec(wglu.shape), _const_spec(bglu.shape),
                  _const_spec(gs.shape), _const_spec(ga.shape),
                  _const_spec(wout.shape), _const_spec(gm.shape)],
        out_specs=[pl.BlockSpec((tm, d), row), pl.BlockSpec((tm, d), row)],
        out_shape=[jax.ShapeDtypeStruct((n, d), F32), jax.ShapeDtypeStruct((n, d), BF16)],
        compiler_params=_cparams(("parallel",)),
        name="outproj",
    )(x2, ys, ya, gt, sc, sh, wglu, bglu, gs, ga, wout, gm)


def _mlp_kernel(h_ref, x1_hbm, gt_ref, w1_ref, w2_ref, o_ref, x1_buf, x1_sem):
    i = pl.program_id(0)
    j = pl.program_id(1)
    tm = o_ref.shape[0]

    def x1_copy():
        rows = pl.ds(pl.multiple_of(i * tm, tm), tm)
        return pltpu.make_async_copy(x1_hbm.at[rows, :], x1_buf, x1_sem)

    @pl.when(j == 0)
    def _():
        x1_copy().start()
        o_ref[...] = jnp.zeros(o_ref.shape, F32)

    a = jnp.maximum(jnp.dot(h_ref[...], w1_ref[...], preferred_element_type=F32), 0.0)
    o_ref[...] += jnp.dot((a * a).astype(BF16), w2_ref[...], preferred_element_type=F32)

    @pl.when(j == pl.num_programs(1) - 1)
    def _():
        x1_copy().wait()
        o_ref[...] = x1_buf[...] + gt_ref[...] * o_ref[...]


def _mlp_call(h2, x1, gt, w1, w2, seq):
    n, d = x1.shape
    dff = w1.shape[1]
    tm, tf = MLP_TM, MLP_TF
    per_b = seq // tm
    return pl.pallas_call(
        _mlp_kernel,
        grid=(n // tm, dff // tf),
        in_specs=[pl.BlockSpec((tm, d), lambda i, j: (i, 0)),
                  pl.BlockSpec(memory_space=pl.ANY),
                  pl.BlockSpec((None, 1, d), lambda i, j: (i // per_b, 0, 0)),
                  pl.BlockSpec((d, tf), lambda i, j: (0, j)),
                  pl.BlockSpec((tf, d), lambda i, j: (j, 0))],
        out_specs=pl.BlockSpec((tm, d), lambda i, j: (i, 0)),
        out_shape=jax.ShapeDtypeStruct((n, d), F32),
        scratch_shapes=[pltpu.VMEM((tm, d), F32), pltpu.SemaphoreType.DMA(())],
        compiler_params=_cparams(("parallel", "arbitrary")),
        name="mlp",
    )(h2, x1, gt, w1, w2)


def _rope_tables():
    def tables(head_dim, per_vreg):
        r = head_dim // ROPE_FRAC
        half = r // 2
        inv = ROPE_THETA ** (-jnp.arange(half, dtype=F32) / half)
        lane = np.arange(LANES) % head_dim
        inv_l = jnp.where(lane < r, inv[lane % half], 0.0).astype(F32)[None, :]
        lo = jnp.asarray((lane < half).astype(np.float32))[None, :]
        hi = jnp.asarray(((lane >= half) & (lane < r)).astype(np.float32))[None, :]
        return inv_l, lo, hi

    inv_qk, lo_qk, hi_qk = tables(HEAD_DIM, 1)
    inv_ix, lo_ix, hi_ix = tables(IDX_DIM, 2)
    return [inv_qk, inv_ix, lo_qk, hi_qk, lo_ix, hi_ix]


def _block_diag(m):
    nb, g, a, b = m.shape
    eye = jnp.eye(g, dtype=m.dtype)
    return jnp.einsum('ngab,gk->ngakb', m, eye).reshape(nb, g * a, g * b)


def kernel(x, c, positions, w_ada, b_ada, g_norm_mix, w_in, lam_re, lam_im, log_dt,
           b_re, b_im, c_re, c_im, d_skip, w_glu, b_glu, g_q, g_k, g_out_ssm, g_out_attn,
           w_out, g_norm_mlp, w_mlp_in, w_mlp_out):
    bsz, seq, d = x.shape
    depth = w_ada.shape[0]
    n = bsz * seq
    topk = min(TOPK_MAX, seq // 4)
    gg, pp, hh = N_SSM_GROUPS, SSM_STATE, SSM_GROUP
    nblk = gg // S5_GB
    chunk = S5_CHUNK
    nchunks = seq // chunk

    c_pad = jnp.zeros((8, d), F32).at[:bsz].set(c.astype(F32))
    pos2 = positions.reshape(n, 1).astype(I32)
    tabs = _rope_tables()
    xcur = x.reshape(n, d).astype(F32)

    for l in range(depth):
        mod = _ada_call(c_pad, w_ada[l], b_ada[l][None, :])[:bsz]
        mod = mod.reshape(bsz, N_MOD, 1, d)
        sh_a, sc_a, gt_a, sh_m, sc_m, gt_m = [mod[:, i] for i in range(N_MOD)]

        wi = w_in[l]
        o0 = D_SSM
        o1 = o0 + N_HEADS * HEAD_DIM
        o2 = o1 + 2 * N_KV_HEADS * HEAD_DIM
        o3 = o2 + N_IDX_HEADS * IDX_DIM
        o4 = o3 + IDX_DIM
        w_ki = wi[:, o3:o4]
        w_wi = wi[:, o4:]
        w_kw = jnp.concatenate(
            [w_ki, w_ki, w_wi, jnp.zeros((d, 2 * LANES - 2 * IDX_DIM - N_IDX_HEADS), wi.dtype)], axis=1)
        u, q, k, v, qi, ki, wv = _inproj_call(
            xcur, sc_a, sh_a, g_norm_mix[l][None, :], pos2, tabs,
            g_q[l][None, :], g_k[l][None, :],
            wi[:, :o0].astype(BF16), wi[:, o0:o1].astype(BF16), wi[:, o1:o2].astype(BF16),
            wi[:, o2:o3].astype(BF16), w_kw.astype(BF16), seq)

        lr = lam_re[l].astype(F32).reshape(1, gg * pp)
        li = lam_im[l].astype(F32).reshape(1, gg * pp)
        dt = jnp.broadcast_to(jnp.exp(log_dt[l].astype(F32))[:, None], (gg, pp)).reshape(1, gg * pp)
        bre_t = b_re[l].astype(F32).transpose(2, 0, 1).reshape(hh, gg * pp)
        bim_t = b_im[l].astype(F32).transpose(2, 0, 1).reshape(hh, gg * pp)
        pw_r, pw_i, bb_r, bb_i = _s5prep_call(lr, li, dt, bre_t, bim_t, chunk)
        bbr = bb_r.reshape(hh, nblk, S5_GB, pp).transpose(1, 2, 0, 3)
        bbi = bb_i.reshape(hh, nblk, S5_GB, pp).transpose(1, 2, 0, 3)
        bw = jnp.concatenate([_block_diag(bbr), _block_diag(bbi)], axis=2).astype(BF16)
        cr = c_re[l].astype(F32).reshape(nblk, S5_GB, hh, pp).transpose(0, 1, 3, 2)
        ci = c_im[l].astype(F32).reshape(nblk, S5_GB, hh, pp).transpose(0, 1, 3, 2)
        cw = jnp.concatenate([_block_diag(cr), -_block_diag(ci)], axis=1).astype(BF16)
        ns = S5_GB * pp
        pw = jnp.concatenate([pw_r.reshape(chunk, nblk, ns), pw_i.reshape(chunk, nblk, ns)],
                             axis=2).transpose(1, 0, 2)
        u_t = u.reshape(bsz, nchunks, chunk, D_SSM).transpose(0, 2, 1, 3)
        y_t = _s5_call(u_t, bw, cw, pw, d_skip[l].astype(F32)[None, :])
        y_ssm = y_t.transpose(0, 2, 1, 3).reshape(n, D_SSM)

        v_t = v.reshape(bsz, seq // DSA_TK, v.shape[1], DSA_TK)
        y_attn = _dsa_call(q.reshape(bsz, seq, -1), qi.reshape(bsz, seq, -1),
                           wv.reshape(bsz, seq, -1), k.reshape(bsz, seq, -1),
                           v_t, ki.reshape(bsz, seq, -1), topk)
        y_attn = y_attn.reshape(n, -1)

        x1, h2 = _outproj_call(
            xcur, y_ssm, y_attn, gt_a, sc_m, sh_m,
            w_glu[l].astype(BF16), b_glu[l].astype(F32)[None, :],
            g_out_ssm[l].astype(F32)[None, :], g_out_attn[l].astype(F32)[None, :],
            w_out[l].astype(BF16), g_norm_mlp[l].astype(F32)[None, :], seq)

        xcur = _mlp_call(h2, x1, gt_m, w_mlp_in[l].astype(BF16), w_mlp_out[l].astype(BF16), seq)

    return xcur.reshape(bsz, seq, d).astype(x.dtype)
```

```python
import functools
import math

import numpy as np
import jax
import jax.numpy as jnp
from jax import lax
from jax.experimental import pallas as pl
from jax.experimental.pallas import tpu as pltpu

F32 = jnp.float32
BF16 = jnp.bfloat16
I32 = jnp.int32

D_MODEL = 2048
D_SSM = 1024
SSM_GROUP = 16
N_SSM_GROUPS = 64
SSM_STATE = 64
HEAD_DIM = 128
N_HEADS = 8
N_KV_HEADS = 2
KV_REP = N_HEADS // N_KV_HEADS
N_IDX_HEADS = 16
IDX_DIM = 64
TOPK_MAX = 256
ROPE_FRAC = 4
ROPE_THETA = 500000.0
D_FF = 4 * D_MODEL
EPS = 1e-6
N_MOD = 6

LANES = 128
VMEM_LIMIT = 56 * 1024 * 1024

ADA_TN = 1024
INPROJ_TM = 512
S5_CHUNK = 32
S5_GB = 16
DSA_TQ = 256
DSA_TK = 256
DSA_ONES = 16
OUT_TM = 256
MLP_TM = 1024
MLP_TF = 512

NEG_BIG = -1e30
INT_MIN = -2147483648
KEY_NEG_INF = -2139095041
KEY16_NEG_INF = KEY_NEG_INF >> 16
KEY16_BAND_LO = -129
KEY16_BAND_HI = 127
SHIFT_MARGIN = 1.01
MAX_SPAN = 80.0


def _cparams(sem):
    return pltpu.CompilerParams(dimension_semantics=sem, vmem_limit_bytes=VMEM_LIMIT)


def _const_spec(shape):
    nd = len(shape)
    return pl.BlockSpec(shape, lambda *_: (0,) * nd, pipeline_mode=pl.Buffered(1))


def _ada_kernel(c_ref, w_ref, b_ref, o_ref):
    c = c_ref[...]
    act = c / (1.0 + jnp.exp(-c))
    o_ref[...] = jnp.dot(act.astype(BF16), w_ref[...].astype(BF16),
                         preferred_element_type=F32) + b_ref[...]


def _ada_call(c_pad, w_ada, b_ada):
    rows, d = c_pad.shape
    n = w_ada.shape[1]
    return pl.pallas_call(
        _ada_kernel,
        grid=(n // ADA_TN,),
        in_specs=[pl.BlockSpec((rows, d), lambda j: (0, 0)),
                  pl.BlockSpec((d, ADA_TN), lambda j: (0, j)),
                  pl.BlockSpec((1, ADA_TN), lambda j: (0, j))],
        out_specs=pl.BlockSpec((rows, ADA_TN), lambda j: (0, j)),
        out_shape=jax.ShapeDtypeStruct((rows, n), F32),
        compiler_params=_cparams(("arbitrary",)),
        name="ada",
    )(c_pad, w_ada, b_ada)


def _rope(x, cos, sin_lo, sin_hi, half):
    n = x.shape[-1]
    return (x * cos + pltpu.roll(x, n - half, 1) * sin_lo + pltpu.roll(x, half, 1) * sin_hi)


def _inproj_kernel(x_ref, sc_ref, sh_ref, g_ref, pos_ref, inv_qk_ref, inv_ix_ref,
                   mlo_qk_ref, mhi_qk_ref, mlo_ix_ref, mhi_ix_ref, gq_ref, gk_ref,
                   wu_ref, wq_ref, wkv_ref, wqi_ref, wkw_ref,
                   u_out, q_out, k_out, v_out, qi_out, ki_out, wv_out):
    x = x_ref[...]
    ms = jnp.mean(x * x, axis=-1, keepdims=True)
    h = x * lax.rsqrt(ms + EPS) * g_ref[...]
    h = h * (1.0 + sc_ref[...]) + sh_ref[...]
    hb = h.astype(BF16)

    u_out[...] = jnp.dot(hb, wu_ref[...], preferred_element_type=F32).astype(BF16)

    pos = pos_ref[...].astype(F32)
    ang = pos * inv_qk_ref[...]
    cos_qk = jnp.cos(ang)
    sin_qk = jnp.sin(ang)
    slo_qk = -sin_qk * mlo_qk_ref[...]
    shi_qk = sin_qk * mhi_qk_ref[...]
    ang = pos * inv_ix_ref[...]
    cos_ix = jnp.cos(ang)
    sin_ix = jnp.sin(ang)
    slo_ix = -sin_ix * mlo_ix_ref[...]
    shi_ix = sin_ix * mhi_ix_ref[...]
    half_qk = HEAD_DIM // ROPE_FRAC // 2
    half_ix = IDX_DIM // ROPE_FRAC // 2

    def head_norm(t, g):
        return t * lax.rsqrt(jnp.mean(t * t, axis=-1, keepdims=True) + EPS) * g

    q = jnp.dot(hb, wq_ref[...], preferred_element_type=F32)
    attn_scale = HEAD_DIM ** -0.5
    for hd in range(N_HEADS):
        sl = slice(hd * HEAD_DIM, (hd + 1) * HEAD_DIM)
        t = _rope(head_norm(q[:, sl], gq_ref[...]), cos_qk, slo_qk, shi_qk, half_qk)
        q_out[:, sl] = (t * attn_scale).astype(BF16)

    kv = jnp.dot(hb, wkv_ref[...], preferred_element_type=F32)
    nk = N_KV_HEADS * HEAD_DIM
    one_col = (lax.broadcasted_iota(I32, (x.shape[0], HEAD_DIM), 1) == 0).astype(BF16)
    for hd in range(N_KV_HEADS):
        sl = slice(hd * HEAD_DIM, (hd + 1) * HEAD_DIM)
        t = _rope(head_norm(kv[:, sl], gk_ref[...]), cos_qk, slo_qk, shi_qk, half_qk)
        k_out[:, 2 * hd * HEAD_DIM:(2 * hd + 1) * HEAD_DIM] = t.astype(BF16)
        k_out[:, (2 * hd + 1) * HEAD_DIM:(2 * hd + 2) * HEAD_DIM] = one_col
    vrows = HEAD_DIM + DSA_ONES
    ones = jnp.ones((DSA_ONES, DSA_TK), BF16)
    for t in range(v_out.shape[0]):
        for hd in range(N_KV_HEADS):
            vh = kv[t * DSA_TK:(t + 1) * DSA_TK, nk + hd * HEAD_DIM:nk + (hd + 1) * HEAD_DIM]
            v_out[t, hd * vrows:hd * vrows + HEAD_DIM, :] = vh.T.astype(BF16)
            v_out[t, hd * vrows + HEAD_DIM:(hd + 1) * vrows, :] = ones

    qi = jnp.dot(hb, wqi_ref[...], preferred_element_type=F32)
    idx_scale = IDX_DIM ** -0.5
    for pr in range(N_IDX_HEADS * IDX_DIM // LANES):
        sl = slice(pr * LANES, (pr + 1) * LANES)
        t = _rope(qi[:, sl], cos_ix, slo_ix, shi_ix, half_ix)
        qi_out[:, sl] = (t * idx_scale).astype(BF16)

    kw = jnp.dot(hb, wkw_ref[...], preferred_element_type=F32)
    ki_out[...] = _rope(kw[:, :LANES], cos_ix, slo_ix, shi_ix, half_ix).astype(BF16)
    wv_out[...] = kw[:, LANES:] * (N_IDX_HEADS ** -0.5)


def _inproj_call(x2, sc, sh, g, pos2, tabs, gq, gk, wu, wq, wkv, wqi, wkw, seq):
    n, d = x2.shape
    tm = INPROJ_TM
    per_b = seq // tm
    row = lambda i: (i, 0)
    modmap = lambda i: (i // per_b, 0, 0)
    small = [_const_spec(t.shape) for t in tabs]
    in_specs = ([pl.BlockSpec((tm, d), row),
                 pl.BlockSpec((None, 1, d), modmap),
                 pl.BlockSpec((None, 1, d), modmap),
                 _const_spec(g.shape),
                 pl.BlockSpec((tm, 1), row)]
                + small
                + [_const_spec(gq.shape), _const_spec(gk.shape),
                   _const_spec(wu.shape), _const_spec(wq.shape), _const_spec(wkv.shape),
                   _const_spec(wqi.shape), _const_spec(wkw.shape)])
    widths = (D_SSM, N_HEADS * HEAD_DIM, 2 * N_KV_HEADS * HEAD_DIM, N_KV_HEADS * HEAD_DIM,
              N_IDX_HEADS * IDX_DIM, LANES, LANES)
    dtypes = (BF16, BF16, BF16, BF16, BF16, BF16, F32)
    vt_rows = N_KV_HEADS * (HEAD_DIM + DSA_ONES)
    out_specs = [pl.BlockSpec((tm, w), row) for w in widths]
    out_shape = [jax.ShapeDtypeStruct((n, w), dt) for w, dt in zip(widths, dtypes)]
    out_specs[3] = pl.BlockSpec((tm // DSA_TK, vt_rows, DSA_TK), lambda i: (i, 0, 0))
    out_shape[3] = jax.ShapeDtypeStruct((n // DSA_TK, vt_rows, DSA_TK), BF16)
    return pl.pallas_call(
        _inproj_kernel,
        grid=(n // tm,),
        in_specs=in_specs,
        out_specs=out_specs,
        out_shape=out_shape,
        compiler_params=_cparams(("parallel",)),
        name="inproj",
    )(x2, sc, sh, g, pos2, *tabs, gq, gk, wu, wq, wkv, wqi, wkw)


def _s5prep_kernel(lr_ref, li_ref, dt_ref, bre_ref, bim_ref, pr_ref, pi_ref, bbr_ref, bbi_ref):
    lr = jnp.minimum(lr_ref[...], -1e-4)
    li = li_ref[...]
    dt = dt_ref[...]
    mag = jnp.exp(lr * dt)
    ab_r = mag * jnp.cos(li * dt)
    ab_i = mag * jnp.sin(li * dt)
    den = lr * lr + li * li
    fr = ((ab_r - 1.0) * lr + ab_i * li) / den
    fi = (ab_i * lr - (ab_r - 1.0) * li) / den
    bbr_ref[...] = fr * bre_ref[...] - fi * bim_ref[...]
    bbi_ref[...] = fr * bim_ref[...] + fi * bre_ref[...]
    steps = (lax.broadcasted_iota(I32, pr_ref.shape, 0) + 1).astype(F32)
    magn = jnp.exp(steps * (lr * dt))
    pr_ref[...] = magn * jnp.cos(steps * (li * dt))
    pi_ref[...] = magn * jnp.sin(steps * (li * dt))


def _s5prep_call(lr, li, dt, bre_t, bim_t, chunk):
    gp = lr.shape[1]
    hh = bre_t.shape[0]
    outs = [jax.ShapeDtypeStruct((chunk, gp), F32)] * 2 + [jax.ShapeDtypeStruct((hh, gp), F32)] * 2
    return pl.pallas_call(_s5prep_kernel, out_shape=outs, name="s5prep")(lr, li, dt, bre_t, bim_t)


def _s5_kernel(u_ref, bw_ref, cw_ref, pw_ref, dsk_ref, y_ref, st_ref, cin_ref, xs_ref):
    tt, rr, _ = u_ref.shape
    ns = st_ref.shape[1] // 2
    ar = pw_ref[0:1, :ns]
    ai = pw_ref[0:1, ns:]
    st_ref[...] = jnp.zeros(st_ref.shape, F32)

    def cproj(zr, zi):
        return (jnp.dot(zr.astype(BF16), cw_ref[:ns, :], preferred_element_type=F32)
                + jnp.dot(zi.astype(BF16), cw_ref[ns:, :], preferred_element_type=F32))

    xs_ref[0] = jnp.dot(u_ref[0], bw_ref[...], preferred_element_type=F32)

    def local_step(i, carry):
        x = xs_ref[i % 2]
        nxt = jnp.minimum(i + 1, tt - 1)
        xs_ref[(i + 1) % 2] = jnp.dot(u_ref[nxt], bw_ref[...], preferred_element_type=F32)
        sr = st_ref[:, :ns]
        si = st_ref[:, ns:]
        nr = ar * sr - ai * si + x[:, :ns]
        ni = ar * si + ai * sr + x[:, ns:]
        st_ref[:, :ns] = nr
        st_ref[:, ns:] = ni
        y_ref[i] = cproj(nr, ni) + dsk_ref[...] * u_ref[i].astype(F32)
        return carry

    lax.fori_loop(0, tt, local_step, 0)

    atr = pw_ref[tt - 1:tt, :ns]
    ati = pw_ref[tt - 1:tt, ns:]

    def chain(r, carry):
        cr, ci = carry
        cin_ref[pl.ds(r, 1), :ns] = cr
        cin_ref[pl.ds(r, 1), ns:] = ci
        er = st_ref[pl.ds(r, 1), :ns]
        ei = st_ref[pl.ds(r, 1), ns:]
        return (atr * cr - ati * ci + er, atr * ci + ati * cr + ei)

    zero = jnp.zeros((1, ns), F32)
    lax.fori_loop(0, rr, chain, (zero, zero))

    def carry_step(i, carry):
        pr = pw_ref[pl.ds(i, 1), :ns]
        pi = pw_ref[pl.ds(i, 1), ns:]
        cr = cin_ref[:, :ns]
        ci = cin_ref[:, ns:]
        y_ref[i] += cproj(pr * cr - pi * ci, pr * ci + pi * cr)
        return carry

    lax.fori_loop(0, tt, carry_step, 0)


def _s5_call(u_t, bw, cw, pw, dsk):
    bsz, tt, rr, dch = u_t.shape
    nb = bw.shape[0]
    cb = bw.shape[1]
    ns2 = bw.shape[2]
    return pl.pallas_call(
        _s5_kernel,
        grid=(bsz, nb),
        in_specs=[pl.BlockSpec((None, tt, rr, cb), lambda b, j: (b, 0, 0, j)),
                  pl.BlockSpec((None, cb, ns2), lambda b, j: (j, 0, 0)),
                  pl.BlockSpec((None, ns2, cb), lambda b, j: (j, 0, 0)),
                  pl.BlockSpec((None, tt, ns2), lambda b, j: (j, 0, 0)),
                  pl.BlockSpec((1, cb), lambda b, j: (0, j))],
        out_specs=pl.BlockSpec((None, tt, rr, cb), lambda b, j: (b, 0, 0, j)),
        out_shape=jax.ShapeDtypeStruct((bsz, tt, rr, dch), F32),
        scratch_shapes=[pltpu.VMEM((rr, ns2), F32), pltpu.VMEM((rr, ns2), F32),
                        pltpu.VMEM((2, rr, ns2), F32)],
        compiler_params=_cparams(("parallel", "parallel")),
        name="s5scan",
    )(u_t, bw, cw, pw, dsk)


def _order_key(x):
    bits = pltpu.bitcast(x, I32)
    return jnp.where(bits < 0, bits ^ jnp.int32(0x7FFFFFFF), bits)


def _dsa_kernel(q_ref, qi_ref, wv_ref, k_ref, vt_ref, ki_ref, o_ref,
                key_sc, kb_sc, lhs_sc, wt_sc, qs_sc, m_sc, acc_sc, s_sc, p_sc, al_sc, kn_sm,
                *, topk):
    tq = q_ref.shape[0]
    tk = key_sc.shape[1]
    qb = pl.program_id(1)
    nkt = (qb * tq + tq + tk - 1) // tk

    lane = lax.broadcasted_iota(I32, (tq, LANES), 1)
    kiota = lax.broadcasted_iota(I32, (tk, tq), 0)
    qpos = qb * tq + lax.broadcasted_iota(I32, (tk, tq), 1)

    @pl.when(qb == 0)
    def _():
        rows = min(k_ref.shape[0], 1024)
        for g in range(N_KV_HEADS):
            def norm_max(c, best):
                r0 = pl.multiple_of(c * rows, rows)
                kk = k_ref[pl.ds(r0, rows), g * 2 * HEAD_DIM:g * 2 * HEAD_DIM + HEAD_DIM].astype(F32)
                return jnp.maximum(best, jnp.max(jnp.sum(kk * kk, axis=1, keepdims=True)))

            kn_sm[g] = lax.fori_loop(0, k_ref.shape[0] // rows, norm_max, jnp.float32(0.0))

    for hd in range(N_IDX_HEADS):
        pair = qi_ref[:, (hd // 2) * LANES:(hd // 2 + 1) * LANES].astype(F32)
        keep = (lane < IDX_DIM) if hd % 2 == 0 else (lane >= IDX_DIM)
        lhs_sc[hd] = jnp.where(keep, pair, 0.0).T.astype(BF16)
    wt_sc[...] = wv_ref[...].T

    def score_tiles(kt, ntile):
        k0 = pl.multiple_of(kt * tk, tk)
        kit = ki_ref[pl.ds(k0, ntile * tk), :]
        acc = jnp.zeros((ntile * tk, tq), F32)
        for hd in range(N_IDX_HEADS):
            lg = jnp.dot(kit, lhs_sc[hd], preferred_element_type=F32)
            acc = acc + jnp.maximum(lg, 0.0) * wt_sc[hd:hd + 1, :]
        for j in range(ntile):
            a = jnp.where(kiota + (k0 + j * tk) <= qpos, acc[j * tk:(j + 1) * tk], -jnp.inf)
            bits = pltpu.bitcast(a, I32)
            key_sc[kt + j] = jnp.where(bits < 0, bits ^ jnp.int32(0x7FFFFFFF), bits)
            kb_sc[kt + j] = pltpu.bitcast(bits & jnp.int32(-65536), F32).astype(BF16)

    def score_oct(ko, carry):
        score_tiles(8 * ko, 8)
        return carry

    lax.fori_loop(0, nkt // 8, score_oct, 0)

    @pl.when(nkt % 8 >= 4)
    def _():
        score_tiles((nkt // 8) * 8, 4)

    @pl.when(nkt % 4 >= 2)
    def _():
        score_tiles((nkt // 4) * 4, 2)

    @pl.when(nkt % 2 == 1)
    def _():
        score_tiles(nkt - 1, 1)

    def search(count, bit0, t0, n0):
        def cond(st):
            bit, _, cnt = st
            return jnp.logical_and(bit >= 0, jnp.max(jnp.abs(cnt - topk)) > 0)

        def body(st):
            bit, t, cnt = st
            for _ in range(2):
                cand = t + jnp.left_shift(jnp.int32(1), jnp.maximum(bit, 0))
                c = count(cand)
                ok = jnp.logical_and(c >= topk, bit >= 0)
                t = jnp.where(ok, cand, t)
                cnt = jnp.where(ok, c, cnt)
                bit = bit - 1
            return bit, t, cnt

        _, t, cnt = lax.while_loop(cond, body, (jnp.asarray(bit0, I32), t0, n0))
        return t, cnt

    one_b = jnp.ones((tk, tq), BF16)
    zero_b = jnp.zeros((tk, tq), BF16)
    sub = tk // 16

    def count16(c16):
        c16 = jnp.maximum(c16, KEY16_NEG_INF)
        pat = jnp.where(c16 >= 0, c16, c16 ^ jnp.int32(0x7FFF))
        cand = pltpu.bitcast(jnp.left_shift(pat, 16), F32).astype(BF16)

        def hits(kt):
            hit = jnp.where(kb_sc[kt] >= cand, one_b, zero_b)
            part = hit[0:16]
            for j in range(1, sub):
                part = part + hit[j * 16:(j + 1) * 16]
            return part

        def quad(kq, acc):
            part = hits(4 * kq)
            for j in range(1, 4):
                part = part + hits(4 * kq + j)
            return acc + part.astype(F32)

        def single(kt, acc):
            return acc + hits(kt).astype(F32)

        acc = lax.fori_loop(0, nkt // 4, quad, jnp.zeros((16, tq), F32))
        acc = lax.fori_loop((nkt // 4) * 4, nkt, single, acc)
        return jnp.sum(acc, axis=0, keepdims=True).astype(I32)

    def count32(cand):
        def single(kt, acc):
            hit = jnp.where(key_sc[kt] >= cand, 1, 0).astype(I32)
            return acc + jnp.sum(hit.reshape(tk // 8, 8, tq), axis=0)

        def quad(kq, acc):
            for j in range(4):
                acc = single(4 * kq + j, acc)
            return acc

        acc = lax.fori_loop(0, nkt // 4, quad, jnp.zeros((8, tq), I32))
        acc = lax.fori_loop((nkt // 4) * 4, nkt, single, acc)
        return jnp.sum(acc, axis=0, keepdims=True)

    c0 = count16(jnp.zeros((1, tq), I32))
    pos_ok = c0 >= topk
    def upper_bit(i, st):
        t, cnt = st
        cand = t + jnp.left_shift(jnp.int32(1), 14 - i)
        c = count16(cand)
        ok = c >= topk
        return jnp.where(ok, cand, t), jnp.where(ok, c, cnt)

    t16, c16 = lax.fori_loop(0, 15, upper_bit,
                             (jnp.where(pos_ok, 0, -32768).astype(I32),
                              jnp.where(pos_ok, c0, nkt * tk).astype(I32)))
    band = jnp.logical_and(t16 >= KEY16_BAND_LO, t16 <= KEY16_BAND_HI)
    t32 = jnp.left_shift(jnp.where(band, KEY16_BAND_LO, t16), 16)
    c32 = jnp.where(band, nkt * tk + 1, c16).astype(I32)
    any_band = jnp.max(band.astype(I32)) > 0
    top_bit = jnp.where(any_band, 24, 15)

    def lower_bit(i, st):
        t, cnt = st
        cand = t + jnp.left_shift(jnp.int32(1), top_bit - i)
        c = count32(cand)
        ok = c >= topk
        return jnp.where(ok, cand, t), jnp.where(ok, c, cnt)

    t32, c32 = lax.fori_loop(0, top_bit - 9, lower_bit, (t32, c32))
    thr, cnt = search(count32, 9, t32, c32)
    thr = jnp.maximum(thr, KEY_NEG_INF + 1)

    @pl.when(jnp.max(cnt) > topk)
    def _():
        need = topk - count32(thr + 1)
        tri = (lax.broadcasted_iota(I32, (tk, tk), 1)
               <= lax.broadcasted_iota(I32, (tk, tk), 0)).astype(BF16)

        def demote(kt, seen):
            keys = key_sc[kt]
            eq = keys == thr
            rank = seen + jnp.dot(tri, jnp.where(eq, 1.0, 0.0).astype(BF16),
                                  preferred_element_type=F32)
            key_sc[kt] = jnp.where(jnp.logical_and(eq, rank > need.astype(F32)), thr - 1, keys)
            return rank[tk - 1:tk, :]

        lax.fori_loop(0, nkt, demote, jnp.zeros((1, tq), F32))

    kv_w = 2 * HEAD_DIM
    shifts = []
    for g in range(N_KV_HEADS):
        parts = []
        for r in range(KV_REP):
            hd = g * KV_REP + r
            qht = q_ref[:, hd * HEAD_DIM:(hd + 1) * HEAD_DIM].astype(F32).T
            qs_sc[g, :HEAD_DIM, r * tq:(r + 1) * tq] = qht.astype(BF16)
            qn2 = jnp.sum(qht * qht, axis=0, keepdims=True)
            parts.append(jnp.sqrt(qn2 * kn_sm[g]) * SHIFT_MARGIN)
        shifts.append(jnp.concatenate(parts, axis=1))
    bound = jnp.maximum(jnp.max(shifts[0]), jnp.max(shifts[1]))
    fast = 2.0 * bound <= MAX_SPAN
    row0 = lax.broadcasted_iota(I32, (HEAD_DIM, KV_REP * tq), 0) == 0
    for g in range(N_KV_HEADS):
        shift_row = jnp.where(fast, -shifts[g], 0.0)
        qs_sc[g, HEAD_DIM:, :] = jnp.where(row0, shift_row, 0.0).astype(BF16)

    vrows = acc_sc.shape[1]

    def qk(kt, g):
        k0 = pl.multiple_of(kt * tk, tk)
        kg = k_ref[pl.ds(k0, tk), g * kv_w:(g + 1) * kv_w]
        s_sc[g] = jnp.dot(kg, qs_sc[g], preferred_element_type=F32)

    def softmax_shifted(g, bias):
        p_sc[g] = jnp.exp(s_sc[g] + bias).astype(BF16)

    def pv_shifted(kt, g):
        vtg = vt_ref[kt, g * vrows:(g + 1) * vrows, :]
        acc_sc[g] += jnp.dot(vtg, p_sc[g], preferred_element_type=F32)

    def softmax_online(g, bias):
        s = s_sc[g] + bias
        m_old = m_sc[g]
        m_new = jnp.maximum(m_old, jnp.max(s, axis=0, keepdims=True))
        al_sc[g] = jnp.exp(m_old - m_new)
        p_sc[g] = jnp.exp(s - m_new).astype(BF16)
        m_sc[g] = m_new

    def pv_online(kt, g):
        vtg = vt_ref[kt, g * vrows:(g + 1) * vrows, :]
        acc_sc[g] = al_sc[g] * acc_sc[g] + jnp.dot(vtg, p_sc[g], preferred_element_type=F32)

    def attend(softmax, pv, per_trip):
        m_sc[...] = jnp.full(m_sc.shape, NEG_BIG, F32)
        acc_sc[...] = jnp.zeros(acc_sc.shape, F32)
        p_sc[1] = jnp.zeros(p_sc.shape[1:], BF16)
        al_sc[1] = jnp.ones(al_sc.shape[1:], F32)

        def attn_tile(kt, carry):
            bias = jnp.where(key_sc[kt] >= thr, 0.0, NEG_BIG)
            bias = jnp.concatenate([bias] * KV_REP, axis=1)
            qk(kt, 1)
            softmax(0, bias)
            pv(jnp.maximum(kt - 1, 0), 1)
            qk(jnp.minimum(kt + 1, nkt - 1), 0)
            softmax(1, bias)
            pv(kt, 0)
            return carry

        def attn_group(kg, carry):
            for j in range(per_trip):
                attn_tile(per_trip * kg + j, carry)
            return carry

        qk(0, 0)
        lax.fori_loop(0, nkt // per_trip, attn_group, 0)
        done = (nkt // per_trip) * per_trip
        if per_trip == 4:
            @pl.when(nkt - done >= 2)
            def _():
                attn_tile(done, 0)
                attn_tile(done + 1, 0)

            @pl.when(nkt % 2 == 1)
            def _():
                attn_tile(nkt - 1, 0)
        else:
            lax.fori_loop(done, nkt, attn_tile, 0)
        pv(nkt - 1, 1)

    @pl.when(fast)
    def _():
        attend(softmax_shifted, pv_shifted, 4)

    @pl.when(jnp.logical_not(fast))
    def _():
        attend(softmax_online, pv_online, 1)

    for g in range(N_KV_HEADS):
        out = acc_sc[g, :HEAD_DIM, :] / acc_sc[g, HEAD_DIM:HEAD_DIM + 1, :]
        for r in range(KV_REP):
            hd = g * KV_REP + r
            o_ref[:, hd * HEAD_DIM:(hd + 1) * HEAD_DIM] = out[:, r * tq:(r + 1) * tq].T.astype(o_ref.dtype)


def _dsa_call(q, qi, wv, k, vt, ki, topk):
    bsz, seq, dq = q.shape
    tq = DSA_TQ
    nkt, dv, tk = vt.shape[1:]
    nq = seq // tq
    qmap = lambda b, i: (b, i, 0)

    def whole(shape):
        return pl.BlockSpec((None,) + tuple(shape), lambda b, i: (b,) + (0,) * len(shape),
                            pipeline_mode=pl.Buffered(1))

    return pl.pallas_call(
        functools.partial(_dsa_kernel, topk=topk),
        grid=(bsz, nq),
        in_specs=[pl.BlockSpec((None, tq, dq), qmap),
                  pl.BlockSpec((None, tq, qi.shape[2]), qmap),
                  pl.BlockSpec((None, tq, wv.shape[2]), qmap),
                  whole(k.shape[1:]), whole(vt.shape[1:]), whole(ki.shape[1:])],
        out_specs=pl.BlockSpec((None, tq, dq), qmap),
        out_shape=jax.ShapeDtypeStruct((bsz, seq, dq), BF16),
        scratch_shapes=[pltpu.VMEM((nkt, tk, tq), I32),
                        pltpu.VMEM((nkt, tk, tq), BF16),
                        pltpu.VMEM((N_IDX_HEADS, LANES, tq), BF16),
                        pltpu.VMEM((LANES, tq), F32),
                        pltpu.VMEM((N_KV_HEADS, 2 * HEAD_DIM, KV_REP * tq), BF16),
                        pltpu.VMEM((N_KV_HEADS, 1, KV_REP * tq), F32),
                        pltpu.VMEM((N_KV_HEADS, dv // N_KV_HEADS, KV_REP * tq), F32),
                        pltpu.VMEM((N_KV_HEADS, tk, KV_REP * tq), F32),
                        pltpu.VMEM((N_KV_HEADS, tk, KV_REP * tq), BF16),
                        pltpu.VMEM((N_KV_HEADS, 1, KV_REP * tq), F32),
                        pltpu.SMEM((N_KV_HEADS,), F32)],
        compiler_params=_cparams(("parallel", "arbitrary")),
        name="dsa",
    )(q, qi, wv, k, vt, ki)


def _outproj_kernel(x_ref, ys_ref, ya_ref, gt_ref, sc_ref, sh_ref, wglu_ref, bglu_ref,
                    gs_ref, ga_ref, wout_ref, gm_ref, x1_out, h2_out):
    def rms(t, g):
        return t * lax.rsqrt(jnp.mean(t * t, axis=-1, keepdims=True) + EPS) * g

    y = ys_ref[...]
    ya = 0.5 * y * (1.0 + jnp.tanh(math.sqrt(2.0 / math.pi) * (y + 0.044715 * (y * y * y))))
    z = jnp.dot(ya.astype(BF16), wglu_ref[...], preferred_element_type=F32) + bglu_ref[...]
    o = ya / (1.0 + jnp.exp(-z))
    n1 = rms(o, gs_ref[...]).astype(BF16)
    n2 = rms(ya_ref[...].astype(F32), ga_ref[...]).astype(BF16)
    d1 = n1.shape[1]
    mixw = (jnp.dot(n1, wout_ref[:d1, :], preferred_element_type=F32)
            + jnp.dot(n2, wout_ref[d1:, :], preferred_element_type=F32))
    x1 = x_ref[...] + gt_ref[...] * mixw
    x1_out[...] = x1
    h2 = rms(x1, gm_ref[...]) * (1.0 + sc_ref[...]) + sh_ref[...]
    h2_out[...] = h2.astype(BF16)


def _outproj_call(x2, ys, ya, gt, sc, sh, wglu, bglu, gs, ga, wout, gm, seq):
    n, d = x2.shape
    tm = OUT_TM
    per_b = seq // tm
    row = lambda i: (i, 0)
    modmap = lambda i: (i // per_b, 0, 0)
    return pl.pallas_call(
        _outproj_kernel,
        grid=(n // tm,),
        in_specs=[pl.BlockSpec((tm, d), row),
                  pl.BlockSpec((tm, ys.shape[1]), row),
                  pl.BlockSpec((tm, ya.shape[1]), row),
                  pl.BlockSpec((None, 1, d), modmap),
                  pl.BlockSpec((None, 1, d), modmap),
                  pl.BlockSpec((None, 1, d), modmap),
                  _const_spec(wglu.shape), _const_spec(bglu.shape),
                  _const_spec(gs.shape), _const_spec(ga.shape),
                  _const_spec(wout.shape), _const_spec(gm.shape)],
        out_specs=[pl.BlockSpec((tm, d), row), pl.BlockSpec((tm, d), row)],
        out_shape=[jax.ShapeDtypeStruct((n, d), F32), jax.ShapeDtypeStruct((n, d), BF16)],
        compiler_params=_cparams(("parallel",)),
        name="outproj",
    )(x2, ys, ya, gt, sc, sh, wglu, bglu, gs, ga, wout, gm)


def _mlp_kernel(h_ref, x1_hbm, gt_ref, w1_ref, w2_ref, o_ref, x1_buf, x1_sem):
    i = pl.program_id(0)
    j = pl.program_id(1)
    tm = o_ref.shape[0]

    def x1_copy():
        rows = pl.ds(pl.multiple_of(i * tm, tm), tm)
        return pltpu.make_async_copy(x1_hbm.at[rows, :], x1_buf, x1_sem)

    @pl.when(j == 0)
    def _():
        x1_copy().start()
        o_ref[...] = jnp.zeros(o_ref.shape, F32)

    a = jnp.maximum(jnp.dot(h_ref[...], w1_ref[...], preferred_element_type=F32), 0.0)
    o_ref[...] += jnp.dot((a * a).astype(BF16), w2_ref[...], preferred_element_type=F32)

    @pl.when(j == pl.num_programs(1) - 1)
    def _():
        x1_copy().wait()
        o_ref[...] = x1_buf[...] + gt_ref[...] * o_ref[...]


def _mlp_call(h2, x1, gt, w1, w2, seq):
    n, d = x1.shape
    dff = w1.shape[1]
    tm, tf = MLP_TM, MLP_TF
    per_b = seq // tm
    return pl.pallas_call(
        _mlp_kernel,
        grid=(n // tm, dff // tf),
        in_specs=[pl.BlockSpec((tm, d), lambda i, j: (i, 0)),
                  pl.BlockSpec(memory_space=pl.ANY),
                  pl.BlockSpec((None, 1, d), lambda i, j: (i // per_b, 0, 0)),
                  pl.BlockSpec((d, tf), lambda i, j: (0, j)),
                  pl.BlockSpec((tf, d), lambda i, j: (j, 0))],
        out_specs=pl.BlockSpec((tm, d), lambda i, j: (i, 0)),
        out_shape=jax.ShapeDtypeStruct((n, d), F32),
        scratch_shapes=[pltpu.VMEM((tm, d), F32), pltpu.SemaphoreType.DMA(())],
        compiler_params=_cparams(("parallel", "arbitrary")),
        name="mlp",
    )(h2, x1, gt, w1, w2)


def _rope_tables():
    def tables(head_dim, per_vreg):
        r = head_dim // ROPE_FRAC
        half = r // 2
        inv = ROPE_THETA ** (-jnp.arange(half, dtype=F32) / half)
        lane = np.arange(LANES) % head_dim
        inv_l = jnp.where(lane < r, inv[lane % half], 0.0).astype(F32)[None, :]
        lo = jnp.asarray((lane < half).astype(np.float32))[None, :]
        hi = jnp.asarray(((lane >= half) & (lane < r)).astype(np.float32))[None, :]
        return inv_l, lo, hi

    inv_qk, lo_qk, hi_qk = tables(HEAD_DIM, 1)
    inv_ix, lo_ix, hi_ix = tables(IDX_DIM, 2)
    return [inv_qk, inv_ix, lo_qk, hi_qk, lo_ix, hi_ix]


def _block_diag(m):
    nb, g, a, b = m.shape
    eye = jnp.eye(g, dtype=m.dtype)
    return jnp.einsum('ngab,gk->ngakb', m, eye).reshape(nb, g * a, g * b)


def kernel(x, c, positions, w_ada, b_ada, g_norm_mix, w_in, lam_re, lam_im, log_dt,
           b_re, b_im, c_re, c_im, d_skip, w_glu, b_glu, g_q, g_k, g_out_ssm, g_out_attn,
           w_out, g_norm_mlp, w_mlp_in, w_mlp_out):
    bsz, seq, d = x.shape
    depth = w_ada.shape[0]
    n = bsz * seq
    topk = min(TOPK_MAX, seq // 4)
    gg, pp, hh = N_SSM_GROUPS, SSM_STATE, SSM_GROUP
    nblk = gg // S5_GB
    chunk = S5_CHUNK
    nchunks = seq // chunk

    c_pad = jnp.zeros((8, d), F32).at[:bsz].set(c.astype(F32))
    pos2 = positions.reshape(n, 1).astype(I32)
    tabs = _rope_tables()
    xcur = x.reshape(n, d).astype(F32)

    for l in range(depth):
        mod = _ada_call(c_pad, w_ada[l], b_ada[l][None, :])[:bsz]
        mod = mod.reshape(bsz, N_MOD, 1, d)
        sh_a, sc_a, gt_a, sh_m, sc_m, gt_m = [mod[:, i] for i in range(N_MOD)]

        wi = w_in[l]
        o0 = D_SSM
        o1 = o0 + N_HEADS * HEAD_DIM
        o2 = o1 + 2 * N_KV_HEADS * HEAD_DIM
        o3 = o2 + N_IDX_HEADS * IDX_DIM
        o4 = o3 + IDX_DIM
        w_ki = wi[:, o3:o4]
        w_wi = wi[:, o4:]
        w_kw = jnp.concatenate(
            [w_ki, w_ki, w_wi, jnp.zeros((d, 2 * LANES - 2 * IDX_DIM - N_IDX_HEADS), wi.dtype)], axis=1)
        u, q, k, v, qi, ki, wv = _inproj_call(
            xcur, sc_a, sh_a, g_norm_mix[l][None, :], pos2, tabs,
            g_q[l][None, :], g_k[l][None, :],
            wi[:, :o0].astype(BF16), wi[:, o0:o1].astype(BF16), wi[:, o1:o2].astype(BF16),
            wi[:, o2:o3].astype(BF16), w_kw.astype(BF16), seq)

        lr = lam_re[l].astype(F32).reshape(1, gg * pp)
        li = lam_im[l].astype(F32).reshape(1, gg * pp)
        dt = jnp.broadcast_to(jnp.exp(log_dt[l].astype(F32))[:, None], (gg, pp)).reshape(1, gg * pp)
        bre_t = b_re[l].astype(F32).transpose(2, 0, 1).reshape(hh, gg * pp)
        bim_t = b_im[l].astype(F32).transpose(2, 0, 1).reshape(hh, gg * pp)
        pw_r, pw_i, bb_r, bb_i = _s5prep_call(lr, li, dt, bre_t, bim_t, chunk)
        bbr = bb_r.reshape(hh, nblk, S5_GB, pp).transpose(1, 2, 0, 3)
        bbi = bb_i.reshape(hh, nblk, S5_GB, pp).transpose(1, 2, 0, 3)
        bw = jnp.concatenate([_block_diag(bbr), _block_diag(bbi)], axis=2).astype(BF16)
        cr = c_re[l].astype(F32).reshape(nblk, S5_GB, hh, pp).transpose(0, 1, 3, 2)
        ci = c_im[l].astype(F32).reshape(nblk, S5_GB, hh, pp).transpose(0, 1, 3, 2)
        cw = jnp.concatenate([_block_diag(cr), -_block_diag(ci)], axis=1).astype(BF16)
        ns = S5_GB * pp
        pw = jnp.concatenate([pw_r.reshape(chunk, nblk, ns), pw_i.reshape(chunk, nblk, ns)],
                             axis=2).transpose(1, 0, 2)
        u_t = u.reshape(bsz, nchunks, chunk, D_SSM).transpose(0, 2, 1, 3)
        y_t = _s5_call(u_t, bw, cw, pw, d_skip[l].astype(F32)[None, :])
        y_ssm = y_t.transpose(0, 2, 1, 3).reshape(n, D_SSM)

        v_t = v.reshape(bsz, seq // DSA_TK, v.shape[1], DSA_TK)
        y_attn = _dsa_call(q.reshape(bsz, seq, -1), qi.reshape(bsz, seq, -1),
                           wv.reshape(bsz, seq, -1), k.reshape(bsz, seq, -1),
                           v_t, ki.reshape(bsz, seq, -1), topk)
        y_attn = y_attn.reshape(n, -1)

        x1, h2 = _outproj_call(
            xcur, y_ssm, y_attn, gt_a, sc_m, sh_m,
            w_glu[l].astype(BF16), b_glu[l].astype(F32)[None, :],
            g_out_ssm[l].astype(F32)[None, :], g_out_attn[l].astype(F32)[None, :],
            w_out[l].astype(BF16), g_norm_mlp[l].astype(F32)[None, :], seq)

        xcur = _mlp_call(h2, x1, gt_m, w_mlp_in[l].astype(BF16), w_mlp_out[l].astype(BF16), seq)

    return xcur.reshape(bsz, seq, d).astype(x.dtype)
```

```python
import functools
import math

import numpy as np
import jax
import jax.numpy as jnp
from jax import lax
from jax.experimental import pallas as pl
from jax.experimental.pallas import tpu as pltpu

F32 = jnp.float32
BF16 = jnp.bfloat16
I32 = jnp.int32

D_MODEL = 2048
D_SSM = 1024
SSM_GROUP = 16
N_SSM_GROUPS = 64
SSM_STATE = 64
HEAD_DIM = 128
N_HEADS = 8
N_KV_HEADS = 2
KV_REP = N_HEADS // N_KV_HEADS
N_IDX_HEADS = 16
IDX_DIM = 64
TOPK_MAX = 256
ROPE_FRAC = 4
ROPE_THETA = 500000.0
D_FF = 4 * D_MODEL
EPS = 1e-6
N_MOD = 6

LANES = 128
VMEM_LIMIT = 56 * 1024 * 1024

ADA_TN = 1024
INPROJ_TM = 512
S5_CHUNK = 32
S5_GB = 16
DSA_TQ = 256
DSA_TK = 256
DSA_ONES = 16
OUT_TM = 256
MLP_TM = 1024
MLP_TF = 512

NEG_BIG = -1e30
INT_MIN = -2147483648
KEY_NEG_INF = -2139095041
KEY16_NEG_INF = KEY_NEG_INF >> 16
KEY16_BAND_LO = -129
KEY16_BAND_HI = 127
SHIFT_MARGIN = 1.01
MAX_SPAN = 80.0


def _cparams(sem):
    return pltpu.CompilerParams(dimension_semantics=sem, vmem_limit_bytes=VMEM_LIMIT)


def _const_spec(shape):
    nd = len(shape)
    return pl.BlockSpec(shape, lambda *_: (0,) * nd, pipeline_mode=pl.Buffered(1))


def _ada_kernel(c_ref, w_ref, b_ref, o_ref):
    c = c_ref[...]
    act = c / (1.0 + jnp.exp(-c))
    o_ref[...] = jnp.dot(act.astype(BF16), w_ref[...].astype(BF16),
                         preferred_element_type=F32) + b_ref[...]


def _ada_call(c_pad, w_ada, b_ada):
    rows, d = c_pad.shape
    n = w_ada.shape[1]
    return pl.pallas_call(
        _ada_kernel,
        grid=(n // ADA_TN,),
        in_specs=[pl.BlockSpec((rows, d), lambda j: (0, 0)),
                  pl.BlockSpec((d, ADA_TN), lambda j: (0, j)),
                  pl.BlockSpec((1, ADA_TN), lambda j: (0, j))],
        out_specs=pl.BlockSpec((rows, ADA_TN), lambda j: (0, j)),
        out_shape=jax.ShapeDtypeStruct((rows, n), F32),
        compiler_params=_cparams(("arbitrary",)),
        name="ada",
    )(c_pad, w_ada, b_ada)


def _rope(x, cos, sin_lo, sin_hi, half):
    n = x.shape[-1]
    return (x * cos + pltpu.roll(x, n - half, 1) * sin_lo + pltpu.roll(x, half, 1) * sin_hi)


def _inproj_kernel(x_ref, sc_ref, sh_ref, g_ref, pos_ref, inv_qk_ref, inv_ix_ref,
                   mlo_qk_ref, mhi_qk_ref, mlo_ix_ref, mhi_ix_ref, gq_ref, gk_ref,
                   wu_ref, wq_ref, wkv_ref, wqi_ref, wkw_ref,
                   u_out, q_out, k_out, v_out, qi_out, ki_out, wv_out):
    x = x_ref[...]
    ms = jnp.mean(x * x, axis=-1, keepdims=True)
    h = x * lax.rsqrt(ms + EPS) * g_ref[...]
    h = h * (1.0 + sc_ref[...]) + sh_ref[...]
    hb = h.astype(BF16)

    u_out[...] = jnp.dot(hb, wu_ref[...], preferred_element_type=F32).astype(BF16)

    pos = pos_ref[...].astype(F32)
    ang = pos * inv_qk_ref[...]
    cos_qk = jnp.cos(ang)
    sin_qk = jnp.sin(ang)
    slo_qk = -sin_qk * mlo_qk_ref[...]
    shi_qk = sin_qk * mhi_qk_ref[...]
    ang = pos * inv_ix_ref[...]
    cos_ix = jnp.cos(ang)
    sin_ix = jnp.sin(ang)
    slo_ix = -sin_ix * mlo_ix_ref[...]
    shi_ix = sin_ix * mhi_ix_ref[...]
    half_qk = HEAD_DIM // ROPE_FRAC // 2
    half_ix = IDX_DIM // ROPE_FRAC // 2

    def head_norm(t, g):
        return t * lax.rsqrt(jnp.mean(t * t, axis=-1, keepdims=True) + EPS) * g

    q = jnp.dot(hb, wq_ref[...], preferred_element_type=F32)
    attn_scale = HEAD_DIM ** -0.5
    for hd in range(N_HEADS):
        sl = slice(hd * HEAD_DIM, (hd + 1) * HEAD_DIM)
        t = _rope(head_norm(q[:, sl], gq_ref[...]), cos_qk, slo_qk, shi_qk, half_qk)
        q_out[:, sl] = (t * attn_scale).astype(BF16)

    kv = jnp.dot(hb, wkv_ref[...], preferred_element_type=F32)
    nk = N_KV_HEADS * HEAD_DIM
    one_col = (lax.broadcasted_iota(I32, (x.shape[0], HEAD_DIM), 1) == 0).astype(BF16)
    for hd in range(N_KV_HEADS):
        sl = slice(hd * HEAD_DIM, (hd + 1) * HEAD_DIM)
        t = _rope(head_norm(kv[:, sl], gk_ref[...]), cos_qk, slo_qk, shi_qk, half_qk)
        k_out[:, 2 * hd * HEAD_DIM:(2 * hd + 1) * HEAD_DIM] = t.astype(BF16)
        k_out[:, (2 * hd + 1) * HEAD_DIM:(2 * hd + 2) * HEAD_DIM] = one_col
    vrows = HEAD_DIM + DSA_ONES
    ones = jnp.ones((DSA_ONES, DSA_TK), BF16)
    for t in range(v_out.shape[0]):
        for hd in range(N_KV_HEADS):
            vh = kv[t * DSA_TK:(t + 1) * DSA_TK, nk + hd * HEAD_DIM:nk + (hd + 1) * HEAD_DIM]
            v_out[t, hd * vrows:hd * vrows + HEAD_DIM, :] = vh.T.astype(BF16)
            v_out[t, hd * vrows + HEAD_DIM:(hd + 1) * vrows, :] = ones

    qi = jnp.dot(hb, wqi_ref[...], preferred_element_type=F32)
    idx_scale = IDX_DIM ** -0.5
    for pr in range(N_IDX_HEADS * IDX_DIM // LANES):
        sl = slice(pr * LANES, (pr + 1) * LANES)
        t = _rope(qi[:, sl], cos_ix, slo_ix, shi_ix, half_ix)
        qi_out[:, sl] = (t * idx_scale).astype(BF16)

    kw = jnp.dot(hb, wkw_ref[...], preferred_element_type=F32)
    ki_out[...] = _rope(kw[:, :LANES], cos_ix, slo_ix, shi_ix, half_ix).astype(BF16)
    wv_out[...] = kw[:, LANES:] * (N_IDX_HEADS ** -0.5)


def _inproj_call(x2, sc, sh, g, pos2, tabs, gq, gk, wu, wq, wkv, wqi, wkw, seq):
    n, d = x2.shape
    tm = INPROJ_TM
    per_b = seq // tm
    row = lambda i: (i, 0)
    modmap = lambda i: (i // per_b, 0, 0)
    small = [_const_spec(t.shape) for t in tabs]
    in_specs = ([pl.BlockSpec((tm, d), row),
                 pl.BlockSpec((None, 1, d), modmap),
                 pl.BlockSpec((None, 1, d), modmap),
                 _const_spec(g.shape),
                 pl.BlockSpec((tm, 1), row)]
                + small
                + [_const_spec(gq.shape), _const_spec(gk.shape),
                   _const_spec(wu.shape), _const_spec(wq.shape), _const_spec(wkv.shape),
                   _const_spec(wqi.shape), _const_spec(wkw.shape)])
    widths = (D_SSM, N_HEADS * HEAD_DIM, 2 * N_KV_HEADS * HEAD_DIM, N_KV_HEADS * HEAD_DIM,
              N_IDX_HEADS * IDX_DIM, LANES, LANES)
    dtypes = (BF16, BF16, BF16, BF16, BF16, BF16, F32)
    vt_rows = N_KV_HEADS * (HEAD_DIM + DSA_ONES)
    out_specs = [pl.BlockSpec((tm, w), row) for w in widths]
    out_shape = [jax.ShapeDtypeStruct((n, w), dt) for w, dt in zip(widths, dtypes)]
    out_specs[3] = pl.BlockSpec((tm // DSA_TK, vt_rows, DSA_TK), lambda i: (i, 0, 0))
    out_shape[3] = jax.ShapeDtypeStruct((n // DSA_TK, vt_rows, DSA_TK), BF16)
    return pl.pallas_call(
        _inproj_kernel,
        grid=(n // tm,),
        in_specs=in_specs,
        out_specs=out_specs,
        out_shape=out_shape,
        compiler_params=_cparams(("parallel",)),
        name="inproj",
    )(x2, sc, sh, g, pos2, *tabs, gq, gk, wu, wq, wkv, wqi, wkw)


def _s5prep_kernel(lr_ref, li_ref, dt_ref, bre_ref, bim_ref, pr_ref, pi_ref, bbr_ref, bbi_ref):
    lr = jnp.minimum(lr_ref[...], -1e-4)
    li = li_ref[...]
    dt = dt_ref[...]
    mag = jnp.exp(lr * dt)
    ab_r = mag * jnp.cos(li * dt)
    ab_i = mag * jnp.sin(li * dt)
    den = lr * lr + li * li
    fr = ((ab_r - 1.0) * lr + ab_i * li) / den
    fi = (ab_i * lr - (ab_r - 1.0) * li) / den
    bbr_ref[...] = fr * bre_ref[...] - fi * bim_ref[...]
    bbi_ref[...] = fr * bim_ref[...] + fi * bre_ref[...]
    steps = (lax.broadcasted_iota(I32, pr_ref.shape, 0) + 1).astype(F32)
    magn = jnp.exp(steps * (lr * dt))
    pr_ref[...] = magn * jnp.cos(steps * (li * dt))
    pi_ref[...] = magn * jnp.sin(steps * (li * dt))


def _s5prep_call(lr, li, dt, bre_t, bim_t, chunk):
    gp = lr.shape[1]
    hh = bre_t.shape[0]
    outs = [jax.ShapeDtypeStruct((chunk, gp), F32)] * 2 + [jax.ShapeDtypeStruct((hh, gp), F32)] * 2
    return pl.pallas_call(_s5prep_kernel, out_shape=outs, name="s5prep")(lr, li, dt, bre_t, bim_t)


def _s5_kernel(u_ref, bw_ref, cw_ref, pw_ref, dsk_ref, y_ref, st_ref, cin_ref, xs_ref):
    tt, rr, _ = u_ref.shape
    ns = st_ref.shape[1] // 2
    ar = pw_ref[0:1, :ns]
    ai = pw_ref[0:1, ns:]
    st_ref[...] = jnp.zeros(st_ref.shape, F32)

    def cproj(zr, zi):
        return (jnp.dot(zr.astype(BF16), cw_ref[:ns, :], preferred_element_type=F32)
                + jnp.dot(zi.astype(BF16), cw_ref[ns:, :], preferred_element_type=F32))

    xs_ref[0] = jnp.dot(u_ref[0], bw_ref[...], preferred_element_type=F32)

    def local_step(i, carry):
        x = xs_ref[i % 2]
        nxt = jnp.minimum(i + 1, tt - 1)
        xs_ref[(i + 1) % 2] = jnp.dot(u_ref[nxt], bw_ref[...], preferred_element_type=F32)
        sr = st_ref[:, :ns]
        si = st_ref[:, ns:]
        nr = ar * sr - ai * si + x[:, :ns]
        ni = ar * si + ai * sr + x[:, ns:]
        st_ref[:, :ns] = nr
        st_ref[:, ns:] = ni
        y_ref[i] = (cproj(nr, ni) + dsk_ref[...] * u_ref[i].astype(F32)).astype(y_ref.dtype)
        return carry

    lax.fori_loop(0, tt, local_step, 0)

    atr = pw_ref[tt - 1:tt, :ns]
    ati = pw_ref[tt - 1:tt, ns:]

    def chain(r, carry):
        cr, ci = carry
        cin_ref[pl.ds(r, 1), :ns] = cr
        cin_ref[pl.ds(r, 1), ns:] = ci
        er = st_ref[pl.ds(r, 1), :ns]
        ei = st_ref[pl.ds(r, 1), ns:]
        return (atr * cr - ati * ci + er, atr * ci + ati * cr + ei)

    zero = jnp.zeros((1, ns), F32)
    lax.fori_loop(0, rr, chain, (zero, zero))

    def carry_step(i, carry):
        pr = pw_ref[pl.ds(i, 1), :ns]
        pi = pw_ref[pl.ds(i, 1), ns:]
        cr = cin_ref[:, :ns]
        ci = cin_ref[:, ns:]
        corr = cproj(pr * cr - pi * ci, pr * ci + pi * cr)
        y_ref[i] = (y_ref[i].astype(F32) + corr).astype(y_ref.dtype)
        return carry

    lax.fori_loop(0, tt, carry_step, 0)


def _s5_call(u_t, bw, cw, pw, dsk):
    bsz, tt, rr, dch = u_t.shape
    nb = bw.shape[0]
    cb = bw.shape[1]
    ns2 = bw.shape[2]
    return pl.pallas_call(
        _s5_kernel,
        grid=(bsz, nb),
        in_specs=[pl.BlockSpec((None, tt, rr, cb), lambda b, j: (b, 0, 0, j)),
                  pl.BlockSpec((None, cb, ns2), lambda b, j: (j, 0, 0)),
                  pl.BlockSpec((None, ns2, cb), lambda b, j: (j, 0, 0)),
                  pl.BlockSpec((None, tt, ns2), lambda b, j: (j, 0, 0)),
                  pl.BlockSpec((1, cb), lambda b, j: (0, j))],
        out_specs=pl.BlockSpec((None, tt, rr, cb), lambda b, j: (b, 0, 0, j)),
        out_shape=jax.ShapeDtypeStruct((bsz, tt, rr, dch), BF16),
        scratch_shapes=[pltpu.VMEM((rr, ns2), F32), pltpu.VMEM((rr, ns2), F32),
                        pltpu.VMEM((2, rr, ns2), F32)],
        compiler_params=_cparams(("parallel", "parallel")),
        name="s5scan",
    )(u_t, bw, cw, pw, dsk)


def _order_key(x):
    bits = pltpu.bitcast(x, I32)
    return jnp.where(bits < 0, bits ^ jnp.int32(0x7FFFFFFF), bits)


def _dsa_kernel(q_ref, qi_ref, wv_ref, k_ref, vt_ref, ki_ref, o_ref,
                key_sc, kb_sc, lhs_sc, wt_sc, qs_sc, m_sc, acc_sc, s_sc, p_sc, al_sc, kn_sm,
                *, topk):
    tq = q_ref.shape[0]
    tk = key_sc.shape[1]
    qb = pl.program_id(1)
    nkt = (qb * tq + tq + tk - 1) // tk

    lane = lax.broadcasted_iota(I32, (tq, LANES), 1)
    kiota = lax.broadcasted_iota(I32, (tk, tq), 0)
    qpos = qb * tq + lax.broadcasted_iota(I32, (tk, tq), 1)

    @pl.when(qb == 0)
    def _():
        rows = min(k_ref.shape[0], 1024)
        for g in range(N_KV_HEADS):
            def norm_max(c, best):
                r0 = pl.multiple_of(c * rows, rows)
                kk = k_ref[pl.ds(r0, rows), g * 2 * HEAD_DIM:g * 2 * HEAD_DIM + HEAD_DIM].astype(F32)
                return jnp.maximum(best, jnp.max(jnp.sum(kk * kk, axis=1, keepdims=True)))

            kn_sm[g] = lax.fori_loop(0, k_ref.shape[0] // rows, norm_max, jnp.float32(0.0))

    for hd in range(N_IDX_HEADS):
        pair = qi_ref[:, (hd // 2) * LANES:(hd // 2 + 1) * LANES].astype(F32)
        keep = (lane < IDX_DIM) if hd % 2 == 0 else (lane >= IDX_DIM)
        lhs_sc[hd] = jnp.where(keep, pair, 0.0).T.astype(BF16)
    wt_sc[...] = wv_ref[...].T

    def score_tiles(kt, ntile):
        k0 = pl.multiple_of(kt * tk, tk)
        kit = ki_ref[pl.ds(k0, ntile * tk), :]
        acc = jnp.zeros((ntile * tk, tq), F32)
        for hd in range(N_IDX_HEADS):
            lg = jnp.dot(kit, lhs_sc[hd], preferred_element_type=F32)
            acc = acc + jnp.maximum(lg, 0.0) * wt_sc[hd:hd + 1, :]
        for j in range(ntile):
            a = jnp.where(kiota + (k0 + j * tk) <= qpos, acc[j * tk:(j + 1) * tk], -jnp.inf)
            bits = pltpu.bitcast(a, I32)
            key_sc[kt + j] = jnp.where(bits < 0, bits ^ jnp.int32(0x7FFFFFFF), bits)
            kb_sc[kt + j] = pltpu.bitcast(bits & jnp.int32(-65536), F32).astype(BF16)

    def score_quad(kq, carry):
        score_tiles(4 * kq, 4)
        return carry

    lax.fori_loop(0, nkt // 4, score_quad, 0)

    @pl.when(nkt % 4 >= 2)
    def _():
        score_tiles((nkt // 4) * 4, 2)

    @pl.when(nkt % 2 == 1)
    def _():
        score_tiles(nkt - 1, 1)

    def search(count, bit0, t0, n0):
        def cond(st):
            bit, _, cnt = st
            return jnp.logical_and(bit >= 0, jnp.max(jnp.abs(cnt - topk)) > 0)

        def body(st):
            bit, t, cnt = st
            for _ in range(2):
                cand = t + jnp.left_shift(jnp.int32(1), jnp.maximum(bit, 0))
                c = count(cand)
                ok = jnp.logical_and(c >= topk, bit >= 0)
                t = jnp.where(ok, cand, t)
                cnt = jnp.where(ok, c, cnt)
                bit = bit - 1
            return bit, t, cnt

        _, t, cnt = lax.while_loop(cond, body, (jnp.asarray(bit0, I32), t0, n0))
        return t, cnt

    one_b = jnp.ones((tk, tq), BF16)
    zero_b = jnp.zeros((tk, tq), BF16)
    sub = tk // 16

    def count16(c16):
        c16 = jnp.maximum(c16, KEY16_NEG_INF)
        pat = jnp.where(c16 >= 0, c16, c16 ^ jnp.int32(0x7FFF))
        cand = pltpu.bitcast(jnp.left_shift(pat, 16), F32).astype(BF16)

        def hits(kt):
            hit = jnp.where(kb_sc[kt] >= cand, one_b, zero_b)
            part = hit[0:16]
            for j in range(1, sub):
                part = part + hit[j * 16:(j + 1) * 16]
            return part

        def quad(kq, acc):
            part = hits(4 * kq)
            for j in range(1, 4):
                part = part + hits(4 * kq + j)
            return acc + part.astype(F32)

        def single(kt, acc):
            return acc + hits(kt).astype(F32)

        acc = lax.fori_loop(0, nkt // 4, quad, jnp.zeros((16, tq), F32))
        acc = lax.fori_loop((nkt // 4) * 4, nkt, single, acc)
        return jnp.sum(acc, axis=0, keepdims=True).astype(I32)

    def count32(cand):
        def single(kt, acc):
            hit = jnp.where(key_sc[kt] >= cand, 1, 0).astype(I32)
            return acc + jnp.sum(hit.reshape(tk // 8, 8, tq), axis=0)

        def quad(kq, acc):
            for j in range(4):
                acc = single(4 * kq + j, acc)
            return acc

        acc = lax.fori_loop(0, nkt // 4, quad, jnp.zeros((8, tq), I32))
        acc = lax.fori_loop((nkt // 4) * 4, nkt, single, acc)
        return jnp.sum(acc, axis=0, keepdims=True)

    c0 = count16(jnp.zeros((1, tq), I32))
    pos_ok = c0 >= topk
    def upper_bit(i, st):
        t, cnt = st
        cand = t + jnp.left_shift(jnp.int32(1), 14 - i)
        c = count16(cand)
        ok = c >= topk
        return jnp.where(ok, cand, t), jnp.where(ok, c, cnt)

    t16, c16 = lax.fori_loop(0, 15, upper_bit,
                             (jnp.where(pos_ok, 0, -32768).astype(I32),
                              jnp.where(pos_ok, c0, nkt * tk).astype(I32)))
    band = jnp.logical_and(t16 >= KEY16_BAND_LO, t16 <= KEY16_BAND_HI)
    t32 = jnp.left_shift(jnp.where(band, KEY16_BAND_LO, t16), 16)
    c32 = jnp.where(band, nkt * tk + 1, c16).astype(I32)
    any_band = jnp.max(band.astype(I32)) > 0
    top_bit = jnp.where(any_band, 24, 15)

    def lower_bit(i, st):
        t, cnt = st
        cand = t + jnp.left_shift(jnp.int32(1), top_bit - i)
        c = count32(cand)
        ok = c >= topk
        return jnp.where(ok, cand, t), jnp.where(ok, c, cnt)

    t32, c32 = lax.fori_loop(0, top_bit - 9, lower_bit, (t32, c32))
    thr, cnt = search(count32, 9, t32, c32)
    thr = jnp.maximum(thr, KEY_NEG_INF + 1)

    @pl.when(jnp.max(cnt) > topk)
    def _():
        need = topk - count32(thr + 1)
        tri = (lax.broadcasted_iota(I32, (tk, tk), 1)
               <= lax.broadcasted_iota(I32, (tk, tk), 0)).astype(BF16)

        def demote(kt, seen):
            keys = key_sc[kt]
            eq = keys == thr
            rank = seen + jnp.dot(tri, jnp.where(eq, 1.0, 0.0).astype(BF16),
                                  preferred_element_type=F32)
            key_sc[kt] = jnp.where(jnp.logical_and(eq, rank > need.astype(F32)), thr - 1, keys)
            return rank[tk - 1:tk, :]

        lax.fori_loop(0, nkt, demote, jnp.zeros((1, tq), F32))

    kv_w = 2 * HEAD_DIM
    shifts = []
    for g in range(N_KV_HEADS):
        parts = []
        for r in range(KV_REP):
            hd = g * KV_REP + r
            qht = q_ref[:, hd * HEAD_DIM:(hd + 1) * HEAD_DIM].astype(F32).T
            qs_sc[g, :HEAD_DIM, r * tq:(r + 1) * tq] = qht.astype(BF16)
            qn2 = jnp.sum(qht * qht, axis=0, keepdims=True)
            parts.append(jnp.sqrt(qn2 * kn_sm[g]) * SHIFT_MARGIN)
        shifts.append(jnp.concatenate(parts, axis=1))
    bound = jnp.maximum(jnp.max(shifts[0]), jnp.max(shifts[1]))
    fast = 2.0 * bound <= MAX_SPAN
    row0 = lax.broadcasted_iota(I32, (HEAD_DIM, KV_REP * tq), 0) == 0
    for g in range(N_KV_HEADS):
        shift_row = jnp.where(fast, -shifts[g], 0.0)
        qs_sc[g, HEAD_DIM:, :] = jnp.where(row0, shift_row, 0.0).astype(BF16)

    vrows = acc_sc.shape[1]

    def qk(kt, g):
        k0 = pl.multiple_of(kt * tk, tk)
        kg = k_ref[pl.ds(k0, tk), g * kv_w:(g + 1) * kv_w]
        s_sc[g] = jnp.dot(kg, qs_sc[g], preferred_element_type=F32)

    def softmax_shifted(g, bias):
        p_sc[g] = jnp.exp(s_sc[g] + bias).astype(BF16)

    def pv_shifted(kt, g):
        vtg = vt_ref[kt, g * vrows:(g + 1) * vrows, :]
        acc_sc[g] += jnp.dot(vtg, p_sc[g], preferred_element_type=F32)

    def softmax_online(g, bias):
        s = s_sc[g] + bias
        m_old = m_sc[g]
        m_new = jnp.maximum(m_old, jnp.max(s, axis=0, keepdims=True))
        al_sc[g] = jnp.exp(m_old - m_new)
        p_sc[g] = jnp.exp(s - m_new).astype(BF16)
        m_sc[g] = m_new

    def pv_online(kt, g):
        vtg = vt_ref[kt, g * vrows:(g + 1) * vrows, :]
        acc_sc[g] = al_sc[g] * acc_sc[g] + jnp.dot(vtg, p_sc[g], preferred_element_type=F32)

    def attend(softmax, pv, per_trip):
        m_sc[...] = jnp.full(m_sc.shape, NEG_BIG, F32)
        acc_sc[...] = jnp.zeros(acc_sc.shape, F32)
        p_sc[1] = jnp.zeros(p_sc.shape[1:], BF16)
        al_sc[1] = jnp.ones(al_sc.shape[1:], F32)

        def attn_tile(kt, carry):
            bias = jnp.where(key_sc[kt] >= thr, 0.0, NEG_BIG)
            bias = jnp.concatenate([bias] * KV_REP, axis=1)
            qk(kt, 1)
            softmax(0, bias)
            pv(jnp.maximum(kt - 1, 0), 1)
            qk(jnp.minimum(kt + 1, nkt - 1), 0)
            softmax(1, bias)
            pv(kt, 0)
            return carry

        def attn_group(kg, carry):
            for j in range(per_trip):
                attn_tile(per_trip * kg + j, carry)
            return carry

        qk(0, 0)
        lax.fori_loop(0, nkt // per_trip, attn_group, 0)
        lax.fori_loop((nkt // per_trip) * per_trip, nkt, attn_tile, 0)
        pv(nkt - 1, 1)

    @pl.when(fast)
    def _():
        attend(softmax_shifted, pv_shifted, 4)

    @pl.when(jnp.logical_not(fast))
    def _():
        attend(softmax_online, pv_online, 1)

    for g in range(N_KV_HEADS):
        out = acc_sc[g, :HEAD_DIM, :] / acc_sc[g, HEAD_DIM:HEAD_DIM + 1, :]
        for r in range(KV_REP):
            hd = g * KV_REP + r
            o_ref[:, hd * HEAD_DIM:(hd + 1) * HEAD_DIM] = out[:, r * tq:(r + 1) * tq].T.astype(o_ref.dtype)


def _dsa_call(q, qi, wv, k, vt, ki, topk):
    bsz, seq, dq = q.shape
    tq = DSA_TQ
    nkt, dv, tk = vt.shape[1:]
    nq = seq // tq
    qmap = lambda b, i: (b, i, 0)

    def whole(shape):
        return pl.BlockSpec((None,) + tuple(shape), lambda b, i: (b,) + (0,) * len(shape),
                            pipeline_mode=pl.Buffered(1))

    return pl.pallas_call(
        functools.partial(_dsa_kernel, topk=topk),
        grid=(bsz, nq),
        in_specs=[pl.BlockSpec((None, tq, dq), qmap),
                  pl.BlockSpec((None, tq, qi.shape[2]), qmap),
                  pl.BlockSpec((None, tq, wv.shape[2]), qmap),
                  whole(k.shape[1:]), whole(vt.shape[1:]), whole(ki.shape[1:])],
        out_specs=pl.BlockSpec((None, tq, dq), qmap),
        out_shape=jax.ShapeDtypeStruct((bsz, seq, dq), BF16),
        scratch_shapes=[pltpu.VMEM((nkt, tk, tq), I32),
                        pltpu.VMEM((nkt, tk, tq), BF16),
                        pltpu.VMEM((N_IDX_HEADS, LANES, tq), BF16),
                        pltpu.VMEM((LANES, tq), F32),
                        pltpu.VMEM((N_KV_HEADS, 2 * HEAD_DIM, KV_REP * tq), BF16),
                        pltpu.VMEM((N_KV_HEADS, 1, KV_REP * tq), F32),
                        pltpu.VMEM((N_KV_HEADS, dv // N_KV_HEADS, KV_REP * tq), F32),
                        pltpu.VMEM((N_KV_HEADS, tk, KV_REP * tq), F32),
                        pltpu.VMEM((N_KV_HEADS, tk, KV_REP * tq), BF16),
                        pltpu.VMEM((N_KV_HEADS, 1, KV_REP * tq), F32),
                        pltpu.SMEM((N_KV_HEADS,), F32)],
        compiler_params=_cparams(("parallel", "arbitrary")),
        name="dsa",
    )(q, qi, wv, k, vt, ki)


def _outproj_kernel(x_ref, ys_ref, ya_ref, gt_ref, sc_ref, sh_ref, wglu_ref, bglu_ref,
                    gs_ref, ga_ref, wout_ref, gm_ref, x1_out, h2_out):
    def rms(t, g):
        return t * lax.rsqrt(jnp.mean(t * t, axis=-1, keepdims=True) + EPS) * g

    y = ys_ref[...].astype(F32)
    ya = 0.5 * y * (1.0 + jnp.tanh(math.sqrt(2.0 / math.pi) * (y + 0.044715 * (y * y * y))))
    z = jnp.dot(ya.astype(BF16), wglu_ref[...], preferred_element_type=F32) + bglu_ref[...]
    o = ya / (1.0 + jnp.exp(-z))
    n1 = rms(o, gs_ref[...]).astype(BF16)
    n2 = rms(ya_ref[...].astype(F32), ga_ref[...]).astype(BF16)
    d1 = n1.shape[1]
    mixw = (jnp.dot(n1, wout_ref[:d1, :], preferred_element_type=F32)
            + jnp.dot(n2, wout_ref[d1:, :], preferred_element_type=F32))
    x1 = x_ref[...] + gt_ref[...] * mixw
    x1_out[...] = x1
    h2 = rms(x1, gm_ref[...]) * (1.0 + sc_ref[...]) + sh_ref[...]
    h2_out[...] = h2.astype(BF16)


def _outproj_call(x2, ys, ya, gt, sc, sh, wglu, bglu, gs, ga, wout, gm, seq):
    n, d = x2.shape
    tm = OUT_TM
    per_b = seq // tm
    row = lambda i: (i, 0)
    modmap = lambda i: (i // per_b, 0, 0)
    return pl.pallas_call(
        _outproj_kernel,
        grid=(n // tm,),
        in_specs=[pl.BlockSpec((tm, d), row),
                  pl.BlockSpec((tm, ys.shape[1]), row),
                  pl.BlockSpec((tm, ya.shape[1]), row),
                  pl.BlockSpec((None, 1, d), modmap),
                  pl.BlockSpec((None, 1, d), modmap),
                  pl.BlockSpec((None, 1, d), modmap),
                  _const_spec(wglu.shape), _const_spec(bglu.shape),
                  _const_spec(gs.shape), _const_spec(ga.shape),
                  _const_spec(wout.shape), _const_spec(gm.shape)],
        out_specs=[pl.BlockSpec((tm, d), row), pl.BlockSpec((tm, d), row)],
        out_shape=[jax.ShapeDtypeStruct((n, d), F32), jax.ShapeDtypeStruct((n, d), BF16)],
        compiler_params=_cparams(("parallel",)),
        name="outproj",
    )(x2, ys, ya, gt, sc, sh, wglu, bglu, gs, ga, wout, gm)


def _mlp_kernel(h_ref, x1_hbm, gt_ref, w1_ref, w2_ref, o_ref, x1_buf, x1_sem):
    i = pl.program_id(0)
    j = pl.program_id(1)
    tm = o_ref.shape[0]

    def x1_copy():
        rows = pl.ds(pl.multiple_of(i * tm, tm), tm)
        return pltpu.make_async_copy(x1_hbm.at[rows, :], x1_buf, x1_sem)

    @pl.when(j == 0)
    def _():
        x1_copy().start()
        o_ref[...] = jnp.zeros(o_ref.shape, F32)

    a = jnp.maximum(jnp.dot(h_ref[...], w1_ref[...], preferred_element_type=F32), 0.0)
    o_ref[...] += jnp.dot((a * a).astype(BF16), w2_ref[...], preferred_element_type=F32)

    @pl.when(j == pl.num_programs(1) - 1)
    def _():
        x1_copy().wait()
        o_ref[...] = x1_buf[...] + gt_ref[...] * o_ref[...]


def _mlp_call(h2, x1, gt, w1, w2, seq):
    n, d = x1.shape
    dff = w1.shape[1]
    tm, tf = MLP_TM, MLP_TF
    per_b = seq // tm
    return pl.pallas_call(
        _mlp_kernel,
        grid=(n // tm, dff // tf),
        in_specs=[pl.BlockSpec((tm, d), lambda i, j: (i, 0)),
                  pl.BlockSpec(memory_space=pl.ANY),
                  pl.BlockSpec((None, 1, d), lambda i, j: (i // per_b, 0, 0)),
                  pl.BlockSpec((d, tf), lambda i, j: (0, j)),
                  pl.BlockSpec((tf, d), lambda i, j: (j, 0))],
        out_specs=pl.BlockSpec((tm, d), lambda i, j: (i, 0)),
        out_shape=jax.ShapeDtypeStruct((n, d), F32),
        scratch_shapes=[pltpu.VMEM((tm, d), F32), pltpu.SemaphoreType.DMA(())],
        compiler_params=_cparams(("parallel", "arbitrary")),
        name="mlp",
    )(h2, x1, gt, w1, w2)


def _rope_tables():
    def tables(head_dim, per_vreg):
        r = head_dim // ROPE_FRAC
        half = r // 2
        inv = ROPE_THETA ** (-jnp.arange(half, dtype=F32) / half)
        lane = np.arange(LANES) % head_dim
        inv_l = jnp.where(lane < r, inv[lane % half], 0.0).astype(F32)[None, :]
        lo = jnp.asarray((lane < half).astype(np.float32))[None, :]
        hi = jnp.asarray(((lane >= half) & (lane < r)).astype(np.float32))[None, :]
        return inv_l, lo, hi

    inv_qk, lo_qk, hi_qk = tables(HEAD_DIM, 1)
    inv_ix, lo_ix, hi_ix = tables(IDX_DIM, 2)
    return [inv_qk, inv_ix, lo_qk, hi_qk, lo_ix, hi_ix]


def _block_diag(m):
    nb, g, a, b = m.shape
    eye = jnp.eye(g, dtype=m.dtype)
    return jnp.einsum('ngab,gk->ngakb', m, eye).reshape(nb, g * a, g * b)


def kernel(x, c, positions, w_ada, b_ada, g_norm_mix, w_in, lam_re, lam_im, log_dt,
           b_re, b_im, c_re, c_im, d_skip, w_glu, b_glu, g_q, g_k, g_out_ssm, g_out_attn,
           w_out, g_norm_mlp, w_mlp_in, w_mlp_out):
    bsz, seq, d = x.shape
    depth = w_ada.shape[0]
    n = bsz * seq
    topk = min(TOPK_MAX, seq // 4)
    gg, pp, hh = N_SSM_GROUPS, SSM_STATE, SSM_GROUP
    nblk = gg // S5_GB
    chunk = S5_CHUNK
    nchunks = seq // chunk

    c_pad = jnp.zeros((8, d), F32).at[:bsz].set(c.astype(F32))
    pos2 = positions.reshape(n, 1).astype(I32)
    tabs = _rope_tables()
    xcur = x.reshape(n, d).astype(F32)

    for l in range(depth):
        mod = _ada_call(c_pad, w_ada[l], b_ada[l][None, :])[:bsz]
        mod = mod.reshape(bsz, N_MOD, 1, d)
        sh_a, sc_a, gt_a, sh_m, sc_m, gt_m = [mod[:, i] for i in range(N_MOD)]

        wi = w_in[l]
        o0 = D_SSM
        o1 = o0 + N_HEADS * HEAD_DIM
        o2 = o1 + 2 * N_KV_HEADS * HEAD_DIM
        o3 = o2 + N_IDX_HEADS * IDX_DIM
        o4 = o3 + IDX_DIM
        w_ki = wi[:, o3:o4]
        w_wi = wi[:, o4:]
        w_kw = jnp.concatenate(
            [w_ki, w_ki, w_wi, jnp.zeros((d, 2 * LANES - 2 * IDX_DIM - N_IDX_HEADS), wi.dtype)], axis=1)
        u, q, k, v, qi, ki, wv = _inproj_call(
            xcur, sc_a, sh_a, g_norm_mix[l][None, :], pos2, tabs,
            g_q[l][None, :], g_k[l][None, :],
            wi[:, :o0].astype(BF16), wi[:, o0:o1].astype(BF16), wi[:, o1:o2].astype(BF16),
            wi[:, o2:o3].astype(BF16), w_kw.astype(BF16), seq)

        lr = lam_re[l].astype(F32).reshape(1, gg * pp)
        li = lam_im[l].astype(F32).reshape(1, gg * pp)
        dt = jnp.broadcast_to(jnp.exp(log_dt[l].astype(F32))[:, None], (gg, pp)).reshape(1, gg * pp)
        bre_t = b_re[l].astype(F32).transpose(2, 0, 1).reshape(hh, gg * pp)
        bim_t = b_im[l].astype(F32).transpose(2, 0, 1).reshape(hh, gg * pp)
        pw_r, pw_i, bb_r, bb_i = _s5prep_call(lr, li, dt, bre_t, bim_t, chunk)
        bbr = bb_r.reshape(hh, nblk, S5_GB, pp).transpose(1, 2, 0, 3)
        bbi = bb_i.reshape(hh, nblk, S5_GB, pp).transpose(1, 2, 0, 3)
        bw = jnp.concatenate([_block_diag(bbr), _block_diag(bbi)], axis=2).astype(BF16)
        cr = c_re[l].astype(F32).reshape(nblk, S5_GB, hh, pp).transpose(0, 1, 3, 2)
        ci = c_im[l].astype(F32).reshape(nblk, S5_GB, hh, pp).transpose(0, 1, 3, 2)
        cw = jnp.concatenate([_block_diag(cr), -_block_diag(ci)], axis=1).astype(BF16)
        ns = S5_GB * pp
        pw = jnp.concatenate([pw_r.reshape(chunk, nblk, ns), pw_i.reshape(chunk, nblk, ns)],
                             axis=2).transpose(1, 0, 2)
        u_t = u.reshape(bsz, nchunks, chunk, D_SSM).transpose(0, 2, 1, 3)
        y_t = _s5_call(u_t, bw, cw, pw, d_skip[l].astype(F32)[None, :])
        y_ssm = y_t.transpose(0, 2, 1, 3).reshape(n, D_SSM)

        v_t = v.reshape(bsz, seq // DSA_TK, v.shape[1], DSA_TK)
        y_attn = _dsa_call(q.reshape(bsz, seq, -1), qi.reshape(bsz, seq, -1),
                           wv.reshape(bsz, seq, -1), k.reshape(bsz, seq, -1),
                           v_t, ki.reshape(bsz, seq, -1), topk)
        y_attn = y_attn.reshape(n, -1)

        x1, h2 = _outproj_call(
            xcur, y_ssm, y_attn, gt_a, sc_m, sh_m,
            w_glu[l].astype(BF16), b_glu[l].astype(F32)[None, :],
            g_out_ssm[l].astype(F32)[None, :], g_out_attn[l].astype(F32)[None, :],
            w_out[l].astype(BF16), g_norm_mlp[l].astype(F32)[None, :], seq)

        xcur = _mlp_call(h2, x1, gt_m, w_mlp_in[l].astype(BF16), w_mlp_out[l].astype(BF16), seq)

    return xcur.reshape(bsz, seq, d).astype(x.dtype)
```
